```python
import jax
import jax.numpy as jnp
from jax import lax
import numpy as np

D_MODEL = 2048
BATCH = 16
SEQ = 2048
DEPTH = 4

GRID_W = 64
CTX_LEN = 256
N_MIXERS = 2
N_RWKV = (DEPTH + N_MIXERS - 1) // N_MIXERS
N_MLSTM = DEPTH // N_MIXERS
N_DIRS = 2
NORM_EPS = 1e-6

RWKV_HEAD = 64
RWKV_HEADS = D_MODEL // RWKV_HEAD
DECAY_LORA = 96
ICLR_LORA = 96
VRES_LORA = 64
GATE_LORA = 256
RWKV_GN_EPS = RWKV_HEAD * 1e-5

MLSTM_HEADS = 8
MLSTM_DV = D_MODEL // MLSTM_HEADS
MLSTM_DK = MLSTM_DV // 2
MLSTM_QK = MLSTM_HEADS * MLSTM_DK
MLSTM_V = MLSTM_HEADS * MLSTM_DV
MLSTM_PROJ = 2 * MLSTM_QK + 2 * MLSTM_V + N_DIRS * 2 * MLSTM_HEADS
CHUNK = 64
GATE_CAP = 15.0

D_FF = ((8 * D_MODEL + 767) // 768) * 256

kernel_name = 'hybrid_rwkv7_mlstm_flow_trunk'


def _rmsnorm(x, g):
    x32 = x.astype(jnp.float32)
    y = x32 * lax.rsqrt(jnp.mean(x32 * x32, axis=-1, keepdims=True) + NORM_EPS)
    return (y * g.astype(jnp.float32)).astype(x.dtype)


def _head_layernorm(y, n_heads, eps):
    shp = y.shape
    y32 = y.astype(jnp.float32).reshape(shp[:-1] + (n_heads, shp[-1] // n_heads))
    mu = jnp.mean(y32, axis=-1, keepdims=True)
    var = jnp.mean(jnp.square(y32 - mu), axis=-1, keepdims=True)
    return ((y32 - mu) * lax.rsqrt(var + eps)).reshape(shp)


def _grid_shift(h):
    b, t, d = h.shape
    rows = t // GRID_W
    q = d // 4
    g = h.reshape(b, rows, GRID_W, d)
    left = jnp.pad(g[:, :, :-1, :q], ((0, 0), (0, 0), (1, 0), (0, 0)))
    right = jnp.pad(g[:, :, 1:, q:2 * q], ((0, 0), (0, 0), (0, 1), (0, 0)))
    up = jnp.pad(g[:, :-1, :, 2 * q:3 * q], ((0, 0), (1, 0), (0, 0), (0, 0)))
    down = jnp.pad(g[:, 1:, :, 3 * q:], ((0, 0), (0, 1), (0, 0), (0, 0)))
    return jnp.concatenate([left, right, up, down], axis=-1).reshape(b, t, d)


def _seq_shift(h):
    half = h.shape[-1] // 2
    prev = jnp.pad(h[:, :-1, :half], ((0, 0), (1, 0), (0, 0)))
    nxt = jnp.pad(h[:, 1:, half:], ((0, 0), (0, 1), (0, 0)))
    return jnp.concatenate([prev, nxt], axis=-1)


def _dwconv_grid(u, w, bias):
    b, t, ch = u.shape
    rows = t // GRID_W
    y = lax.conv_general_dilated(u.reshape(b, rows, GRID_W, ch), w[:, :, None, :].astype(u.dtype),
                                 (1, 1), 'SAME', dimension_numbers=('NHWC', 'HWIO', 'NHWC'),
                                 feature_group_count=ch)
    return y.reshape(b, t, ch) + bias


def _dwconv_seq(u, w, bias):
    ch = u.shape[-1]
    y = lax.conv_general_dilated(u, w[:, None, :].astype(u.dtype), (1,), 'SAME',
                                 dimension_numbers=('NWC', 'WIO', 'NWC'), feature_group_count=ch)
    return y + bias


def _to_dirs(c_fwd, x_fwd, c_bwd, x_bwd):
    fwd = jnp.concatenate([c_fwd, x_fwd], axis=1)
    bwd = jnp.concatenate([jnp.flip(c_bwd, 1), jnp.flip(x_bwd, 1)], axis=1)
    return jnp.stack([fwd, bwd], axis=0)


def _from_dirs(y, l):
    yf, yb = y[0], y[1]
    return (yf[:, :l] + jnp.flip(yb[:, :l], 1), yf[:, l:] + jnp.flip(yb[:, l:], 1))


def _swiglu(h, w_in, w_out):
    u = h @ w_in
    return (jax.nn.silu(u[..., :D_FF]) * u[..., D_FF:]) @ w_out


def _rwkv7_scan(r, decay, k, v, z, b):
    nz, nb, tt, nh, n = r.shape
    tfirst = lambda a: jnp.moveaxis(a, 2, 0).astype(jnp.float32)

    def step(s, inp):
        r_t, w_t, k_t, v_t, z_t, b_t = inp
        sz = jnp.einsum('zbhvk,zbhk->zbhv', s, z_t)
        s = s * w_t[..., None, :] + sz[..., :, None] * b_t[..., None, :] + v_t[..., :, None] * k_t[..., None, :]
        return s, jnp.einsum('zbhvk,zbhk->zbhv', s, r_t)

    s0 = jnp.zeros((nz, nb, nh, n, n), jnp.float32)
    _, y = lax.scan(step, s0, (tfirst(r), tfirst(decay), tfirst(k), tfirst(v), tfirst(z), tfirst(b)))
    return jnp.moveaxis(y, 0, 2)


def _rwkv7_project(h, xx, p, v_first):
    xr, xw, xk, xv, xa, xg = [h + xx * p['mu'][n] for n in range(6)]
    r = xr @ p['w_r']
    k = xk @ p['w_k']
    v = xv @ p['w_v']
    if v_first is not None:
        v = v + (v_first - v) * jax.nn.sigmoid(p['v0'] + (xv @ p['v1']) @ p['v2'])
    w_pre = p['w0'][:, None, None, :] + jnp.einsum(
        'zbtr,zrd->zbtd', jnp.tanh(jnp.einsum('btd,zdr->zbtr', xw, p['w1'])), p['w2'])
    decay = jnp.exp(-jnp.exp(-jax.nn.softplus(-w_pre.astype(jnp.float32)) - 0.5))
    a = jax.nn.sigmoid(p['a0'][:, None, None, :] + jnp.einsum(
        'zbtr,zrd->zbtd', jnp.einsum('btd,zdr->zbtr', xa, p['a1']), p['a2']))
    g = jax.nn.sigmoid(xg @ p['g1']) @ p['g2']
    kk = (k * p['k_k']).astype(jnp.float32).reshape(k.shape[:-1] + (RWKV_HEADS, RWKV_HEAD))
    kk = (kk / jnp.maximum(jnp.linalg.norm(kk, axis=-1, keepdims=True), 1e-12)).reshape(k.shape).astype(k.dtype)
    k_mod = k * (1 + (a - 1) * p['k_a'])
    return r, decay, k_mod, v, -kk, kk * a, g


def _rwkv7_mixer(hx, hc, p, v_first, need_ctx):
    nb, t, d = hx.shape
    l = hc.shape[1]
    vf_c, vf_x = (None, None) if v_first is None else v_first
    rx, dx, kx, vx, zx, bx, gx = _rwkv7_project(hx, _grid_shift(hx) - hx, p, vf_x)
    rc, dc, kc, vc, zc, bc, gc = _rwkv7_project(hc, _seq_shift(hc) - hc, p, vf_c)
    heads = lambda a: a.reshape(a.shape[:-1] + (RWKV_HEADS, RWKV_HEAD))
    shared = lambda u_c, u_x: heads(_to_dirs(u_c, u_x, u_c, u_x))
    split = lambda u_c, u_x: heads(_to_dirs(u_c[0], u_x[0], u_c[1], u_x[1]))
    y = _rwkv7_scan(shared(rc, rx), split(dc, dx), split(kc, kx), shared(vc, vx), shared(zc, zx), split(bc, bx))
    y_c, y_x = _from_dirs(y.reshape(N_DIRS, nb, l + t, d), l)

    def readout(y_s, r, k_mod, v, g):
        yn = _head_layernorm(y_s, RWKV_HEADS, RWKV_GN_EPS) * p['ln_w'] + p['ln_b']
        rk = heads(r[None] * k_mod * p['r_k'].reshape(-1)).sum(-1).sum(0)
        bonus = (rk[..., None] * heads(v)).reshape(v.shape)
        return ((yn + bonus) * g).astype(hx.dtype) @ p['w_o']

    out_x = readout(y_x, rx, kx, vx, gx)
    out_c = readout(y_c, rc, kc, vc, gc) if need_ctx else None
    return out_x, out_c, (vc, vx)


def _mlstm_chunkwise(q, k, v, ig, lf):
    nz, nb, tt, nh, dk = q.shape
    dv = v.shape[-1]
    nc = tt // CHUNK

    def chunks(a):
        a = a.astype(jnp.float32).reshape((nz, nb, nc, CHUNK, nh) + a.shape[4:])
        return jnp.swapaxes(jnp.moveaxis(a, 2, 0), 3, 4)

    causal = jnp.tril(jnp.ones((CHUNK, CHUNK), dtype=bool))

    def step(carry, inp):
        c_st, n_st, m_st = carry
        qc, kc, vc, ic, fc = inp
        bcum = jnp.cumsum(fc, axis=-1)
        log_d = jnp.where(causal, bcum[..., :, None] - bcum[..., None, :] + ic[..., None, :], -jnp.inf)
        log_inter = bcum + m_st[..., None]
        m_t = jnp.maximum(log_inter, jnp.max(log_d, axis=-1))
        s = jnp.einsum('zbhtd,zbhjd->zbhtj', qc, kc) * jnp.exp(log_d - m_t[..., None])
        w_inter = jnp.exp(log_inter - m_t)
        num = w_inter[..., None] * jnp.einsum('zbhtd,zbhde->zbhte', qc, c_st) + jnp.einsum('zbhtj,zbhje->zbhte', s, vc)
        den = w_inter * jnp.einsum('zbhtd,zbhd->zbht', qc, n_st) + jnp.sum(s, axis=-1)
        h_out = num / jnp.maximum(jnp.abs(den), jnp.exp(-m_t))[..., None]
        b_end = bcum[..., -1]
        a_j = b_end[..., None] - bcum + ic
        m_new = jnp.maximum(b_end + m_st, jnp.max(a_j, axis=-1))
        wk = jnp.exp(a_j - m_new[..., None])[..., None] * kc
        dec = jnp.exp(b_end + m_st - m_new)
        c_st = dec[..., None, None] * c_st + jnp.einsum('zbhjd,zbhje->zbhde', wk, vc)
        n_st = dec[..., None] * n_st + jnp.sum(wk, axis=-2)
        return (c_st, n_st, m_new), h_out

    init = (jnp.zeros((nz, nb, nh, dk, dv), jnp.float32), jnp.zeros((nz, nb, nh, dk), jnp.float32),
            jnp.zeros((nz, nb, nh), jnp.float32))
    _, hs = lax.scan(step, init, (chunks(q), chunks(k), chunks(v), chunks(ig), chunks(lf)))
    hs = jnp.moveaxis(jnp.swapaxes(hs, 3, 4), 0, 2)
    return hs.reshape(nz, nb, tt, nh, dv).astype(v.dtype)


def _mlstm_mixer(hx, hc, p, need_ctx):
    nb, t, d = hx.shape
    l = hc.shape[1]

    def project(h, conv):
        u = h @ p['w_in']
        qk = jax.nn.silu(conv(u[..., :2 * MLSTM_QK]))
        v = u[..., 2 * MLSTM_QK:2 * MLSTM_QK + MLSTM_V]
        o = u[..., 2 * MLSTM_QK + MLSTM_V:2 * MLSTM_QK + 2 * MLSTM_V]
        gates = u[..., 2 * MLSTM_QK + 2 * MLSTM_V:].reshape(u.shape[:-1] + (N_DIRS, 2, MLSTM_HEADS)) + p['b_gate']
        gates = GATE_CAP * jnp.tanh(gates.astype(jnp.float32) / GATE_CAP)
        q = qk[..., :MLSTM_QK] * (MLSTM_DK ** -0.5)
        k = qk[..., MLSTM_QK:]
        return q, k, v, o, gates[..., 0, :], jax.nn.log_sigmoid(gates[..., 1, :])

    qx, kx, vx, ox, ix, fx = project(hx, lambda a: _dwconv_grid(a, p['conv_w'], p['conv_b']))
    qc, kc, vc, oc, ic, fc = project(hc, lambda a: _dwconv_seq(a, p['conv_w'][1], p['conv_b']))
    hd = lambda a, dh: a.reshape(a.shape[:-1] + (MLSTM_HEADS, dh))
    q = hd(_to_dirs(qc, qx, qc, qx), MLSTM_DK)
    k = hd(_to_dirs(kc, kx, kc, kx), MLSTM_DK)
    v = hd(_to_dirs(vc, vx, vc, vx), MLSTM_DV)
    ig = _to_dirs(ic[:, :, 0], ix[:, :, 0], ic[:, :, 1], ix[:, :, 1])
    lf = _to_dirs(fc[:, :, 0], fx[:, :, 0], fc[:, :, 1], fx[:, :, 1])
    h = _mlstm_chunkwise(q, k, v, ig, lf).reshape(N_DIRS, nb, l + t, MLSTM_V)
    h_c, h_x = _from_dirs(h, l)

    def readout(h_s, o):
        hn = _head_layernorm(h_s, MLSTM_HEADS, NORM_EPS) * p['norm_w']
        return (hn * jax.nn.sigmoid(o.astype(jnp.float32))).astype(hx.dtype) @ p['w_out']

    out_x = readout(h_x, ox)
    out_c = readout(h_c, oc) if need_ctx else None
    return out_x, out_c


def setup_inputs(seed: int = 0) -> dict:
    key = jax.random.key(seed)
    ks = iter(jax.random.split(key, 64))
    d = D_MODEL
    na, nbl = N_RWKV, N_MLSTM
    nrm = lambda shape, scale: jax.random.normal(next(ks), shape, jnp.float32) * scale
    uni = lambda shape, lo, hi: jax.random.uniform(next(ks), shape, jnp.float32, lo, hi)
    b_gate = jnp.stack([nrm((nbl, N_DIRS, MLSTM_HEADS), 0.1),
                        uni((nbl, N_DIRS, MLSTM_HEADS), 3.0, 6.0)], axis=2)
    return {
        'x': nrm((BATCH, SEQ, d), 1.0),
        'c': nrm((BATCH, d), 1.0),
        'ctx': nrm((BATCH, CTX_LEN, d), 1.0),
        'c_ctx': nrm((d,), 1.0),
        'mod_w': nrm((DEPTH, d, 6 * d), 0.5 * d ** -0.5),
        'mod_b': nrm((DEPTH, 6 * d), 0.02),
        'norm_g': 1.0 + nrm((DEPTH, 2, d), 0.02),
        'final_g': 1.0 + nrm((d,), 0.02),
        'rwkv_mu': uni((na, 6, d), 0.0, 1.0),
        'rwkv_w_r': nrm((na, d, d), d ** -0.5),
        'rwkv_w_k': nrm((na, d, d), d ** -0.5),
        'rwkv_w_v': nrm((na, d, d), d ** -0.5),
        'rwkv_w_o': nrm((na, d, d), d ** -0.5),
        'rwkv_w0': uni((na, N_DIRS, d), -6.0, -0.5),
        'rwkv_w1': nrm((na, N_DIRS, d, DECAY_LORA), d ** -0.5),
        'rwkv_w2': nrm((na, N_DIRS, DECAY_LORA, d), 0.5 * DECAY_LORA ** -0.5),
        'rwkv_a0': nrm((na, N_DIRS, d), 0.3),
        'rwkv_a1': nrm((na, N_DIRS, d, ICLR_LORA), d ** -0.5),
        'rwkv_a2': nrm((na, N_DIRS, ICLR_LORA, d), 0.5 * ICLR_LORA ** -0.5),
        'rwkv_g1': nrm((na, d, GATE_LORA), d ** -0.5),
        'rwkv_g2': nrm((na, GATE_LORA, d), GATE_LORA ** -0.5),
        'rwkv_k_k': 0.85 + nrm((na, d), 0.02),
        'rwkv_k_a': 1.0 + nrm((na, d), 0.02),
        'rwkv_r_k': nrm((na, RWKV_HEADS, RWKV_HEAD), 0.1),
        'rwkv_ln_w': 1.0 + nrm((na, d), 0.02),
        'rwkv_ln_b': nrm((na, d), 0.02),
        'rwkv_v0': nrm((na - 1, d), 0.3),
        'rwkv_v1': nrm((na - 1, d, VRES_LORA), d ** -0.5),
        'rwkv_v2': nrm((na - 1, VRES_LORA, d), 0.5 * VRES_LORA ** -0.5),
        'mlstm_w_in': nrm((nbl, d, MLSTM_PROJ), d ** -0.5),
        'mlstm_b_gate': b_gate,
        'mlstm_conv_w': nrm((nbl, 3, 3, 2 * MLSTM_QK), 1.0 / 3.0),
        'mlstm_conv_b': nrm((nbl, 2 * MLSTM_QK), 0.02),
        'mlstm_norm_w': 1.0 + nrm((nbl, MLSTM_V), 0.02),
        'mlstm_w_out': nrm((nbl, MLSTM_V, d), MLSTM_V ** -0.5),
        'ffn_w_in': nrm((DEPTH, d, 2 * D_FF), d ** -0.5),
        'ffn_w_out': nrm((DEPTH, D_FF, d), D_FF ** -0.5),
    }


def reference(x, c, ctx, c_ctx, mod_w, mod_b, norm_g, final_g,
              rwkv_mu, rwkv_w_r, rwkv_w_k, rwkv_w_v, rwkv_w_o, rwkv_w0, rwkv_w1, rwkv_w2,
              rwkv_a0, rwkv_a1, rwkv_a2, rwkv_g1, rwkv_g2, rwkv_k_k, rwkv_k_a, rwkv_r_k,
              rwkv_ln_w, rwkv_ln_b, rwkv_v0, rwkv_v1, rwkv_v2,
              mlstm_w_in, mlstm_b_gate, mlstm_conv_w, mlstm_conv_b, mlstm_norm_w, mlstm_w_out,
              ffn_w_in, ffn_w_out):
    v_first = None
    for i in range(DEPTH):
        last = i == DEPTH - 1
        j = i // N_MIXERS
        mod_x = jax.nn.silu(c) @ mod_w[i] + mod_b[i]
        mod_c = jax.nn.silu(c_ctx) @ mod_w[i] + mod_b[i]
        sh1, sc1, gt1, sh2, sc2, gt2 = jnp.split(mod_x[:, None, :], 6, axis=-1)
        csh1, csc1, cgt1, csh2, csc2, cgt2 = jnp.split(mod_c, 6, axis=-1)
        hx = _rmsnorm(x, norm_g[i, 0]) * (1 + sc1) + sh1
        hc = _rmsnorm(ctx, norm_g[i, 0]) * (1 + csc1) + csh1
        if i % N_MIXERS == 0:
            p = {'mu': rwkv_mu[j], 'w_r': rwkv_w_r[j], 'w_k': rwkv_w_k[j], 'w_v': rwkv_w_v[j],
                 'w_o': rwkv_w_o[j], 'w0': rwkv_w0[j], 'w1': rwkv_w1[j], 'w2': rwkv_w2[j],
                 'a0': rwkv_a0[j], 'a1': rwkv_a1[j], 'a2': rwkv_a2[j], 'g1': rwkv_g1[j], 'g2': rwkv_g2[j],
                 'k_k': rwkv_k_k[j], 'k_a': rwkv_k_a[j], 'r_k': rwkv_r_k[j],
                 'ln_w': rwkv_ln_w[j], 'ln_b': rwkv_ln_b[j]}
            if j > 0:
                p['v0'] = rwkv_v0[j - 1]
                p['v1'] = rwkv_v1[j - 1]
                p['v2'] = rwkv_v2[j - 1]
            out_x, out_c, v_pair = _rwkv7_mixer(hx, hc, p, v_first if j > 0 else None, not last)
            if j == 0:
                v_first = v_pair
        else:
            p = {'w_in': mlstm_w_in[j], 'b_gate': mlstm_b_gate[j], 'conv_w': mlstm_conv_w[j],
                 'conv_b': mlstm_conv_b[j], 'norm_w': mlstm_norm_w[j], 'w_out': mlstm_w_out[j]}
            out_x, out_c = _mlstm_mixer(hx, hc, p, not last)
        x = x + gt1 * out_x
        hx = _rmsnorm(x, norm_g[i, 1]) * (1 + sc2) + sh2
        x = x + gt2 * _swiglu(hx, ffn_w_in[i], ffn_w_out[i])
        if not last:
            ctx = ctx + cgt1 * out_c
            hc = _rmsnorm(ctx, norm_g[i, 1]) * (1 + csc2) + csh2
            ctx = ctx + cgt2 * _swiglu(hc, ffn_w_in[i], ffn_w_out[i])
    return _rmsnorm(x, final_g)
```

```python
import functools

import jax
import jax.numpy as jnp
from jax import lax
from jax.experimental import pallas as pl
from jax.experimental.pallas import tpu as pltpu

F32 = jnp.float32
BF16 = jnp.bfloat16

GRID_W = 64
NORM_EPS = 1e-6
RWKV_HEAD = 64
MLSTM_HEADS = 8
CHUNK = 64
GATE_CAP = 15.0
LANES = 128
SUBLANES = 8
VMEM_LIMIT = 56 * 1024 * 1024


def _cparams(sem):
    return pltpu.CompilerParams(dimension_semantics=sem, vmem_limit_bytes=VMEM_LIMIT)


def _mm_body(*refs, nk, prologue, epilogue):
    it = iter(refs)
    x_ref = next(it)
    if prologue == "normmod":
        g_ref, sc_ref, sh_ref = next(it), next(it), next(it)
    w_ref = next(it)
    w2_ref = next(it) if epilogue == "swiglu" else None
    if epilogue == "gate_res":
        res_ref, gate_ref = next(it), next(it)
    o_ref = next(it)
    acc_ref = next(it)
    acc2_ref = next(it) if epilogue == "swiglu" else None
    h_ref = next(it) if prologue == "normmod" else None

    j = pl.program_id(1)
    k = pl.program_id(2)

    if prologue == "normmod":
        @pl.when(j == 0)
        def _():
            x = x_ref[...].astype(F32)
            ms = jnp.mean(x * x, axis=-1, keepdims=True)
            y = x * lax.rsqrt(ms + NORM_EPS) * g_ref[...]
            h_ref[...] = (y * (1.0 + sc_ref[0]) + sh_ref[0]).astype(BF16)
        xb = h_ref[...]
    else:
        xb = x_ref[...].astype(BF16)

    @pl.when(k == 0)
    def _():
        acc_ref[...] = jnp.zeros_like(acc_ref)
        if acc2_ref is not None:
            acc2_ref[...] = jnp.zeros_like(acc2_ref)

    acc_ref[...] += jnp.dot(xb, w_ref[...].astype(BF16), preferred_element_type=F32)
    if acc2_ref is not None:
        acc2_ref[...] += jnp.dot(xb, w2_ref[...].astype(BF16), preferred_element_type=F32)

    @pl.when(k == nk - 1)
    def _():
        acc = acc_ref[...]
        if epilogue == "swiglu":
            acc = acc * jax.nn.sigmoid(acc) * acc2_ref[...]
        elif epilogue == "gate_res":
            acc = res_ref[...] + gate_ref[0] * acc
        o_ref[...] = acc.astype(o_ref.dtype)


def _matmul(x, w, *, tm, tn, tk=None, out_dtype=F32, prologue=None, pro_args=None,
            epilogue=None, epi_args=None, mod_index=None, n_out=None, w_col0=0, w2_col0=0,
            w_row0=0):
    m, kdim = x.shape
    n_out = w.shape[1] if n_out is None else n_out
    tk = kdim if tk is None else tk
    assert m % tm == 0 and n_out % tn == 0 and kdim % tk == 0
    assert w_col0 % tn == 0 and w2_col0 % tn == 0 and w_row0 % tk == 0
    nk = kdim // tk
    if prologue == "normmod":
        assert nk == 1
    c0, c2, r0 = w_col0 // tn, w2_col0 // tn, w_row0 // tk

    in_specs = [pl.BlockSpec((tm, tk), lambda i, j, k: (i, k))]
    args = [x]
    if prologue == "normmod":
        g, sc, sh = pro_args
        in_specs += [pl.BlockSpec((1, tk), lambda i, j, k: (0, 0)),
                     pl.BlockSpec((1, 1, tk), lambda i, j, k: (mod_index(i), 0, 0)),
                     pl.BlockSpec((1, 1, tk), lambda i, j, k: (mod_index(i), 0, 0))]
        args += [g, sc, sh]
    in_specs.append(pl.BlockSpec((tk, tn), lambda i, j, k: (k + r0, j + c0)))
    args.append(w)
    if epilogue == "swiglu":
        in_specs.append(pl.BlockSpec((tk, tn), lambda i, j, k: (k + r0, j + c2)))
        args.append(w)
    if epilogue == "gate_res":
        res, gate = epi_args
        in_specs += [pl.BlockSpec((tm, tn), lambda i, j, k: (i, j)),
                     pl.BlockSpec((1, 1, tn), lambda i, j, k: (mod_index(i), 0, j))]
        args += [res, gate]
    scratch = [pltpu.VMEM((tm, tn), F32)]
    if epilogue == "swiglu":
        scratch.append(pltpu.VMEM((tm, tn), F32))
    if prologue == "normmod":
        scratch.append(pltpu.VMEM((tm, tk), BF16))
    return pl.pallas_call(
        functools.partial(_mm_body, nk=nk, prologue=prologue, epilogue=epilogue),
        grid=(m // tm, n_out // tn, nk),
        in_specs=in_specs,
        out_specs=pl.BlockSpec((tm, tn), lambda i, j, k: (i, j)),
        out_shape=jax.ShapeDtypeStruct((m, n_out), out_dtype),
        scratch_shapes=scratch,
        compiler_params=_cparams(("parallel", "arbitrary", "arbitrary")),
        name="mm_" + (prologue or "x") + "_" + (epilogue or "plain"),
    )(*args)


def _normmod_body(x_ref, g_ref, sc_ref, sh_ref, o_ref):
    x = x_ref[...].astype(F32)
    ms = jnp.mean(x * x, axis=-1, keepdims=True)
    y = x * lax.rsqrt(ms + NORM_EPS) * g_ref[...]
    o_ref[...] = (y * (1.0 + sc_ref[0]) + sh_ref[0]).astype(o_ref.dtype)


def _normmod(x, g, sc, sh, *, tm, mod_index):
    m, d = x.shape
    mod_spec = pl.BlockSpec((1, 1, d), lambda i: (mod_index(i), 0, 0))
    return pl.pallas_call(
        _normmod_body,
        grid=(m // tm,),
        in_specs=[pl.BlockSpec((tm, d), lambda i: (i, 0)), pl.BlockSpec((1, d), lambda i: (0, 0)),
                  mod_spec, mod_spec],
        out_specs=pl.BlockSpec((tm, d), lambda i: (i, 0)),
        out_shape=jax.ShapeDtypeStruct((m, d), F32),
        compiler_params=_cparams(("parallel",)),
        name="normmod",
    )(x, g, sc, sh)


def _rwkv_scan_body(r_ref, w_ref, k_ref, v_ref, z_ref, b_ref, y_ref, s_ref, sz_ref, *, tblk, n):
    zdir = pl.program_id(0)
    tb = pl.program_id(2)

    @pl.when(tb == 0)
    def _():
        s_ref[...] = jnp.zeros_like(s_ref)

    bwd = zdir == 1
    t_first = jnp.where(bwd, tblk - 1, 0)
    t_step = jnp.where(bwd, -1, 1)

    def row(ref, t, kk):
        return jnp.broadcast_to(ref[t, pl.ds(kk, 1), :], (n, LANES))

    acc = jnp.zeros((n, LANES), F32)
    for kk in range(n):
        acc = acc + s_ref[kk] * row(z_ref, t_first, kk)
    sz_ref[...] = acc

    def step(i, carry):
        t = t_first + i * t_step
        t_next = jnp.clip(t + t_step, 0, tblk - 1)
        sz = sz_ref[...]
        v_t = v_ref[t]
        y = jnp.zeros((n, LANES), F32)
        sz_next = jnp.zeros((n, LANES), F32)
        for kk in range(n):
            s_new = (s_ref[kk] * row(w_ref, t, kk) + sz * row(b_ref, t, kk)
                     + v_t * row(k_ref, t, kk))
            s_ref[kk] = s_new
            y = y + s_new * row(r_ref, t, kk)
            sz_next = sz_next + s_new * row(z_ref, t_next, kk)
        y_ref[t] = y
        sz_ref[...] = sz_next
        return carry

    lax.fori_loop(0, tblk, step, 0)


def _rwkv_scan(r, w, k, v, z, b, *, n_ctx, tblk):
    s_len, n, c = r.shape
    assert c % LANES == 0 and n_ctx % tblk == 0 and s_len % tblk == 0
    nbc, nb = n_ctx // tblk, s_len // tblk

    def tmap(zd, p):
        rev = jnp.where(p < nbc, nbc - 1 - p, nb - 1 - (p - nbc))
        return jnp.where(zd == 0, p, rev)

    shared = pl.BlockSpec((tblk, n, LANES), lambda zd, g, p: (tmap(zd, p), 0, g))
    perdir = pl.BlockSpec((None, tblk, n, LANES), lambda zd, g, p: (zd, tmap(zd, p), 0, g))
    return pl.pallas_call(
        functools.partial(_rwkv_scan_body, tblk=tblk, n=n),
        grid=(2, c // LANES, nb),
        in_specs=[shared, perdir, perdir, shared, shared, perdir],
        out_specs=perdir,
        out_shape=jax.ShapeDtypeStruct((2, s_len, n, c), F32),
        scratch_shapes=[pltpu.VMEM((n, n, LANES), F32), pltpu.VMEM((n, LANES), F32)],
        compiler_params=_cparams(("parallel", "parallel", "arbitrary")),
        name="rwkv_scan",
    )(r, w, k, v, z, b)


def _mlstm_body(q_ref, k_ref, v_ref, ic_ref, fc_ref, ir_ref, fr_ref, o_ref,
                c_ref, n_ref, m_ref, *, nh, dk, dv):
    zdir = pl.program_id(0)
    p = pl.program_id(2)

    @pl.when(p == 0)
    def _():
        c_ref[...] = jnp.zeros_like(c_ref)
        n_ref[...] = jnp.zeros_like(n_ref)
        m_ref[...] = jnp.zeros_like(m_ref)

    ti = lax.broadcasted_iota(jnp.int32, (CHUNK, CHUNK), 0)
    tj = lax.broadcasted_iota(jnp.int32, (CHUNK, CHUNK), 1)
    mask = (ti - tj) * (1 - 2 * zdir) >= 0
    tri = mask.astype(F32)
    fcol = fc_ref[0]
    icol = ic_ref[0]
    frow = fr_ref[0, 0]
    irow = ir_ref[0, 0]
    hi = lax.Precision.HIGHEST
    bcum_col = jnp.dot(tri, fcol, precision=hi, preferred_element_type=F32)
    bcum_row = lax.dot_general(frow, tri, (((1,), (1,)), ((), ())), precision=hi,
                               preferred_element_type=F32)

    for h in range(nh):
        q32 = q_ref[:, h * dk:(h + 1) * dk].astype(F32)
        q = q32.astype(BF16)
        k32 = k_ref[:, h * dk:(h + 1) * dk].astype(F32)
        v = v_ref[:, h * dv:(h + 1) * dv].astype(BF16)
        c_st = c_ref[h]
        n_st = n_ref[h]
        m_st = m_ref[h][:, :1]
        bc = bcum_col[:, h:h + 1]
        br = bcum_row[h:h + 1, :]
        log_d = jnp.where(mask, bc - br + irow[h:h + 1, :], -jnp.inf)
        log_inter = bc + m_st
        m_t = jnp.maximum(log_inter, jnp.max(log_d, axis=-1, keepdims=True))
        qk = lax.dot_general(q, k32.astype(BF16), (((1,), (1,)), ((), ())),
                             preferred_element_type=F32)
        s = qk * jnp.exp(log_d - m_t)
        w_inter = jnp.exp(log_inter - m_t)
        num = (w_inter * jnp.dot(q, c_st.astype(BF16), preferred_element_type=F32)
               + jnp.dot(s.astype(BF16), v, preferred_element_type=F32))
        qn = jnp.sum(q32 * n_st, axis=-1, keepdims=True)
        den = w_inter * qn + jnp.sum(s, axis=-1, keepdims=True)
        o_ref[0, :, h * dv:(h + 1) * dv] = num / jnp.maximum(jnp.abs(den), jnp.exp(-m_t))

        b_end = jnp.sum(frow[h:h + 1, :], axis=-1, keepdims=True)
        a_col = b_end - bc + icol[:, h:h + 1]
        m_new = jnp.maximum(b_end + m_st, jnp.max(a_col, axis=0, keepdims=True))
        wk = jnp.exp(a_col - m_new) * k32
        dec = jnp.exp(b_end + m_st - m_new)
        c_ref[h] = dec * c_st + lax.dot_general(wk.astype(BF16), v, (((0,), (0,)), ((), ())),
                                                preferred_element_type=F32)
        n_ref[h] = dec * n_st + jnp.sum(wk, axis=0, keepdims=True)
        m_ref[h] = jnp.broadcast_to(m_new, (1, LANES))


def _mlstm_scan(q, k, u, v_col0, ig, lf, *, nb, n_ctx, n_lat):
    m = q.shape[0]
    nh = MLSTM_HEADS
    dk = q.shape[1] // nh
    dvt = 2 * q.shape[1]
    dv = dvt // nh
    ncc, ncl = n_ctx // CHUNK, n_lat // CHUNK
    nchunks = ncc + ncl
    ctx_blocks = nb * ncc
    assert v_col0 % dvt == 0

    def rmap(zd, bb, p):
        rev = jnp.where(p < ncc, ncc - 1 - p, nchunks - 1 - (p - ncc))
        ch = jnp.where(zd == 0, p, rev)
        return jnp.where(ch < ncc, bb * ncc + ch, ctx_blocks + bb * ncl + (ch - ncc))

    ig_row = jnp.swapaxes(ig.reshape(2, m // CHUNK, CHUNK, nh), 2, 3)
    lf_row = jnp.swapaxes(lf.reshape(2, m // CHUNK, CHUNK, nh), 2, 3)
    col_spec = pl.BlockSpec((1, CHUNK, nh), lambda zd, bb, p: (zd, rmap(zd, bb, p), 0))
    row_spec = pl.BlockSpec((1, 1, nh, CHUNK), lambda zd, bb, p: (zd, rmap(zd, bb, p), 0, 0))
    return pl.pallas_call(
        functools.partial(_mlstm_body, nh=nh, dk=dk, dv=dv),
        grid=(2, nb, nchunks),
        in_specs=[pl.BlockSpec((CHUNK, nh * dk), lambda zd, bb, p: (rmap(zd, bb, p), 0)),
                  pl.BlockSpec((CHUNK, nh * dk), lambda zd, bb, p: (rmap(zd, bb, p), 0)),
                  pl.BlockSpec((CHUNK, dvt), lambda zd, bb, p: (rmap(zd, bb, p), v_col0 // dvt)),
                  col_spec, col_spec, row_spec, row_spec],
        out_specs=pl.BlockSpec((1, CHUNK, dvt), lambda zd, bb, p: (zd, rmap(zd, bb, p), 0)),
        out_shape=jax.ShapeDtypeStruct((2, m, dvt), F32),
        scratch_shapes=[pltpu.VMEM((nh, dk, dv), F32), pltpu.VMEM((nh, 1, dk), F32),
                        pltpu.VMEM((nh, 1, LANES), F32)],
        compiler_params=_cparams(("parallel", "parallel", "arbitrary")),
        name="mlstm_scan",
    )(q, k, u, ig, lf, ig_row, lf_row)


def _head_ln(y, nheads, eps):
    shp = y.shape
    y = y.reshape(shp[:-1] + (nheads, shp[-1] // nheads))
    mu = jnp.mean(y, axis=-1, keepdims=True)
    var = jnp.mean(jnp.square(y - mu), axis=-1, keepdims=True)
    return ((y - mu) * lax.rsqrt(var + eps)).reshape(shp)


def _grid_shift(h):
    b, t, d = h.shape
    rows = t // GRID_W
    q = d // 4
    g = h.reshape(b, rows, GRID_W, d)
    left = jnp.pad(g[:, :, :-1, :q], ((0, 0), (0, 0), (1, 0), (0, 0)))
    right = jnp.pad(g[:, :, 1:, q:2 * q], ((0, 0), (0, 0), (0, 1), (0, 0)))
    up = jnp.pad(g[:, :-1, :, 2 * q:3 * q], ((0, 0), (1, 0), (0, 0), (0, 0)))
    down = jnp.pad(g[:, 1:, :, 3 * q:], ((0, 0), (0, 1), (0, 0), (0, 0)))
    return jnp.concatenate([left, right, up, down], axis=-1).reshape(b, t, d)


def _seq_shift(h):
    half = h.shape[-1] // 2
    prev = jnp.pad(h[:, :-1, :half], ((0, 0), (1, 0), (0, 0)))
    nxt = jnp.pad(h[:, 1:, half:], ((0, 0), (0, 1), (0, 0)))
    return jnp.concatenate([prev, nxt], axis=-1)


def _dwconv_grid(u, w, bias):
    b, t, ch = u.shape
    rows = t // GRID_W
    g = u.reshape(b, rows, GRID_W, ch)
    gp = jnp.pad(g, ((0, 0), (1, 1), (1, 1), (0, 0)))
    y = jnp.zeros_like(g)
    for di in range(3):
        for dj in range(3):
            y = y + gp[:, di:di + rows, dj:dj + GRID_W, :] * w[di, dj]
    return y.reshape(b, t, ch) + bias


def _dwconv_seq(u, w, bias):
    l = u.shape[1]
    up = jnp.pad(u, ((0, 0), (1, 1), (0, 0)))
    y = up[:, 0:l] * w[0] + up[:, 1:l + 1] * w[1] + up[:, 2:l + 2] * w[2]
    return y + bias


class _Dims:
    def __init__(self, nb, n_ctx, n_lat, d):
        self.nb, self.n_ctx, self.n_lat, self.d = nb, n_ctx, n_lat, d
        self.mc, self.mx = nb * n_ctx, nb * n_lat
        self.m = self.mc + self.mx
        self.tm = 1024 if (self.mc % 1024 == 0 and n_lat % 1024 == 0) else n_ctx
        assert self.mc % self.tm == 0 and n_lat % self.tm == 0

    def mod_index(self, i):
        nbc = self.mc // self.tm
        bpb = self.n_lat // self.tm
        return jnp.where(i < nbc, self.nb, (i - nbc) // bpb)

    def split(self, a):
        return (a[:self.mc].reshape(self.nb, self.n_ctx, -1),
                a[self.mc:].reshape(self.nb, self.n_lat, -1))

    def join(self, a_c, a_x):
        return jnp.concatenate([a_c.reshape(self.mc, -1), a_x.reshape(self.mx, -1)], axis=0)

    def to_scan(self, a):
        a_c, a_x = self.split(a)
        s = jnp.concatenate([a_c, a_x], axis=1)
        s = s.reshape(self.nb, self.n_ctx + self.n_lat, -1, RWKV_HEAD)
        return jnp.transpose(s, (1, 3, 0, 2)).reshape(self.n_ctx + self.n_lat, RWKV_HEAD, -1)

    def from_scan(self, y):
        s_len = y.shape[0]
        y = y.reshape(s_len, RWKV_HEAD, self.nb, -1)
        y = jnp.transpose(y, (2, 0, 3, 1)).reshape(self.nb, s_len, -1)
        return self.join(y[:, :self.n_ctx], y[:, self.n_ctx:])


def _tn(n, cap=512):
    for t in (cap, 512, 384, 256, 128):
        if t <= cap and n % t == 0:
            return t
    return n


def _rwkv_layer(dm, h, p, v_first):
    d = dm.d
    nheads = d // RWKV_HEAD
    h_c, h_x = dm.split(h)
    xx = dm.join(_seq_shift(h_c) - h_c, _grid_shift(h_x) - h_x)
    xr, xw, xk, xv, xa, xg = [(h + xx * p['mu'][n]).astype(BF16) for n in range(6)]
    mm = functools.partial(_matmul, tm=dm.tm)
    r = mm(xr, p['w_r'], tn=512)
    k = mm(xk, p['w_k'], tn=512)
    v = mm(xv, p['w_v'], tn=512)
    if v_first is not None:
        lv = mm(xv, p['v1'], tn=_tn(p['v1'].shape[1]))
        v = v + (v_first - v) * jax.nn.sigmoid(p['v0'] + mm(lv, p['v2'], tn=512))
    nw = p['w1'].shape[1] // 2
    lw = jnp.tanh(mm(xw, p['w1'], tn=_tn(p['w1'].shape[1])))
    la = mm(xa, p['a1'], tn=_tn(p['a1'].shape[1]))
    lg = jax.nn.sigmoid(mm(xg, p['g1'], tn=_tn(p['g1'].shape[1])))
    g = mm(lg, p['g2'], tn=512)
    kk = (k * p['k_k']).reshape(dm.m, nheads, RWKV_HEAD)
    kk = (kk / jnp.maximum(jnp.sqrt(jnp.sum(kk * kk, axis=-1, keepdims=True)), 1e-12)).reshape(dm.m, d)
    decay, kmod, bb = [], [], []
    rk = 0.0
    for zd in range(2):
        w_pre = p['w0'][zd] + mm(lw[:, zd * nw:(zd + 1) * nw], p['w2'][zd], tn=512)
        decay.append(jnp.exp(-jnp.exp(-0.5) * jax.nn.sigmoid(w_pre)))
        a = jax.nn.sigmoid(p['a0'][zd] + mm(la[:, zd * nw:(zd + 1) * nw], p['a2'][zd], tn=512))
        km = k * (1 + (a - 1) * p['k_a'])
        kmod.append(km)
        bb.append(kk * a)
        rk = rk + (r * km * p['r_k']).reshape(dm.m, nheads, RWKV_HEAD).sum(-1)
    ts = dm.to_scan
    y = _rwkv_scan(ts(r), jnp.stack([ts(decay[0]), ts(decay[1])]),
                   jnp.stack([ts(kmod[0]), ts(kmod[1])]), ts(v), ts(-kk),
                   jnp.stack([ts(bb[0]), ts(bb[1])]), n_ctx=dm.n_ctx, tblk=32)
    y = dm.from_scan(y[0] + y[1])
    yn = _head_ln(y, nheads, RWKV_HEAD * 1e-5) * p['ln_w'] + p['ln_b']
    bonus = (rk[..., None] * v.reshape(dm.m, nheads, RWKV_HEAD)).reshape(dm.m, d)
    return ((yn + bonus) * g).astype(BF16), v


def _mlstm_layer(dm, xs, norm_g, sc1p, sh, p):
    d = dm.d
    nh = MLSTM_HEADS
    qk_w = d
    u = _matmul(xs, p['w_in'], tm=dm.tm, tn=896, prologue="normmod",
                pro_args=(norm_g, sc1p, sh), mod_index=dm.mod_index)
    qk_c, qk_x = dm.split(u[:, :qk_w])
    qk = dm.join(jax.nn.silu(_dwconv_seq(qk_c, p['conv_w'][1], p['conv_b'])),
                 jax.nn.silu(_dwconv_grid(qk_x, p['conv_w'], p['conv_b'])))
    q = qk[:, :qk_w // 2] * ((qk_w // 2 // nh) ** -0.5)
    k = qk[:, qk_w // 2:]
    gates = u[:, qk_w + 2 * d:qk_w + 2 * d + 4 * nh].reshape(dm.m, 2, 2, nh) + p['b_gate']
    gates = GATE_CAP * jnp.tanh(gates / GATE_CAP)
    ig = jnp.moveaxis(gates[:, :, 0, :], 1, 0)
    lf = jnp.moveaxis(jax.nn.log_sigmoid(gates[:, :, 1, :]), 1, 0)
    hs = _mlstm_scan(q, k, u, qk_w, ig, lf, nb=dm.nb, n_ctx=dm.n_ctx, n_lat=dm.n_lat)
    hsum = hs[0] + hs[1]
    hn = _head_ln(hsum, nh, NORM_EPS) * p['norm_w']
    o = u[:, qk_w + d:qk_w + 2 * d]
    return (hn * jax.nn.sigmoid(o)).astype(BF16)


def kernel(x, c, ctx, c_ctx, mod_w, mod_b, norm_g, final_g, rwkv_mu, rwkv_w_r, rwkv_w_k, rwkv_w_v, rwkv_w_o, rwkv_w0, rwkv_w1, rwkv_w2, rwkv_a0, rwkv_a1, rwkv_a2, rwkv_g1, rwkv_g2, rwkv_k_k, rwkv_k_a, rwkv_r_k, rwkv_ln_w, rwkv_ln_b, rwkv_v0, rwkv_v1, rwkv_v2, mlstm_w_in, mlstm_b_gate, mlstm_conv_w, mlstm_conv_b, mlstm_norm_w, mlstm_w_out, ffn_w_in, ffn_w_out):
    nb, n_lat, d = x.shape
    n_ctx = ctx.shape[1]
    depth = mod_w.shape[0]
    d_ff = ffn_w_out.shape[1]
    dm = _Dims(nb, n_ctx, n_lat, d)
    bf = lambda a: a.astype(BF16)

    cond = jax.nn.silu(jnp.concatenate([c, c_ctx[None, :]], axis=0))
    rows = cond.shape[0]
    rpad = -rows % (2 * SUBLANES)
    cond = jnp.pad(cond, ((0, rpad), (0, 0)))
    mods = []
    mod_w2 = mod_w.reshape(depth * d, 6 * d)
    for i in range(depth):
        mo = _matmul(cond, mod_w2, tm=rows + rpad, tn=512, w_row0=i * d) + mod_b[i]
        mods.append(mo[:rows].reshape(rows, 6, 1, d))

    xs = dm.join(ctx, x)
    v_first = None
    for i in range(depth):
        last = i == depth - 1
        j = i // 2
        sh1, sc1, gt1, sh2, sc2, gt2 = [mods[i][:, n] for n in range(6)]
        g1 = norm_g[i, 0][None, :]
        g2 = norm_g[i, 1][None, :]
        if i % 2 == 0:
            hmod = _normmod(xs, g1, sc1, sh1, tm=dm.tm, mod_index=dm.mod_index)
            p = {'mu': rwkv_mu[j], 'w_r': bf(rwkv_w_r[j]), 'w_k': bf(rwkv_w_k[j]), 'w_v': bf(rwkv_w_v[j]),
                 'w0': rwkv_w0[j],
                 'w1': bf(jnp.concatenate([rwkv_w1[j, 0], rwkv_w1[j, 1]], axis=1)),
                 'w2': bf(rwkv_w2[j]), 'a0': rwkv_a0[j],
                 'a1': bf(jnp.concatenate([rwkv_a1[j, 0], rwkv_a1[j, 1]], axis=1)),
                 'a2': bf(rwkv_a2[j]), 'g1': bf(rwkv_g1[j]), 'g2': bf(rwkv_g2[j]),
                 'k_k': rwkv_k_k[j], 'k_a': rwkv_k_a[j], 'r_k': rwkv_r_k[j].reshape(-1),
                 'ln_w': rwkv_ln_w[j], 'ln_b': rwkv_ln_b[j]}
            if j > 0:
                p['v0'] = rwkv_v0[j - 1]
                p['v1'] = bf(rwkv_v1[j - 1])
                p['v2'] = bf(rwkv_v2[j - 1])
            ro, v_cur = _rwkv_layer(dm, hmod, p, v_first if j > 0 else None)
            if j == 0:
                v_first = v_cur
            w_o = bf(rwkv_w_o[j])
        else:
            p = {'w_in': bf(jnp.pad(mlstm_w_in[j], ((0, 0), (0, -mlstm_w_in.shape[2] % 896)))),
                 'b_gate': mlstm_b_gate[j], 'conv_w': mlstm_conv_w[j], 'conv_b': mlstm_conv_b[j],
                 'norm_w': mlstm_norm_w[j]}
            ro = _mlstm_layer(dm, xs, g1, sc1, sh1, p)
            w_o = bf(mlstm_w_out[j])
        xs = _matmul(ro, w_o, tm=dm.tm, tn=512, epilogue="gate_res", epi_args=(xs, gt1),
                     mod_index=dm.mod_index)
        w_in = bf(ffn_w_in[i])
        hid = _matmul(xs, w_in, tm=dm.tm, tn=512, out_dtype=BF16, prologue="normmod",
                      pro_args=(g2, sc2, sh2), mod_index=dm.mod_index,
                      epilogue="swiglu", n_out=d_ff, w2_col0=d_ff)
        xs = _matmul(hid, bf(ffn_w_out[i]), tm=dm.tm, tn=512, tk=d_ff // 2,
                     epilogue="gate_res", epi_args=(xs, gt2), mod_index=dm.mod_index)
    xl = xs[dm.mc:].reshape(nb, n_lat, d)
    return xl * lax.rsqrt(jnp.mean(xl * xl, axis=-1, keepdims=True) + NORM_EPS) * final_g
```

```python
import functools

import jax
import jax.numpy as jnp
from jax import lax
from jax.experimental import pallas as pl
from jax.experimental.pallas import tpu as pltpu

F32 = jnp.float32
BF16 = jnp.bfloat16

GRID_W = 64
NORM_EPS = 1e-6
RWKV_HEAD = 64
MLSTM_HEADS = 8
CHUNK = 64
GATE_CAP = 15.0
LANES = 128
SUBLANES = 8
VMEM_LIMIT = 56 * 1024 * 1024


def _cparams(sem):
    return pltpu.CompilerParams(dimension_semantics=sem, vmem_limit_bytes=VMEM_LIMIT)


def _rms_mod(x, g, sc, sh):
    ms = jnp.mean(x * x, axis=-1, keepdims=True)
    return x * lax.rsqrt(ms + NORM_EPS) * g * (1.0 + sc) + sh


def _head_ln_lanes(y, nheads, eps):
    hd = y.shape[-1] // nheads
    out = []
    for h in range(nheads):
        seg = y[:, h * hd:(h + 1) * hd]
        mu = jnp.mean(seg, axis=-1, keepdims=True)
        var = jnp.mean(jnp.square(seg - mu), axis=-1, keepdims=True)
        out.append((seg - mu) * lax.rsqrt(var + eps))
    return jnp.concatenate(out, axis=-1)


def _pro_normmod(x_ref, g_ref, sc_ref, sh_ref):
    return _rms_mod(x_ref[...].astype(F32), g_ref[...], sc_ref[0], sh_ref[0]).astype(BF16)


def _pro_mulg(x_ref, g_ref):
    return (x_ref[...] * g_ref[...]).astype(BF16)


def _pro_mlstm_read(h_ref, o_ref, nw_ref):
    hn = _head_ln_lanes(h_ref[0] + h_ref[1], MLSTM_HEADS, NORM_EPS) * nw_ref[...]
    return (hn * jax.nn.sigmoid(o_ref[...])).astype(BF16)


def _epi_swiglu(acc, acc2):
    return acc * jax.nn.sigmoid(acc) * acc2


def _epi_gate_res(acc, res_ref, gate_ref):
    return res_ref[...] + gate_ref[0] * acc


def _epi_vres(acc, v_ref, vf_ref, v0_ref):
    v = v_ref[...]
    return v + (vf_ref[...] - v) * jax.nn.sigmoid(v0_ref[...] + acc)


def _mm_body(*refs, nk, n_pro, pro_fn, n_epi, epi_fn, dual):
    it = iter(refs)
    pro_refs = [next(it) for _ in range(n_pro)]
    w_ref = next(it)
    w2_ref = next(it) if dual else None
    epi_refs = [next(it) for _ in range(n_epi)]
    o_ref = next(it)
    acc_ref = next(it)
    acc2_ref = next(it) if dual else None
    xb_ref = next(it) if pro_fn is not None else None

    j = pl.program_id(1)
    k = pl.program_id(2)

    if pro_fn is not None:
        @pl.when(j == 0)
        def _():
            xb_ref[...] = pro_fn(*pro_refs)
        xb = xb_ref[...]
    else:
        xb = pro_refs[0][...].astype(BF16)

    @pl.when(k == 0)
    def _():
        acc_ref[...] = jnp.zeros_like(acc_ref)
        if dual:
            acc2_ref[...] = jnp.zeros_like(acc2_ref)

    acc_ref[...] += jnp.dot(xb, w_ref[...].astype(BF16), preferred_element_type=F32)
    if dual:
        acc2_ref[...] += jnp.dot(xb, w2_ref[...].astype(BF16), preferred_element_type=F32)

    @pl.when(k == nk - 1)
    def _():
        acc = acc_ref[...]
        if dual:
            acc = epi_fn(acc, acc2_ref[...], *epi_refs)
        elif epi_fn is not None:
            acc = epi_fn(acc, *epi_refs)
        o_ref[...] = acc.astype(o_ref.dtype)


def _matmul(x, w, *, m, kdim, tm, tn, tk=None, out_dtype=F32, n_out=None, w_col0=0, w2_col0=None,
            w_row0=0, x_col0=0, pro=None, epi=None, name="mm"):
    n_out = w.shape[1] if n_out is None else n_out
    tk = kdim if tk is None else tk
    dual = w2_col0 is not None
    assert m % tm == 0 and n_out % tn == 0 and kdim % tk == 0
    assert w_col0 % tn == 0 and w_row0 % tk == 0 and x_col0 % tk == 0
    nk = kdim // tk
    c0, r0, xc0 = w_col0 // tn, w_row0 // tk, x_col0 // tk
    pro_fn, pro_args, pro_specs = pro if pro is not None else (None, [x], [
        pl.BlockSpec((tm, tk), lambda i, j, k: (i, k + xc0))])
    if pro_fn is not None:
        assert nk == 1
    epi_fn, epi_args, epi_specs = epi if epi is not None else (None, [], [])
    in_specs = list(pro_specs) + [pl.BlockSpec((tk, tn), lambda i, j, k: (k + r0, j + c0))]
    args = list(pro_args) + [w]
    if dual:
        assert w2_col0 % tn == 0
        c2 = w2_col0 // tn
        in_specs.append(pl.BlockSpec((tk, tn), lambda i, j, k: (k + r0, j + c2)))
        args.append(w)
    in_specs += list(epi_specs)
    args += list(epi_args)
    scratch = [pltpu.VMEM((tm, tn), F32)]
    if dual:
        scratch.append(pltpu.VMEM((tm, tn), F32))
    if pro_fn is not None:
        scratch.append(pltpu.VMEM((tm, tk), BF16))
    return pl.pallas_call(
        functools.partial(_mm_body, nk=nk, n_pro=len(pro_args), pro_fn=pro_fn,
                          n_epi=len(epi_args), epi_fn=epi_fn, dual=dual),
        grid=(m // tm, n_out // tn, nk),
        in_specs=in_specs,
        out_specs=pl.BlockSpec((tm, tn), lambda i, j, k: (i, j)),
        out_shape=jax.ShapeDtypeStruct((m, n_out), out_dtype),
        scratch_shapes=scratch,
        compiler_params=_cparams(("parallel", "arbitrary", "arbitrary")),
        name=name,
    )(*args)


def _halo_specs(tm, width, col_map, m):
    per = tm // GRID_W
    last = m // GRID_W - 1
    return [pl.BlockSpec((tm, width), lambda i, *j: (i, col_map(*j))),
            pl.BlockSpec((GRID_W, width), lambda i, *j: (jnp.maximum(i * per - 1, 0), col_map(*j))),
            pl.BlockSpec((GRID_W, width), lambda i, *j: (jnp.minimum((i + 1) * per, last), col_map(*j)))]


def _mix_body(x_ref, up_ref, dn_ref, g_ref, sc_ref, sh_ref, mu_ref, *out_refs, tm, nbc, bpb, d):
    i = pl.program_id(0)
    g, sc, sh = g_ref[...], sc_ref[0], sh_ref[0]
    h = _rms_mod(x_ref[...].astype(F32), g, sc, sh)
    row = lax.broadcasted_iota(jnp.int32, (tm, 1), 0)

    def emit(lo, hi, shifted):
        hseg = h[:, lo:hi]
        xx = shifted - hseg
        for n, o_ref in enumerate(out_refs):
            o_ref[:, lo:hi] = (hseg + xx * mu_ref[n:n + 1, lo:hi]).astype(o_ref.dtype)

    @pl.when(i < nbc)
    def _():
        half = d // 2
        emit(0, half, jnp.where(row == 0, 0.0, pltpu.roll(h[:, :half], 1, axis=0)))
        emit(half, d, jnp.where(row == tm - 1, 0.0, pltpu.roll(h[:, half:], tm - 1, axis=0)))

    @pl.when(i >= nbc)
    def _():
        jb = (i - nbc) % bpb
        q = d // 4
        col = row % GRID_W
        emit(0, q, jnp.where(col == 0, 0.0, pltpu.roll(h[:, :q], 1, axis=0)))
        emit(q, 2 * q, jnp.where(col == GRID_W - 1, 0.0, pltpu.roll(h[:, q:2 * q], tm - 1, axis=0)))
        hu = _rms_mod(up_ref[...].astype(F32), g, sc, sh)[:, 2 * q:3 * q]
        hu = hu * jnp.where(jb > 0, 1.0, 0.0)
        emit(2 * q, 3 * q, jnp.concatenate([hu, h[:tm - GRID_W, 2 * q:3 * q]], axis=0))
        hd = _rms_mod(dn_ref[...].astype(F32), g, sc, sh)[:, 3 * q:]
        hd = hd * jnp.where(jb < bpb - 1, 1.0, 0.0)
        emit(3 * q, d, jnp.concatenate([h[GRID_W:, 3 * q:], hd], axis=0))


def _rwkv_mix(xs, g, sc, sh, mu, *, dm):
    m, d = xs.shape
    tm = dm.n_ctx
    nbc, bpb = dm.mc // tm, dm.n_lat // tm
    mod_spec = pl.BlockSpec((1, 1, d), lambda i: (dm.mod_index(i, tm), 0, 0))
    out_spec = pl.BlockSpec((tm, d), lambda i: (i, 0))
    return pl.pallas_call(
        functools.partial(_mix_body, tm=tm, nbc=nbc, bpb=bpb, d=d),
        grid=(m // tm,),
        in_specs=_halo_specs(tm, d, lambda: 0, m) + [
            pl.BlockSpec((1, d), lambda i: (0, 0)), mod_spec, mod_spec,
            pl.BlockSpec(mu.shape, lambda i: (0, 0))],
        out_specs=[out_spec] * 6,
        out_shape=[jax.ShapeDtypeStruct((m, d), BF16)] * 6,
        compiler_params=_cparams(("parallel",)),
        name="rwkv_mix",
    )(xs, xs, xs, g, sc, sh, mu)


def _conv_body(x_ref, up_ref, dn_ref, w_ref, b_ref, o_ref, *, tm, nbc, bpb, n_qcols, q_scale):
    i = pl.program_id(0)
    j = pl.program_id(1)
    row = lax.broadcasted_iota(jnp.int32, (tm, 1), 0)
    x = x_ref[...]
    scale = jnp.where(j < n_qcols, q_scale, 1.0)

    def finish(y):
        y = y + b_ref[...]
        o_ref[...] = (y * jax.nn.sigmoid(y) * scale).astype(o_ref.dtype)

    @pl.when(i < nbc)
    def _():
        prev = jnp.where(row == 0, 0.0, pltpu.roll(x, 1, axis=0))
        nxt = jnp.where(row == tm - 1, 0.0, pltpu.roll(x, tm - 1, axis=0))
        finish(prev * w_ref[3:4, :] + x * w_ref[4:5, :] + nxt * w_ref[5:6, :])

    @pl.when(i >= nbc)
    def _():
        jb = (i - nbc) % bpb
        te = tm + 2 * GRID_W
        ext = jnp.concatenate([up_ref[...] * jnp.where(jb > 0, 1.0, 0.0), x,
                               dn_ref[...] * jnp.where(jb < bpb - 1, 1.0, 0.0)], axis=0)
        col = lax.broadcasted_iota(jnp.int32, (te, 1), 0) % GRID_W
        taps = (jnp.where(col == 0, 0.0, pltpu.roll(ext, 1, axis=0)), ext,
                jnp.where(col == GRID_W - 1, 0.0, pltpu.roll(ext, te - 1, axis=0)))
        y = jnp.zeros((tm, x.shape[1]), F32)
        for di in range(3):
            for dj in range(3):
                y = y + taps[dj][di * GRID_W:di * GRID_W + tm] * w_ref[3 * di + dj:3 * di + dj + 1, :]
        finish(y)


def _mlstm_conv(u, conv_w, conv_b, *, dm, width, tc, q_scale):
    m = u.shape[0]
    tm = dm.n_ctx
    nbc, bpb = dm.mc // tm, dm.n_lat // tm
    w9 = conv_w.reshape(9, width)
    return pl.pallas_call(
        functools.partial(_conv_body, tm=tm, nbc=nbc, bpb=bpb, n_qcols=width // 2 // tc,
                          q_scale=q_scale),
        grid=(m // tm, width // tc),
        in_specs=_halo_specs(tm, tc, lambda j: j, m) + [
            pl.BlockSpec((9, tc), lambda i, j: (0, j)), pl.BlockSpec((1, tc), lambda i, j: (0, j))],
        out_specs=pl.BlockSpec((tm, tc), lambda i, j: (i, j)),
        out_shape=jax.ShapeDtypeStruct((m, width), F32),
        compiler_params=_cparams(("parallel", "parallel")),
        name="mlstm_conv",
    )(u, u, u, w9, conv_b[None, :])


def _rwkv_scan_body(r_ref, k_ref, v_ref, wp_ref, ap_ref, kk_ref, ka_ref, rkp_ref, w0_ref, a0_ref,
                    y_ref, rko_ref, s_ref, sz_ref, w_s, k_s, z_s, b_s, *, tblk, n):
    zdir = pl.program_id(0)
    tb = pl.program_id(2)

    @pl.when(tb == 0)
    def _():
        s_ref[...] = jnp.zeros_like(s_ref)

    def prep(t, carry):
        k_t = k_ref[t]
        a = jax.nn.sigmoid(a0_ref[...] + ap_ref[t])
        w_s[t] = jnp.exp(-jnp.exp(-0.5) * jax.nn.sigmoid(w0_ref[...] + wp_ref[t]))
        kr = k_t * kk_ref[...]
        nrm = jnp.sqrt(jnp.sum(kr * kr, axis=0, keepdims=True))
        kkn = kr / jnp.maximum(nrm, 1e-12)
        z_s[t] = -kkn
        b_s[t] = kkn * a
        km = k_t * (1.0 + (a - 1.0) * ka_ref[...])
        k_s[t] = km
        rko_ref[t] = jnp.sum(r_ref[t] * km * rkp_ref[...], axis=0, keepdims=True)
        return carry

    lax.fori_loop(0, tblk, prep, 0)

    bwd = zdir == 1
    t_first = jnp.where(bwd, tblk - 1, 0)
    t_step = jnp.where(bwd, -1, 1)

    def row(ref, t, kk):
        return jnp.broadcast_to(ref[t, pl.ds(kk, 1), :], (n, LANES))

    acc = jnp.zeros((n, LANES), F32)
    for kk in range(n):
        acc = acc + s_ref[kk] * row(z_s, t_first, kk)
    sz_ref[...] = acc

    def step(i, carry):
        t = t_first + i * t_step
        t_next = jnp.clip(t + t_step, 0, tblk - 1)
        sz = sz_ref[...]
        v_t = v_ref[t]
        y = jnp.zeros((n, LANES), F32)
        sz_next = jnp.zeros((n, LANES), F32)
        for kk in range(n):
            s_new = (s_ref[kk] * row(w_s, t, kk) + sz * row(b_s, t, kk)
                     + v_t * row(k_s, t, kk))
            s_ref[kk] = s_new
            y = y + s_new * row(r_ref, t, kk)
            sz_next = sz_next + s_new * row(z_s, t_next, kk)
        y_ref[t] = y
        sz_ref[...] = sz_next
        return carry

    lax.fori_loop(0, tblk, step, 0)


def _rwkv_scan(r, k, v, wp, ap, kk_p, ka_p, rk_p, w0_p, a0_p, *, n_ctx, tblk):
    s_len, n, c = r.shape
    assert c % LANES == 0 and n_ctx % tblk == 0 and s_len % tblk == 0
    nbc, nb = n_ctx // tblk, s_len // tblk

    def tmap(zd, p):
        rev = jnp.where(p < nbc, nbc - 1 - p, nb - 1 - (p - nbc))
        return jnp.where(zd == 0, p, rev)

    shared = pl.BlockSpec((tblk, n, LANES), lambda zd, g, p: (tmap(zd, p), 0, g))
    perdir = pl.BlockSpec((None, tblk, n, LANES), lambda zd, g, p: (zd, tmap(zd, p), 0, g))
    par = pl.BlockSpec((n, LANES), lambda zd, g, p: (0, g))
    par_dir = pl.BlockSpec((None, n, LANES), lambda zd, g, p: (zd, 0, g))
    step_buf = pltpu.VMEM((tblk, n, LANES), F32)
    return pl.pallas_call(
        functools.partial(_rwkv_scan_body, tblk=tblk, n=n),
        grid=(2, c // LANES, nb),
        in_specs=[shared, shared, shared, perdir, perdir, par, par, par, par_dir, par_dir],
        out_specs=[perdir, pl.BlockSpec((None, tblk, 1, LANES), lambda zd, g, p: (zd, tmap(zd, p), 0, g))],
        out_shape=[jax.ShapeDtypeStruct((2, s_len, n, c), F32),
                   jax.ShapeDtypeStruct((2, s_len, 1, c), F32)],
        scratch_shapes=[pltpu.VMEM((n, n, LANES), F32), pltpu.VMEM((n, LANES), F32),
                        step_buf, step_buf, step_buf, step_buf],
        compiler_params=_cparams(("parallel", "parallel", "arbitrary")),
        name="rwkv_scan",
    )(r, k, v, wp, ap, kk_p, ka_p, rk_p, w0_p, a0_p)


def _rwkv_norm_body(y_ref, v_ref, rk_ref, lw_ref, lb_ref, o_ref, *, eps):
    ys = y_ref[0] + y_ref[1]
    mu = jnp.mean(ys, axis=1, keepdims=True)
    var = jnp.mean(jnp.square(ys - mu), axis=1, keepdims=True)
    yn = (ys - mu) * lax.rsqrt(var + eps) * lw_ref[...] + lb_ref[...]
    o_ref[...] = yn + (rk_ref[0] + rk_ref[1]) * v_ref[...]


def _rwkv_norm(y, v, rk, ln_w_p, ln_b_p, *, tblk, eps):
    _, s_len, n, c = y.shape
    par = pl.BlockSpec((n, LANES), lambda p, g: (0, g))
    return pl.pallas_call(
        functools.partial(_rwkv_norm_body, eps=eps),
        grid=(s_len // tblk, c // LANES),
        in_specs=[pl.BlockSpec((2, tblk, n, LANES), lambda p, g: (0, p, 0, g)),
                  pl.BlockSpec((tblk, n, LANES), lambda p, g: (p, 0, g)),
                  pl.BlockSpec((2, tblk, 1, LANES), lambda p, g: (0, p, 0, g)), par, par],
        out_specs=pl.BlockSpec((tblk, n, LANES), lambda p, g: (p, 0, g)),
        out_shape=jax.ShapeDtypeStruct((s_len, n, c), F32),
        compiler_params=_cparams(("parallel", "parallel")),
        name="rwkv_norm",
    )(y, v, rk, ln_w_p, ln_b_p)


def _mlstm_body(q_ref, k_ref, v_ref, ic_ref, fc_ref, ir_ref, fr_ref, o_ref,
                c_ref, n_ref, m_ref, *, nh, dk, dv):
    zdir = pl.program_id(0)
    p = pl.program_id(2)

    @pl.when(p == 0)
    def _():
        c_ref[...] = jnp.zeros_like(c_ref)
        n_ref[...] = jnp.zeros_like(n_ref)
        m_ref[...] = jnp.zeros_like(m_ref)

    ti = lax.broadcasted_iota(jnp.int32, (CHUNK, CHUNK), 0)
    tj = lax.broadcasted_iota(jnp.int32, (CHUNK, CHUNK), 1)
    mask = (ti - tj) * (1 - 2 * zdir) >= 0
    tri = mask.astype(F32)
    fcol = fc_ref[0]
    icol = ic_ref[0]
    frow = fr_ref[0, 0]
    irow = ir_ref[0, 0]
    hi = lax.Precision.HIGHEST
    bcum_col = jnp.dot(tri, fcol, precision=hi, preferred_element_type=F32)
    bcum_row = lax.dot_general(frow, tri, (((1,), (1,)), ((), ())), precision=hi,
                               preferred_element_type=F32)

    for h in range(nh):
        q32 = q_ref[:, h * dk:(h + 1) * dk].astype(F32)
        q = q32.astype(BF16)
        k32 = k_ref[:, h * dk:(h + 1) * dk].astype(F32)
        v = v_ref[:, h * dv:(h + 1) * dv].astype(BF16)
        c_st = c_ref[h]
        n_st = n_ref[h]
        m_st = m_ref[h][:, :1]
        bc = bcum_col[:, h:h + 1]
        br = bcum_row[h:h + 1, :]
        log_d = jnp.where(mask, bc - br + irow[h:h + 1, :], -jnp.inf)
        log_inter = bc + m_st
        m_t = jnp.maximum(log_inter, jnp.max(log_d, axis=-1, keepdims=True))
        qk = lax.dot_general(q, k32.astype(BF16), (((1,), (1,)), ((), ())),
                             preferred_element_type=F32)
        s = qk * jnp.exp(log_d - m_t)
        w_inter = jnp.exp(log_inter - m_t)
        num = (w_inter * jnp.dot(q, c_st.astype(BF16), preferred_element_type=F32)
               + jnp.dot(s.astype(BF16), v, preferred_element_type=F32))
        qn = jnp.sum(q32 * n_st, axis=-1, keepdims=True)
        den = w_inter * qn + jnp.sum(s, axis=-1, keepdims=True)
        o_ref[0, :, h * dv:(h + 1) * dv] = num / jnp.maximum(jnp.abs(den), jnp.exp(-m_t))

        b_end = jnp.sum(frow[h:h + 1, :], axis=-1, keepdims=True)
        a_col = b_end - bc + icol[:, h:h + 1]
        m_new = jnp.maximum(b_end + m_st, jnp.max(a_col, axis=0, keepdims=True))
        wk = jnp.exp(a_col - m_new) * k32
        dec = jnp.exp(b_end + m_st - m_new)
        c_ref[h] = dec * c_st + lax.dot_general(wk.astype(BF16), v, (((0,), (0,)), ((), ())),
                                                preferred_element_type=F32)
        n_ref[h] = dec * n_st + jnp.sum(wk, axis=0, keepdims=True)
        m_ref[h] = jnp.broadcast_to(m_new, (1, LANES))


def _mlstm_scan(qk, u, v_col0, ig, lf, *, nb, n_ctx, n_lat):
    m = qk.shape[0]
    nh = MLSTM_HEADS
    dkt = qk.shape[1] // 2
    dk = dkt // nh
    dvt = 2 * dkt
    dv = dvt // nh
    ncc, ncl = n_ctx // CHUNK, n_lat // CHUNK
    nchunks = ncc + ncl
    ctx_blocks = nb * ncc
    assert v_col0 % dvt == 0

    def rmap(zd, bb, p):
        rev = jnp.where(p < ncc, ncc - 1 - p, nchunks - 1 - (p - ncc))
        ch = jnp.where(zd == 0, p, rev)
        return jnp.where(ch < ncc, bb * ncc + ch, ctx_blocks + bb * ncl + (ch - ncc))

    ig_row = jnp.swapaxes(ig.reshape(2, m // CHUNK, CHUNK, nh), 2, 3)
    lf_row = jnp.swapaxes(lf.reshape(2, m // CHUNK, CHUNK, nh), 2, 3)
    col_spec = pl.BlockSpec((1, CHUNK, nh), lambda zd, bb, p: (zd, rmap(zd, bb, p), 0))
    row_spec = pl.BlockSpec((1, 1, nh, CHUNK), lambda zd, bb, p: (zd, rmap(zd, bb, p), 0, 0))
    return pl.pallas_call(
        functools.partial(_mlstm_body, nh=nh, dk=dk, dv=dv),
        grid=(2, nb, nchunks),
        in_specs=[pl.BlockSpec((CHUNK, dkt), lambda zd, bb, p: (rmap(zd, bb, p), 0)),
                  pl.BlockSpec((CHUNK, dkt), lambda zd, bb, p: (rmap(zd, bb, p), 1)),
                  pl.BlockSpec((CHUNK, dvt), lambda zd, bb, p: (rmap(zd, bb, p), v_col0 // dvt)),
                  col_spec, col_spec, row_spec, row_spec],
        out_specs=pl.BlockSpec((1, CHUNK, dvt), lambda zd, bb, p: (zd, rmap(zd, bb, p), 0)),
        out_shape=jax.ShapeDtypeStruct((2, m, dvt), F32),
        scratch_shapes=[pltpu.VMEM((nh, dk, dv), F32), pltpu.VMEM((nh, 1, dk), F32),
                        pltpu.VMEM((nh, 1, LANES), F32)],
        compiler_params=_cparams(("parallel", "parallel", "arbitrary")),
        name="mlstm_scan",
    )(qk, qk, u, ig, lf, ig_row, lf_row)


class _Dims:
    def __init__(self, nb, n_ctx, n_lat, d):
        self.nb, self.n_ctx, self.n_lat, self.d = nb, n_ctx, n_lat, d
        self.mc, self.mx = nb * n_ctx, nb * n_lat
        self.m = self.mc + self.mx
        self.tm = 1024 if (self.mc % 1024 == 0 and n_lat % 1024 == 0) else n_ctx
        self.tm_half = max(self.tm // 2, n_ctx) if self.tm > n_ctx else self.tm
        assert self.mc % self.tm == 0 and n_lat % self.tm == 0 and n_ctx % GRID_W == 0

    def mod_index(self, i, tm):
        nbc = self.mc // tm
        bpb = self.n_lat // tm
        return jnp.where(i < nbc, self.nb, (i - nbc) // bpb)

    def to_scan(self, a):
        a_c = a[:self.mc].reshape(self.nb, self.n_ctx, -1)
        a_x = a[self.mc:].reshape(self.nb, self.n_lat, -1)
        s = jnp.concatenate([a_c, a_x], axis=1)
        s = s.reshape(self.nb, self.n_ctx + self.n_lat, -1, RWKV_HEAD)
        return jnp.transpose(s, (1, 3, 0, 2)).reshape(self.n_ctx + self.n_lat, RWKV_HEAD, -1)

    def from_scan(self, y):
        s_len = y.shape[0]
        y = y.reshape(s_len, RWKV_HEAD, self.nb, -1)
        y = jnp.transpose(y, (2, 0, 3, 1)).reshape(self.nb, s_len, -1)
        return jnp.concatenate([y[:, :self.n_ctx].reshape(self.mc, -1),
                                y[:, self.n_ctx:].reshape(self.mx, -1)], axis=0)

    def chan_to_scan(self, p):
        lead = p.shape[:-1]
        pt = jnp.swapaxes(p.reshape(lead + (-1, RWKV_HEAD)), -1, -2)
        pt = jnp.broadcast_to(pt[..., None, :], lead + (RWKV_HEAD, self.nb, pt.shape[-1]))
        return pt.reshape(lead + (RWKV_HEAD, -1))


def _pad_to(a, axis, mult):
    pad = -a.shape[axis] % mult
    if pad == 0:
        return a
    widths = [(0, 0)] * a.ndim
    widths[axis] = (0, pad)
    return jnp.pad(a, widths)


def _rwkv_layer(dm, xs, g1, sc1, sh1, p, v_first):
    m, d, tm = dm.m, dm.d, dm.tm
    xr, xw, xk, xv, xa, xg = _rwkv_mix(xs, g1, sc1, sh1, p['mu'], dm=dm)
    mm = functools.partial(_matmul, m=m, tm=tm)
    r = mm(xr, p['w_r'], kdim=d, tn=512, name="mm_r")
    k = mm(xk, p['w_k'], kdim=d, tn=512, name="mm_k")
    v = mm(xv, p['w_v'], kdim=d, tn=512, name="mm_v")
    row_spec = pl.BlockSpec((tm, 512), lambda i, j, k: (i, j))
    vec_spec = pl.BlockSpec((1, 512), lambda i, j, k: (0, j))
    if v_first is not None:
        lv = mm(xv, p['v1'], kdim=d, tn=LANES, name="mm_v1")
        v = mm(lv, p['v2'], kdim=LANES, tn=512, name="mm_vres",
               epi=(_epi_vres, [v, v_first, p['v0']], [row_spec, row_spec, vec_spec]))
    lw = mm(xw, p['w1'], kdim=d, tn=2 * LANES, out_dtype=BF16, name="mm_w1", epi=(jnp.tanh, [], []))
    la = mm(xa, p['a1'], kdim=d, tn=2 * LANES, out_dtype=BF16, name="mm_a1")
    lg = mm(xg, p['g1'], kdim=d, tn=p['g1'].shape[1], out_dtype=BF16, name="mm_g1",
            epi=(jax.nn.sigmoid, [], []))
    g = mm(lg, p['g2'], kdim=p['g2'].shape[0], tn=512, name="mm_g2")
    wp = jnp.stack([dm.to_scan(mm(lw, p['w2'][zd], kdim=LANES, x_col0=zd * LANES, tn=512,
                                  name="mm_w2")) for zd in range(2)])
    ap = jnp.stack([dm.to_scan(mm(la, p['a2'][zd], kdim=LANES, x_col0=zd * LANES, tn=512,
                                  name="mm_a2")) for zd in range(2)])
    cs = dm.chan_to_scan
    v_s = dm.to_scan(v)
    y, rk = _rwkv_scan(dm.to_scan(r), dm.to_scan(k), v_s, wp, ap,
                       cs(p['k_k']), cs(p['k_a']), cs(p['r_k']), cs(p['w0']), cs(p['a0']),
                       n_ctx=dm.n_ctx, tblk=32)
    pre = _rwkv_norm(y, v_s, rk, cs(p['ln_w']), cs(p['ln_b']), tblk=CHUNK, eps=RWKV_HEAD * 1e-5)
    return dm.from_scan(pre), g, v


def _mlstm_layer(dm, xs, g1, sc1, sh1, p):
    m, d, tm = dm.m, dm.d, dm.tm
    nh = MLSTM_HEADS
    mod_spec = pl.BlockSpec((1, 1, d), lambda i, j, k: (dm.mod_index(i, tm), 0, 0))
    u = _matmul(None, p['w_in'], m=m, kdim=d, tm=tm, tn=896, name="mm_mlstm_in",
                pro=(_pro_normmod, [xs, g1, sc1, sh1],
                     [pl.BlockSpec((tm, d), lambda i, j, k: (i, 0)),
                      pl.BlockSpec((1, d), lambda i, j, k: (0, 0)), mod_spec, mod_spec]))
    qk = _mlstm_conv(u, p['conv_w'], p['conv_b'], dm=dm, width=d, tc=min(512, d // 2),
                     q_scale=float(d // 2 // nh) ** -0.5)
    gates = u[:, 3 * d:3 * d + 4 * nh].reshape(m, 2, 2, nh) + p['b_gate']
    gates = GATE_CAP * jnp.tanh(gates / GATE_CAP)
    ig = jnp.moveaxis(gates[:, :, 0, :], 1, 0)
    lf = jnp.moveaxis(jax.nn.log_sigmoid(gates[:, :, 1, :]), 1, 0)
    hs = _mlstm_scan(qk, u, d, ig, lf, nb=dm.nb, n_ctx=dm.n_ctx, n_lat=dm.n_lat)
    return hs, u


def kernel(x, c, ctx, c_ctx, mod_w, mod_b, norm_g, final_g, rwkv_mu, rwkv_w_r, rwkv_w_k, rwkv_w_v, rwkv_w_o, rwkv_w0, rwkv_w1, rwkv_w2, rwkv_a0, rwkv_a1, rwkv_a2, rwkv_g1, rwkv_g2, rwkv_k_k, rwkv_k_a, rwkv_r_k, rwkv_ln_w, rwkv_ln_b, rwkv_v0, rwkv_v1, rwkv_v2, mlstm_w_in, mlstm_b_gate, mlstm_conv_w, mlstm_conv_b, mlstm_norm_w, mlstm_w_out, ffn_w_in, ffn_w_out):
    nb, n_lat, d = x.shape
    n_ctx = ctx.shape[1]
    depth = mod_w.shape[0]
    d_ff = ffn_w_out.shape[1]
    dm = _Dims(nb, n_ctx, n_lat, d)
    m, tm, tmh = dm.m, dm.tm, dm.tm_half
    bf = lambda a: a.astype(BF16)

    cond = jax.nn.silu(jnp.concatenate([c, c_ctx[None, :]], axis=0))
    rows = cond.shape[0]
    cond = _pad_to(cond, 0, 2 * SUBLANES)
    mod_w2 = mod_w.reshape(depth * d, 6 * d)
    mods = []
    for i in range(depth):
        mo = _matmul(cond, mod_w2, m=cond.shape[0], kdim=d, tm=cond.shape[0], tn=512,
                     w_row0=i * d, name="mm_mod") + mod_b[i]
        mods.append(mo[:rows].reshape(rows, 6, 1, d))

    def gate_res(res, gate, tm_):
        return (_epi_gate_res, [res, gate],
                [pl.BlockSpec((tm_, 512), lambda i, j, k: (i, j)),
                 pl.BlockSpec((1, 1, 512), lambda i, j, k: (dm.mod_index(i, tm_), 0, j))])

    xs = jnp.concatenate([ctx.reshape(dm.mc, d), x.reshape(dm.mx, d)], axis=0)
    v_first = None
    for i in range(depth):
        j = i // 2
        sh1, sc1, gt1, sh2, sc2, gt2 = [mods[i][:, n] for n in range(6)]
        g1 = norm_g[i, 0][None, :]
        g2 = norm_g[i, 1][None, :]
        full_k = lambda tm_: pl.BlockSpec((tm_, d), lambda i, j, k: (i, 0))
        if i % 2 == 0:
            lora = lambda a: bf(jnp.concatenate([_pad_to(a[0], 1, LANES), _pad_to(a[1], 1, LANES)], axis=1))
            p = {'mu': rwkv_mu[j], 'w_r': bf(rwkv_w_r[j]), 'w_k': bf(rwkv_w_k[j]), 'w_v': bf(rwkv_w_v[j]),
                 'w0': rwkv_w0[j], 'w1': lora(rwkv_w1[j]), 'w2': bf(_pad_to(rwkv_w2[j], 1, LANES)),
                 'a0': rwkv_a0[j], 'a1': lora(rwkv_a1[j]), 'a2': bf(_pad_to(rwkv_a2[j], 1, LANES)),
                 'g1': bf(rwkv_g1[j]), 'g2': bf(rwkv_g2[j]),
                 'k_k': rwkv_k_k[j], 'k_a': rwkv_k_a[j], 'r_k': rwkv_r_k[j].reshape(-1),
                 'ln_w': rwkv_ln_w[j], 'ln_b': rwkv_ln_b[j]}
            if j > 0:
                p['v0'] = rwkv_v0[j - 1][None, :]
                p['v1'] = bf(_pad_to(rwkv_v1[j - 1], 1, LANES))
                p['v2'] = bf(_pad_to(rwkv_v2[j - 1], 0, LANES))
            pre, g, v_cur = _rwkv_layer(dm, xs, g1, sc1, sh1, p, v_first if j > 0 else None)
            if j == 0:
                v_first = v_cur
            xs = _matmul(None, bf(rwkv_w_o[j]), m=m, kdim=d, tm=tmh, tn=512, name="mm_rwkv_out",
                         pro=(_pro_mulg, [pre, g], [full_k(tmh), full_k(tmh)]),
                         epi=gate_res(xs, gt1, tmh))
        else:
            p = {'w_in': bf(_pad_to(mlstm_w_in[j], 1, 896)),
                 'b_gate': mlstm_b_gate[j], 'conv_w': mlstm_conv_w[j], 'conv_b': mlstm_conv_b[j]}
            hs, u = _mlstm_layer(dm, xs, g1, sc1, sh1, p)
            xs = _matmul(None, bf(mlstm_w_out[j]), m=m, kdim=d, tm=tmh, tn=512, name="mm_mlstm_out",
                         pro=(_pro_mlstm_read, [hs, u, mlstm_norm_w[j][None, :]],
                              [pl.BlockSpec((2, tmh, d), lambda i, j, k: (0, i, 0)),
                               pl.BlockSpec((tmh, d), lambda i, j, k: (i, 2)),
                               pl.BlockSpec((1, d), lambda i, j, k: (0, 0))]),
                         epi=gate_res(xs, gt1, tmh))
        mod_spec = pl.BlockSpec((1, 1, d), lambda i, j, k: (dm.mod_index(i, tm), 0, 0))
        hid = _matmul(None, bf(ffn_w_in[i]), m=m, kdim=d, tm=tm, tn=512, out_dtype=BF16,
                      n_out=d_ff, w2_col0=d_ff, name="mm_ffn_in",
                      pro=(_pro_normmod, [xs, g2, sc2, sh2],
                           [full_k(tm), pl.BlockSpec((1, d), lambda i, j, k: (0, 0)), mod_spec, mod_spec]),
                      epi=(_epi_swiglu, [], []))
        xs = _matmul(hid, bf(ffn_w_out[i]), m=m, kdim=d_ff, tm=tm, tn=512, tk=d_ff // 2,
                     name="mm_ffn_out", epi=gate_res(xs, gt2, tm))
    xl = xs[dm.mc:].reshape(nb, n_lat, d)
    return xl * lax.rsqrt(jnp.mean(xl * xl, axis=-1, keepdims=True) + NORM_EPS) * final_g
```

```python
import functools

import jax
import jax.numpy as jnp
from jax import lax
from jax.experimental import pallas as pl
from jax.experimental.pallas import tpu as pltpu

F32 = jnp.float32
BF16 = jnp.bfloat16

GRID_W = 64
NORM_EPS = 1e-6
RWKV_HEAD = 64
MLSTM_HEADS = 8
CHUNK = 64
GATE_CAP = 15.0
LANES = 128
SUBLANES = 8
VMEM_LIMIT = 56 * 1024 * 1024


def _cparams(sem):
    return pltpu.CompilerParams(dimension_semantics=sem, vmem_limit_bytes=VMEM_LIMIT)


def _rms_mod(x, g, sc, sh):
    ms = jnp.mean(x * x, axis=-1, keepdims=True)
    return x * lax.rsqrt(ms + NORM_EPS) * g * (1.0 + sc) + sh


def _head_ln_lanes(y, nheads, eps):
    hd = y.shape[-1] // nheads
    out = []
    for h in range(nheads):
        seg = y[:, h * hd:(h + 1) * hd]
        mu = jnp.mean(seg, axis=-1, keepdims=True)
        var = jnp.mean(jnp.square(seg - mu), axis=-1, keepdims=True)
        out.append((seg - mu) * lax.rsqrt(var + eps))
    return jnp.concatenate(out, axis=-1)


def _pro_normmod(x_ref, g_ref, sc_ref, sh_ref):
    return _rms_mod(x_ref[...].astype(F32), g_ref[...], sc_ref[0], sh_ref[0]).astype(BF16)


def _pro_mulg(x_ref, g_ref):
    return (x_ref[...] * g_ref[...]).astype(BF16)


def _pro_mlstm_read(h_ref, o_ref, nw_ref):
    hn = _head_ln_lanes(h_ref[0] + h_ref[1], MLSTM_HEADS, NORM_EPS) * nw_ref[...]
    return (hn * jax.nn.sigmoid(o_ref[...])).astype(BF16)


def _epi_swiglu(acc, acc2):
    return acc * jax.nn.sigmoid(acc) * acc2


def _epi_gate_res(acc, res_ref, gate_ref):
    return res_ref[...] + gate_ref[0] * acc


def _epi_vres(acc, v_ref, vf_ref, v0_ref):
    v = v_ref[...]
    return v + (vf_ref[...] - v) * jax.nn.sigmoid(v0_ref[...] + acc)


def _mm_body(*refs, nk, n_pro, pro_fn, n_epi, epi_fn, dual):
    it = iter(refs)
    pro_refs = [next(it) for _ in range(n_pro)]
    w_ref = next(it)
    w2_ref = next(it) if dual else None
    epi_refs = [next(it) for _ in range(n_epi)]
    o_ref = next(it)
    acc_ref = next(it)
    acc2_ref = next(it) if dual else None
    xb_ref = next(it) if pro_fn is not None else None

    j = pl.program_id(1)
    k = pl.program_id(2)

    if pro_fn is not None:
        @pl.when(j == 0)
        def _():
            xb_ref[...] = pro_fn(*pro_refs)
        xb = xb_ref[...]
    else:
        xb = pro_refs[0][...].astype(BF16)

    @pl.when(k == 0)
    def _():
        acc_ref[...] = jnp.zeros_like(acc_ref)
        if dual:
            acc2_ref[...] = jnp.zeros_like(acc2_ref)

    acc_ref[...] += jnp.dot(xb, w_ref[...].astype(BF16), preferred_element_type=F32)
    if dual:
        acc2_ref[...] += jnp.dot(xb, w2_ref[...].astype(BF16), preferred_element_type=F32)

    @pl.when(k == nk - 1)
    def _():
        acc = acc_ref[...]
        if dual:
            acc = epi_fn(acc, acc2_ref[...], *epi_refs)
        elif epi_fn is not None:
            acc = epi_fn(acc, *epi_refs)
        o_ref[...] = acc.astype(o_ref.dtype)


def _matmul(x, w, *, m, kdim, tm, tn, tk=None, out_dtype=F32, n_out=None, w_col0=0, w2_col0=None,
            w_row0=0, x_col0=0, pro=None, epi=None, name="mm"):
    n_out = w.shape[1] if n_out is None else n_out
    tk = kdim if tk is None else tk
    dual = w2_col0 is not None
    assert m % tm == 0 and n_out % tn == 0 and kdim % tk == 0
    assert w_col0 % tn == 0 and w_row0 % tk == 0 and x_col0 % tk == 0
    nk = kdim // tk
    c0, r0, xc0 = w_col0 // tn, w_row0 // tk, x_col0 // tk
    pro_fn, pro_args, pro_specs = pro if pro is not None else (None, [x], [
        pl.BlockSpec((tm, tk), lambda i, j, k: (i, k + xc0))])
    if pro_fn is not None:
        assert nk == 1
    epi_fn, epi_args, epi_specs = epi if epi is not None else (None, [], [])
    in_specs = list(pro_specs) + [pl.BlockSpec((tk, tn), lambda i, j, k: (k + r0, j + c0))]
    args = list(pro_args) + [w]
    if dual:
        assert w2_col0 % tn == 0
        c2 = w2_col0 // tn
        in_specs.append(pl.BlockSpec((tk, tn), lambda i, j, k: (k + r0, j + c2)))
        args.append(w)
    in_specs += list(epi_specs)
    args += list(epi_args)
    scratch = [pltpu.VMEM((tm, tn), F32)]
    if dual:
        scratch.append(pltpu.VMEM((tm, tn), F32))
    if pro_fn is not None:
        scratch.append(pltpu.VMEM((tm, tk), BF16))
    return pl.pallas_call(
        functools.partial(_mm_body, nk=nk, n_pro=len(pro_args), pro_fn=pro_fn,
                          n_epi=len(epi_args), epi_fn=epi_fn, dual=dual),
        grid=(m // tm, n_out // tn, nk),
        in_specs=in_specs,
        out_specs=pl.BlockSpec((tm, tn), lambda i, j, k: (i, j)),
        out_shape=jax.ShapeDtypeStruct((m, n_out), out_dtype),
        scratch_shapes=scratch,
        compiler_params=_cparams(("parallel", "arbitrary", "arbitrary")),
        name=name,
    )(*args)


def _halo_specs(tm, width, col_map, m):
    per = tm // GRID_W
    last = m // GRID_W - 1
    return [pl.BlockSpec((tm, width), lambda i, *j: (i, col_map(*j))),
            pl.BlockSpec((GRID_W, width), lambda i, *j: (jnp.maximum(i * per - 1, 0), col_map(*j))),
            pl.BlockSpec((GRID_W, width), lambda i, *j: (jnp.minimum((i + 1) * per, last), col_map(*j)))]


def _mix_body(x_ref, up_ref, dn_ref, g_ref, sc_ref, sh_ref, mu_ref, *out_refs, tm, nbc, bpb, d):
    i = pl.program_id(0)
    g, sc, sh = g_ref[...], sc_ref[0], sh_ref[0]
    h = _rms_mod(x_ref[...].astype(F32), g, sc, sh)
    row = lax.broadcasted_iota(jnp.int32, (tm, 1), 0)

    def emit(lo, hi, shifted):
        hseg = h[:, lo:hi]
        xx = shifted - hseg
        for n, o_ref in enumerate(out_refs):
            o_ref[:, lo:hi] = (hseg + xx * mu_ref[n:n + 1, lo:hi]).astype(o_ref.dtype)

    @pl.when(i < nbc)
    def _():
        half = d // 2
        emit(0, half, jnp.where(row == 0, 0.0, pltpu.roll(h[:, :half], 1, axis=0)))
        emit(half, d, jnp.where(row == tm - 1, 0.0, pltpu.roll(h[:, half:], tm - 1, axis=0)))

    @pl.when(i >= nbc)
    def _():
        jb = (i - nbc) % bpb
        q = d // 4
        col = row % GRID_W
        emit(0, q, jnp.where(col == 0, 0.0, pltpu.roll(h[:, :q], 1, axis=0)))
        emit(q, 2 * q, jnp.where(col == GRID_W - 1, 0.0, pltpu.roll(h[:, q:2 * q], tm - 1, axis=0)))
        hu = _rms_mod(up_ref[...].astype(F32), g, sc, sh)[:, 2 * q:3 * q]
        hu = hu * jnp.where(jb > 0, 1.0, 0.0)
        emit(2 * q, 3 * q, jnp.concatenate([hu, h[:tm - GRID_W, 2 * q:3 * q]], axis=0))
        hd = _rms_mod(dn_ref[...].astype(F32), g, sc, sh)[:, 3 * q:]
        hd = hd * jnp.where(jb < bpb - 1, 1.0, 0.0)
        emit(3 * q, d, jnp.concatenate([h[GRID_W:, 3 * q:], hd], axis=0))


def _rwkv_mix(xs, g, sc, sh, mu, *, dm):
    m, d = xs.shape
    tm = dm.n_ctx
    nbc, bpb = dm.mc // tm, dm.n_lat // tm
    mod_spec = pl.BlockSpec((1, 1, d), lambda i: (dm.mod_index(i, tm), 0, 0))
    out_spec = pl.BlockSpec((tm, d), lambda i: dm.time_major_block(i))
    return pl.pallas_call(
        functools.partial(_mix_body, tm=tm, nbc=nbc, bpb=bpb, d=d),
        grid=(m // tm,),
        in_specs=_halo_specs(tm, d, lambda: 0, m) + [
            pl.BlockSpec((1, d), lambda i: (0, 0)), mod_spec, mod_spec,
            pl.BlockSpec(mu.shape, lambda i: (0, 0))],
        out_specs=[out_spec] * 6,
        out_shape=[jax.ShapeDtypeStruct((dm.s_len, dm.nb * d), BF16)] * 6,
        compiler_params=_cparams(("parallel",)),
        name="rwkv_mix",
    )(xs, xs, xs, g, sc, sh, mu)


def _conv_body(x_ref, up_ref, dn_ref, w_ref, b_ref, o_ref, *, tm, nbc, bpb, n_qcols, q_scale):
    i = pl.program_id(0)
    j = pl.program_id(1)
    row = lax.broadcasted_iota(jnp.int32, (tm, 1), 0)
    x = x_ref[...]
    scale = jnp.where(j < n_qcols, q_scale, 1.0)

    def finish(y):
        y = y + b_ref[...]
        o_ref[...] = (y * jax.nn.sigmoid(y) * scale).astype(o_ref.dtype)

    @pl.when(i < nbc)
    def _():
        prev = jnp.where(row == 0, 0.0, pltpu.roll(x, 1, axis=0))
        nxt = jnp.where(row == tm - 1, 0.0, pltpu.roll(x, tm - 1, axis=0))
        finish(prev * w_ref[3:4, :] + x * w_ref[4:5, :] + nxt * w_ref[5:6, :])

    @pl.when(i >= nbc)
    def _():
        jb = (i - nbc) % bpb
        te = tm + 2 * GRID_W
        ext = jnp.concatenate([up_ref[...] * jnp.where(jb > 0, 1.0, 0.0), x,
                               dn_ref[...] * jnp.where(jb < bpb - 1, 1.0, 0.0)], axis=0)
        col = lax.broadcasted_iota(jnp.int32, (te, 1), 0) % GRID_W
        taps = (jnp.where(col == 0, 0.0, pltpu.roll(ext, 1, axis=0)), ext,
                jnp.where(col == GRID_W - 1, 0.0, pltpu.roll(ext, te - 1, axis=0)))
        y = jnp.zeros((tm, x.shape[1]), F32)
        for di in range(3):
            for dj in range(3):
                y = y + taps[dj][di * GRID_W:di * GRID_W + tm] * w_ref[3 * di + dj:3 * di + dj + 1, :]
        finish(y)


def _mlstm_conv(u, conv_w, conv_b, *, dm, width, tc, q_scale):
    m = u.shape[0]
    tm = dm.n_ctx
    nbc, bpb = dm.mc // tm, dm.n_lat // tm
    w9 = conv_w.reshape(9, width)
    return pl.pallas_call(
        functools.partial(_conv_body, tm=tm, nbc=nbc, bpb=bpb, n_qcols=width // 2 // tc,
                          q_scale=q_scale),
        grid=(m // tm, width // tc),
        in_specs=_halo_specs(tm, tc, lambda j: j, m) + [
            pl.BlockSpec((9, tc), lambda i, j: (0, j)), pl.BlockSpec((1, tc), lambda i, j: (0, j))],
        out_specs=pl.BlockSpec((tm, tc), lambda i, j: (i, j)),
        out_shape=jax.ShapeDtypeStruct((m, width), F32),
        compiler_params=_cparams(("parallel", "parallel")),
        name="mlstm_conv",
    )(u, u, u, w9, conv_b[None, :])


def _split_parity(ref, t, n):
    xt = ref[t].reshape(LANES, 2 * n).T
    return xt[:n], xt[n:]


def _rwkv_scan_body(r_ref, k_ref, v_ref, wp_ref, ap_ref, kk_ref, ka_ref, rkp_ref, w0_ref, a0_ref,
                    y_ref, rko_ref, s_ref, sz_ref, r_s, v_s, w_s, k_s, z_s, b_s, *, tblk, n):
    zdir = pl.program_id(0)
    tb = pl.program_id(2)

    @pl.when(tb == 0)
    def _():
        s_ref[...] = jnp.zeros_like(s_ref)

    def prep(t, carry):
        r2, k2, v2 = _split_parity(r_ref, t, n), _split_parity(k_ref, t, n), _split_parity(v_ref, t, n)
        wp2, ap2 = _split_parity(wp_ref, t, n), _split_parity(ap_ref, t, n)
        for g in range(2):
            k_t = k2[g]
            a = jax.nn.sigmoid(a0_ref[g] + ap2[g])
            w_s[t, g] = jnp.exp(-jnp.exp(-0.5) * jax.nn.sigmoid(w0_ref[g] + wp2[g]))
            kr = k_t * kk_ref[g]
            nrm = jnp.sqrt(jnp.sum(kr * kr, axis=0, keepdims=True))
            kkn = kr / jnp.maximum(nrm, 1e-12)
            z_s[t, g] = -kkn
            b_s[t, g] = kkn * a
            km = k_t * (1.0 + (a - 1.0) * ka_ref[g])
            k_s[t, g] = km
            r_s[t, g] = r2[g]
            v_s[t, g] = v2[g]
            rko_ref[t, g] = jnp.sum(r2[g] * km * rkp_ref[g], axis=0, keepdims=True)
        return carry

    lax.fori_loop(0, tblk, prep, 0, unroll=4)

    bwd = zdir == 1
    t_first = jnp.where(bwd, tblk - 1, 0)
    t_step = jnp.where(bwd, -1, 1)

    def row(ref, t, g, kk):
        return jnp.broadcast_to(ref[t, g, pl.ds(kk, 1), :], (n, LANES))

    for g in range(2):
        acc = jnp.zeros((n, LANES), F32)
        for kk in range(n):
            acc = acc + s_ref[g, kk] * row(z_s, t_first, g, kk)
        sz_ref[g] = acc

    def step(i, carry):
        t = t_first + i * t_step
        t_next = jnp.clip(t + t_step, 0, tblk - 1)
        for g in range(2):
            sz = sz_ref[g]
            v_t = v_s[t, g]
            y = jnp.zeros((n, LANES), F32)
            sz_next = jnp.zeros((n, LANES), F32)
            for kk in range(n):
                s_new = (s_ref[g, kk] * row(w_s, t, g, kk) + sz * row(b_s, t, g, kk)
                         + v_t * row(k_s, t, g, kk))
                s_ref[g, kk] = s_new
                y = y + s_new * row(r_s, t, g, kk)
                sz_next = sz_next + s_new * row(z_s, t_next, g, kk)
            y_ref[t, g] = y
            sz_ref[g] = sz_next
        return carry

    lax.fori_loop(0, tblk, step, 0)


def _rwkv_scan(r, k, v, wp, ap, kk_p, ka_p, rk_p, w0_p, a0_p, *, n_ctx, tblk):
    s_len, nbat, hpn, n2 = r.shape
    n = n2 // 2
    nbg = LANES // hpn
    assert n2 == LANES and nbat % nbg == 0 and n_ctx % tblk == 0 and s_len % tblk == 0
    ngrp = nbat // nbg
    nbc, nb = n_ctx // tblk, s_len // tblk

    def tmap(zd, p):
        rev = jnp.where(p < nbc, nbc - 1 - p, nb - 1 - (p - nbc))
        return jnp.where(zd == 0, p, rev)

    shared = pl.BlockSpec((tblk, nbg, hpn, n2), lambda zd, g, p: (tmap(zd, p), g, 0, 0))
    perdir = pl.BlockSpec((tblk, nbg, hpn, n2), lambda zd, g, p: (tmap(zd, p), g, zd, 0))
    par = pl.BlockSpec((2, n, LANES), lambda zd, g, p: (0, 0, 0))
    par_dir = pl.BlockSpec((None, 2, n, LANES), lambda zd, g, p: (zd, 0, 0, 0))
    step_buf = pltpu.VMEM((tblk, 2, n, LANES), F32)
    return pl.pallas_call(
        functools.partial(_rwkv_scan_body, tblk=tblk, n=n),
        grid=(2, ngrp, nb),
        in_specs=[shared, shared, shared, perdir, perdir, par, par, par, par_dir, par_dir],
        out_specs=[pl.BlockSpec((None, tblk, 2, n, LANES), lambda zd, g, p: (zd, tmap(zd, p), g, 0, 0)),
                   pl.BlockSpec((None, tblk, 2, 1, LANES), lambda zd, g, p: (zd, tmap(zd, p), g, 0, 0))],
        out_shape=[jax.ShapeDtypeStruct((2, s_len, 2 * ngrp, n, LANES), F32),
                   jax.ShapeDtypeStruct((2, s_len, 2 * ngrp, 1, LANES), F32)],
        scratch_shapes=[pltpu.VMEM((2, n, n, LANES), F32), pltpu.VMEM((2, n, LANES), F32)]
        + [step_buf] * 6,
        compiler_params=_cparams(("parallel", "parallel", "arbitrary")),
        name="rwkv_scan",
    )(r, k, v, wp, ap, kk_p, ka_p, rk_p, w0_p, a0_p)


def _rwkv_norm_body(y_ref, v_ref, rk_ref, lw_ref, lb_ref, o_ref, *, eps, tblk, n):
    def step(t, carry):
        v2 = _split_parity(v_ref, t, n)
        outs = []
        for g in range(2):
            ys = y_ref[0, t, g] + y_ref[1, t, g]
            mu = jnp.mean(ys, axis=0, keepdims=True)
            var = jnp.mean(jnp.square(ys - mu), axis=0, keepdims=True)
            yn = (ys - mu) * lax.rsqrt(var + eps) * lw_ref[g] + lb_ref[g]
            outs.append(yn + (rk_ref[0, t, g] + rk_ref[1, t, g]) * v2[g])
        o_ref[t] = jnp.concatenate(outs, axis=0).T.reshape(o_ref.shape[1:])
        return carry

    lax.fori_loop(0, tblk, step, 0, unroll=4)


def _rwkv_norm(y, v, rk, ln_w_p, ln_b_p, *, tblk, eps):
    s_len, nbat, hpn, n2 = v.shape
    n = n2 // 2
    nbg = LANES // hpn
    par = pl.BlockSpec((2, n, LANES), lambda p, g: (0, 0, 0))
    nat = pl.BlockSpec((tblk, nbg, hpn, n2), lambda p, g: (p, g, 0, 0))
    return pl.pallas_call(
        functools.partial(_rwkv_norm_body, eps=eps, tblk=tblk, n=n),
        grid=(s_len // tblk, nbat // nbg),
        in_specs=[pl.BlockSpec((2, tblk, 2, n, LANES), lambda p, g: (0, p, g, 0, 0)), nat,
                  pl.BlockSpec((2, tblk, 2, 1, LANES), lambda p, g: (0, p, g, 0, 0)), par, par],
        out_specs=nat,
        out_shape=jax.ShapeDtypeStruct(v.shape, F32),
        compiler_params=_cparams(("parallel", "parallel")),
        name="rwkv_norm",
    )(y, v, rk, ln_w_p, ln_b_p)


def _mlstm_body(q_ref, k_ref, v_ref, ic_ref, fc_ref, ir_ref, fr_ref, o_ref,
                c_ref, n_ref, m_ref, *, nh, dk, dv):
    zdir = pl.program_id(0)
    p = pl.program_id(2)

    @pl.when(p == 0)
    def _():
        c_ref[...] = jnp.zeros_like(c_ref)
        n_ref[...] = jnp.zeros_like(n_ref)
        m_ref[...] = jnp.zeros_like(m_ref)

    ti = lax.broadcasted_iota(jnp.int32, (CHUNK, CHUNK), 0)
    tj = lax.broadcasted_iota(jnp.int32, (CHUNK, CHUNK), 1)
    mask = (ti - tj) * (1 - 2 * zdir) >= 0
    tri = mask.astype(F32)
    fcol = fc_ref[0]
    icol = ic_ref[0]
    frow = fr_ref[0, 0]
    irow = ir_ref[0, 0]
    hi = lax.Precision.HIGHEST
    bcum_col = jnp.dot(tri, fcol, precision=hi, preferred_element_type=F32)
    bcum_row = lax.dot_general(frow, tri, (((1,), (1,)), ((), ())), precision=hi,
                               preferred_element_type=F32)

    hs = range(nh)
    q32 = [q_ref[:, h * dk:(h + 1) * dk].astype(F32) for h in hs]
    qb = [x.astype(BF16) for x in q32]
    k32 = [k_ref[:, h * dk:(h + 1) * dk].astype(F32) for h in hs]
    vb = [v_ref[:, h * dv:(h + 1) * dv].astype(BF16) for h in hs]
    c_st = [c_ref[h] for h in hs]
    n_st = [n_ref[h] for h in hs]
    m_st = [m_ref[h][:, :1] for h in hs]
    bc = [bcum_col[:, h:h + 1] for h in hs]
    log_d = [jnp.where(mask, bc[h] - bcum_row[h:h + 1, :] + irow[h:h + 1, :], -jnp.inf) for h in hs]
    log_inter = [bc[h] + m_st[h] for h in hs]
    m_t = [jnp.maximum(log_inter[h], jnp.max(log_d[h], axis=-1, keepdims=True)) for h in hs]
    b_end = [jnp.sum(frow[h:h + 1, :], axis=-1, keepdims=True) for h in hs]
    a_col = [b_end[h] - bc[h] + icol[:, h:h + 1] for h in hs]
    m_new = [jnp.maximum(b_end[h] + m_st[h], jnp.max(a_col[h], axis=0, keepdims=True)) for h in hs]
    qk = [lax.dot_general(qb[h], k32[h].astype(BF16), (((1,), (1,)), ((), ())),
                          preferred_element_type=F32) for h in hs]
    qc = [jnp.dot(qb[h], c_st[h].astype(BF16), preferred_element_type=F32) for h in hs]
    s = [qk[h] * jnp.exp(log_d[h] - m_t[h]) for h in hs]
    sv = [jnp.dot(s[h].astype(BF16), vb[h], preferred_element_type=F32) for h in hs]
    wk = [jnp.exp(a_col[h] - m_new[h]) * k32[h] for h in hs]
    kv = [lax.dot_general(wk[h].astype(BF16), vb[h], (((0,), (0,)), ((), ())),
                          preferred_element_type=F32) for h in hs]
    for h in hs:
        w_inter = jnp.exp(log_inter[h] - m_t[h])
        qn = jnp.sum(q32[h] * n_st[h], axis=-1, keepdims=True)
        den = w_inter * qn + jnp.sum(s[h], axis=-1, keepdims=True)
        num = w_inter * qc[h] + sv[h]
        o_ref[0, :, h * dv:(h + 1) * dv] = num / jnp.maximum(jnp.abs(den), jnp.exp(-m_t[h]))
        dec = jnp.exp(b_end[h] + m_st[h] - m_new[h])
        c_ref[h] = dec * c_st[h] + kv[h]
        n_ref[h] = dec * n_st[h] + jnp.sum(wk[h], axis=0, keepdims=True)
        m_ref[h] = jnp.broadcast_to(m_new[h], (1, LANES))


def _mlstm_scan(qk, u, v_col0, ig, lf, *, nb, n_ctx, n_lat):
    m = qk.shape[0]
    nh = MLSTM_HEADS
    dkt = qk.shape[1] // 2
    dk = dkt // nh
    dvt = 2 * dkt
    dv = dvt // nh
    ncc, ncl = n_ctx // CHUNK, n_lat // CHUNK
    nchunks = ncc + ncl
    ctx_blocks = nb * ncc
    assert v_col0 % dvt == 0

    def rmap(zd, bb, p):
        rev = jnp.where(p < ncc, ncc - 1 - p, nchunks - 1 - (p - ncc))
        ch = jnp.where(zd == 0, p, rev)
        return jnp.where(ch < ncc, bb * ncc + ch, ctx_blocks + bb * ncl + (ch - ncc))

    ig_row = jnp.swapaxes(ig.reshape(2, m // CHUNK, CHUNK, nh), 2, 3)
    lf_row = jnp.swapaxes(lf.reshape(2, m // CHUNK, CHUNK, nh), 2, 3)
    col_spec = pl.BlockSpec((1, CHUNK, nh), lambda zd, bb, p: (zd, rmap(zd, bb, p), 0))
    row_spec = pl.BlockSpec((1, 1, nh, CHUNK), lambda zd, bb, p: (zd, rmap(zd, bb, p), 0, 0))
    return pl.pallas_call(
        functools.partial(_mlstm_body, nh=nh, dk=dk, dv=dv),
        grid=(2, nb, nchunks),
        in_specs=[pl.BlockSpec((CHUNK, dkt), lambda zd, bb, p: (rmap(zd, bb, p), 0)),
                  pl.BlockSpec((CHUNK, dkt), lambda zd, bb, p: (rmap(zd, bb, p), 1)),
                  pl.BlockSpec((CHUNK, dvt), lambda zd, bb, p: (rmap(zd, bb, p), v_col0 // dvt)),
                  col_spec, col_spec, row_spec, row_spec],
        out_specs=pl.BlockSpec((1, CHUNK, dvt), lambda zd, bb, p: (zd, rmap(zd, bb, p), 0)),
        out_shape=jax.ShapeDtypeStruct((2, m, dvt), F32),
        scratch_shapes=[pltpu.VMEM((nh, dk, dv), F32), pltpu.VMEM((nh, 1, dk), F32),
                        pltpu.VMEM((nh, 1, LANES), F32)],
        compiler_params=_cparams(("parallel", "parallel", "arbitrary")),
        name="mlstm_scan",
    )(qk, qk, u, ig, lf, ig_row, lf_row)


class _Dims:
    def __init__(self, nb, n_ctx, n_lat, d):
        self.nb, self.n_ctx, self.n_lat, self.d = nb, n_ctx, n_lat, d
        self.mc, self.mx = nb * n_ctx, nb * n_lat
        self.m = self.mc + self.mx
        self.s_len = n_ctx + n_lat
        self.tm = 1024 if (self.mc % 1024 == 0 and n_lat % 1024 == 0) else n_ctx
        self.tm_half = max(self.tm // 2, n_ctx) if self.tm > n_ctx else self.tm
        assert self.mc % self.tm == 0 and n_lat % self.tm == 0 and n_ctx % GRID_W == 0

    def mod_index(self, i, tm):
        nbc = self.mc // tm
        bpb = self.n_lat // tm
        return jnp.where(i < nbc, self.nb, (i - nbc) // bpb)

    def time_major_block(self, i):
        nbc = self.mc // self.n_ctx
        bpb = self.n_lat // self.n_ctx
        return (jnp.where(i < nbc, 0, 1 + (i - nbc) % bpb), jnp.where(i < nbc, i, (i - nbc) // bpb))

    def heads_view(self, a, ndir=1):
        return a.reshape(self.s_len, self.nb, -1, 2 * RWKV_HEAD)

    def chan_to_scan(self, p):
        lead = p.shape[:-1]
        hpn = p.shape[-1] // (2 * RWKV_HEAD)
        pt = jnp.moveaxis(p.reshape(lead + (hpn, 2, RWKV_HEAD)), -3, -1)
        pt = jnp.broadcast_to(pt[..., None, :], lead + (2, RWKV_HEAD, LANES // hpn, hpn))
        return pt.reshape(lead + (2, RWKV_HEAD, LANES))


def _pad_to(a, axis, mult):
    pad = -a.shape[axis] % mult
    if pad == 0:
        return a
    widths = [(0, 0)] * a.ndim
    widths[axis] = (0, pad)
    return jnp.pad(a, widths)


def _rwkv_layer(dm, xs, g1, sc1, sh1, p, v_first):
    m, d, tm = dm.m, dm.d, dm.tm
    xr, xw, xk, xv, xa, xg = [a.reshape(m, d) for a in _rwkv_mix(xs, g1, sc1, sh1, p['mu'], dm=dm)]
    mm = functools.partial(_matmul, m=m, tm=tm)
    r = mm(xr, p['w_r'], kdim=d, tn=512, name="mm_r")
    k = mm(xk, p['w_k'], kdim=d, tn=512, name="mm_k")
    v = mm(xv, p['w_v'], kdim=d, tn=512, name="mm_v")
    row_spec = pl.BlockSpec((tm, 512), lambda i, j, k: (i, j))
    vec_spec = pl.BlockSpec((1, 512), lambda i, j, k: (0, j))
    if v_first is not None:
        lv = mm(xv, p['v1'], kdim=d, tn=LANES, name="mm_v1")
        v = mm(lv, p['v2'], kdim=LANES, tn=512, name="mm_vres",
               epi=(_epi_vres, [v, v_first, p['v0']], [row_spec, row_spec, vec_spec]))
    lw = mm(xw, p['w1'], kdim=d, tn=2 * LANES, out_dtype=BF16, name="mm_w1", epi=(jnp.tanh, [], []))
    la = mm(xa, p['a1'], kdim=d, tn=2 * LANES, out_dtype=BF16, name="mm_a1")
    lg = mm(xg, p['g1'], kdim=d, tn=p['g1'].shape[1], out_dtype=BF16, name="mm_g1",
            epi=(jax.nn.sigmoid, [], []))
    g = mm(lg, p['g2'], kdim=p['g2'].shape[0], tn=512, name="mm_g2")
    wp = mm(lw, p['w2'], kdim=2 * LANES, tn=512, name="mm_w2")
    ap = mm(la, p['a2'], kdim=2 * LANES, tn=512, name="mm_a2")
    cs, hv = dm.chan_to_scan, dm.heads_view
    y, rk = _rwkv_scan(hv(r), hv(k), hv(v), hv(wp), hv(ap),
                       cs(p['k_k']), cs(p['k_a']), cs(p['r_k']), cs(p['w0']), cs(p['a0']),
                       n_ctx=dm.n_ctx, tblk=32)
    pre = _rwkv_norm(y, hv(v), rk, cs(p['ln_w']), cs(p['ln_b']), tblk=32, eps=RWKV_HEAD * 1e-5)
    return pre.reshape(m, d), g, v


def _mlstm_layer(dm, xs, g1, sc1, sh1, p):
    m, d, tm = dm.m, dm.d, dm.tm
    nh = MLSTM_HEADS
    mod_spec = pl.BlockSpec((1, 1, d), lambda i, j, k: (dm.mod_index(i, tm), 0, 0))
    u = _matmul(None, p['w_in'], m=m, kdim=d, tm=tm, tn=896, name="mm_mlstm_in",
                pro=(_pro_normmod, [xs, g1, sc1, sh1],
                     [pl.BlockSpec((tm, d), lambda i, j, k: (i, 0)),
                      pl.BlockSpec((1, d), lambda i, j, k: (0, 0)), mod_spec, mod_spec]))
    qk = _mlstm_conv(u, p['conv_w'], p['conv_b'], dm=dm, width=d, tc=min(512, d // 2),
                     q_scale=float(d // 2 // nh) ** -0.5)
    gates = u[:, 3 * d:3 * d + 4 * nh].reshape(m, 2, 2, nh) + p['b_gate']
    gates = GATE_CAP * jnp.tanh(gates / GATE_CAP)
    ig = jnp.moveaxis(gates[:, :, 0, :], 1, 0)
    lf = jnp.moveaxis(jax.nn.log_sigmoid(gates[:, :, 1, :]), 1, 0)
    hs = _mlstm_scan(qk, u, d, ig, lf, nb=dm.nb, n_ctx=dm.n_ctx, n_lat=dm.n_lat)
    return hs, u


def kernel(x, c, ctx, c_ctx, mod_w, mod_b, norm_g, final_g, rwkv_mu, rwkv_w_r, rwkv_w_k, rwkv_w_v, rwkv_w_o, rwkv_w0, rwkv_w1, rwkv_w2, rwkv_a0, rwkv_a1, rwkv_a2, rwkv_g1, rwkv_g2, rwkv_k_k, rwkv_k_a, rwkv_r_k, rwkv_ln_w, rwkv_ln_b, rwkv_v0, rwkv_v1, rwkv_v2, mlstm_w_in, mlstm_b_gate, mlstm_conv_w, mlstm_conv_b, mlstm_norm_w, mlstm_w_out, ffn_w_in, ffn_w_out):
    nb, n_lat, d = x.shape
    n_ctx = ctx.shape[1]
    depth = mod_w.shape[0]
    d_ff = ffn_w_out.shape[1]
    dm = _Dims(nb, n_ctx, n_lat, d)
    m, tm, tmh = dm.m, dm.tm, dm.tm_half
    bf = lambda a: a.astype(BF16)

    cond = jax.nn.silu(jnp.concatenate([c, c_ctx[None, :]], axis=0))
    rows = cond.shape[0]
    cond = _pad_to(cond, 0, 2 * SUBLANES)
    mod_w2 = mod_w.reshape(depth * d, 6 * d)
    mods = []
    for i in range(depth):
        mo = _matmul(cond, mod_w2, m=cond.shape[0], kdim=d, tm=cond.shape[0], tn=512,
                     w_row0=i * d, name="mm_mod") + mod_b[i]
        mods.append(mo[:rows].reshape(rows, 6, 1, d))

    def gate_res(res, gate, tm_, tn_=512):
        return (_epi_gate_res, [res, gate],
                [pl.BlockSpec((tm_, tn_), lambda i, j, k: (i, j)),
                 pl.BlockSpec((1, 1, tn_), lambda i, j, k: (dm.mod_index(i, tm_), 0, j))])

    xs = jnp.concatenate([ctx.reshape(dm.mc, d), x.reshape(dm.mx, d)], axis=0)
    v_first = None
    for i in range(depth):
        j = i // 2
        sh1, sc1, gt1, sh2, sc2, gt2 = [mods[i][:, n] for n in range(6)]
        g1 = norm_g[i, 0][None, :]
        g2 = norm_g[i, 1][None, :]
        full_k = lambda tm_: pl.BlockSpec((tm_, d), lambda i, j, k: (i, 0))
        if i % 2 == 0:
            lora = lambda a: bf(jnp.concatenate([_pad_to(a[0], 1, LANES), _pad_to(a[1], 1, LANES)], axis=1))

            def lora_up(a):
                ap_ = _pad_to(a, 1, LANES)
                zero = jnp.zeros_like(ap_[0])
                return bf(jnp.concatenate([jnp.concatenate([ap_[0], zero], axis=1),
                                           jnp.concatenate([zero, ap_[1]], axis=1)], axis=0))

            p = {'mu': rwkv_mu[j], 'w_r': bf(rwkv_w_r[j]), 'w_k': bf(rwkv_w_k[j]), 'w_v': bf(rwkv_w_v[j]),
                 'w0': rwkv_w0[j], 'w1': lora(rwkv_w1[j]), 'w2': lora_up(rwkv_w2[j]),
                 'a0': rwkv_a0[j], 'a1': lora(rwkv_a1[j]), 'a2': lora_up(rwkv_a2[j]),
                 'g1': bf(rwkv_g1[j]), 'g2': bf(rwkv_g2[j]),
                 'k_k': rwkv_k_k[j], 'k_a': rwkv_k_a[j], 'r_k': rwkv_r_k[j].reshape(-1),
                 'ln_w': rwkv_ln_w[j], 'ln_b': rwkv_ln_b[j]}
            if j > 0:
                p['v0'] = rwkv_v0[j - 1][None, :]
                p['v1'] = bf(_pad_to(rwkv_v1[j - 1], 1, LANES))
                p['v2'] = bf(_pad_to(rwkv_v2[j - 1], 0, LANES))
            pre, g, v_cur = _rwkv_layer(dm, xs, g1, sc1, sh1, p, v_first if j > 0 else None)
            if j == 0:
                v_first = v_cur
            tmo = n_ctx
            tm_spec = pl.BlockSpec((tmo, d), lambda i, j, k: dm.time_major_block(i))
            xs = _matmul(None, bf(rwkv_w_o[j]), m=m, kdim=d, tm=tmo, tn=d, name="mm_rwkv_out",
                         pro=(_pro_mulg, [pre.reshape(dm.s_len, nb * d), g.reshape(dm.s_len, nb * d)],
                              [tm_spec, tm_spec]),
                         epi=gate_res(xs, gt1, tmo, d))
        else:
            p = {'w_in': bf(_pad_to(mlstm_w_in[j], 1, 896)),
                 'b_gate': mlstm_b_gate[j], 'conv_w': mlstm_conv_w[j], 'conv_b': mlstm_conv_b[j]}
            hs, u = _mlstm_layer(dm, xs, g1, sc1, sh1, p)
            xs = _matmul(None, bf(mlstm_w_out[j]), m=m, kdim=d, tm=tmh, tn=512, name="mm_mlstm_out",
                         pro=(_pro_mlstm_read, [hs, u, mlstm_norm_w[j][None, :]],
                              [pl.BlockSpec((2, tmh, d), lambda i, j, k: (0, i, 0)),
                               pl.BlockSpec((tmh, d), lambda i, j, k: (i, 2)),
                               pl.BlockSpec((1, d), lambda i, j, k: (0, 0))]),
                         epi=gate_res(xs, gt1, tmh))
        mod_spec = pl.BlockSpec((1, 1, d), lambda i, j, k: (dm.mod_index(i, tm), 0, 0))
        hid = _matmul(None, bf(ffn_w_in[i]), m=m, kdim=d, tm=tm, tn=512, out_dtype=BF16,
                      n_out=d_ff, w2_col0=d_ff, name="mm_ffn_in",
                      pro=(_pro_normmod, [xs, g2, sc2, sh2],
                           [full_k(tm), pl.BlockSpec((1, d), lambda i, j, k: (0, 0)), mod_spec, mod_spec]),
                      epi=(_epi_swiglu, [], []))
        xs = _matmul(hid, bf(ffn_w_out[i]), m=m, kdim=d_ff, tm=tm, tn=512, tk=d_ff // 2,
                     name="mm_ffn_out", epi=gate_res(xs, gt2, tm))
    xl = xs[dm.mc:].reshape(nb, n_lat, d)
    return xl * lax.rsqrt(jnp.mean(xl * xl, axis=-1, keepdims=True) + NORM_EPS) * final_g
```

```python
import functools

import jax
import jax.numpy as jnp
from jax import lax
from jax.experimental import pallas as pl
from jax.experimental.pallas import tpu as pltpu

F32 = jnp.float32
BF16 = jnp.bfloat16

GRID_W = 64
NORM_EPS = 1e-6
RWKV_HEAD = 64
MLSTM_HEADS = 8
CHUNK = 64
GATE_CAP = 15.0
LANES = 128
SUBLANES = 8
VMEM_LIMIT = 56 * 1024 * 1024


def _cparams(sem):
    return pltpu.CompilerParams(dimension_semantics=sem, vmem_limit_bytes=VMEM_LIMIT)


def _rms_mod(x, g, sc, sh):
    ms = jnp.mean(x * x, axis=-1, keepdims=True)
    return x * lax.rsqrt(ms + NORM_EPS) * g * (1.0 + sc) + sh


def _head_ln_lanes(y, nheads, eps):
    hd = y.shape[-1] // nheads
    out = []
    for h in range(nheads):
        seg = y[:, h * hd:(h + 1) * hd]
        mu = jnp.mean(seg, axis=-1, keepdims=True)
        var = jnp.mean(jnp.square(seg - mu), axis=-1, keepdims=True)
        out.append((seg - mu) * lax.rsqrt(var + eps))
    return jnp.concatenate(out, axis=-1)


def _pro_normmod(x_ref, g_ref, sc_ref, sh_ref):
    return _rms_mod(x_ref[...].astype(F32), g_ref[...], sc_ref[0], sh_ref[0]).astype(BF16)


def _pro_mulg(x_ref, g_ref):
    return (x_ref[...] * g_ref[...]).astype(BF16)


def _pro_mlstm_read(h_ref, o_ref, nw_ref):
    hn = _head_ln_lanes(h_ref[0] + h_ref[1], MLSTM_HEADS, NORM_EPS) * nw_ref[...]
    return (hn * jax.nn.sigmoid(o_ref[...])).astype(BF16)


def _epi_swiglu(acc, acc2):
    return acc * jax.nn.sigmoid(acc) * acc2


def _epi_gate_res(acc, res_ref, gate_ref):
    return res_ref[...] + gate_ref[0] * acc


def _epi_vres(acc, v_ref, vf_ref, v0_ref):
    v = v_ref[...]
    return v + (vf_ref[...] - v) * jax.nn.sigmoid(v0_ref[...] + acc)


def _mm_body(*refs, nk, n_pro, pro_fn, n_epi, epi_fn, dual):
    it = iter(refs)
    pro_refs = [next(it) for _ in range(n_pro)]
    w_ref = next(it)
    w2_ref = next(it) if dual else None
    epi_refs = [next(it) for _ in range(n_epi)]
    o_ref = next(it)
    acc_ref = next(it)
    acc2_ref = next(it) if dual else None
    xb_ref = next(it) if pro_fn is not None else None

    j = pl.program_id(1)
    k = pl.program_id(2)

    if pro_fn is not None:
        @pl.when(j == 0)
        def _():
            xb_ref[...] = pro_fn(*pro_refs)
        xb = xb_ref[...]
    else:
        xb = pro_refs[0][...].astype(BF16)

    @pl.when(k == 0)
    def _():
        acc_ref[...] = jnp.zeros_like(acc_ref)
        if dual:
            acc2_ref[...] = jnp.zeros_like(acc2_ref)

    acc_ref[...] += jnp.dot(xb, w_ref[...].astype(BF16), preferred_element_type=F32)
    if dual:
        acc2_ref[...] += jnp.dot(xb, w2_ref[...].astype(BF16), preferred_element_type=F32)

    @pl.when(k == nk - 1)
    def _():
        acc = acc_ref[...]
        if dual:
            acc = epi_fn(acc, acc2_ref[...], *epi_refs)
        elif epi_fn is not None:
            acc = epi_fn(acc, *epi_refs)
        o_ref[...] = acc.astype(o_ref.dtype)


def _matmul(x, w, *, m, kdim, tm, tn, tk=None, out_dtype=F32, n_out=None, w_col0=0, w2_col0=None,
            w_row0=0, x_col0=0, row_block0=0, pro=None, epi=None, name="mm"):
    n_out = w.shape[1] if n_out is None else n_out
    tk = kdim if tk is None else tk
    dual = w2_col0 is not None
    assert m % tm == 0 and n_out % tn == 0 and kdim % tk == 0
    assert w_col0 % tn == 0 and w_row0 % tk == 0 and x_col0 % tk == 0
    nk = kdim // tk
    c0, r0, xc0 = w_col0 // tn, w_row0 // tk, x_col0 // tk
    pro_fn, pro_args, pro_specs = pro if pro is not None else (None, [x], [
        pl.BlockSpec((tm, tk), lambda i, j, k: (i, k + xc0))])
    if pro_fn is not None:
        assert nk == 1
    epi_fn, epi_args, epi_specs = epi if epi is not None else (None, [], [])
    in_specs = list(pro_specs) + [pl.BlockSpec((tk, tn), lambda i, j, k: (k + r0, j + c0))]
    args = list(pro_args) + [w]
    if dual:
        assert w2_col0 % tn == 0
        c2 = w2_col0 // tn
        in_specs.append(pl.BlockSpec((tk, tn), lambda i, j, k: (k + r0, j + c2)))
        args.append(w)
    in_specs += list(epi_specs)
    args += list(epi_args)
    scratch = [pltpu.VMEM((tm, tn), F32)]
    if dual:
        scratch.append(pltpu.VMEM((tm, tn), F32))
    if pro_fn is not None:
        scratch.append(pltpu.VMEM((tm, tk), BF16))
    out_spec = pl.BlockSpec((tm, tn), lambda i, j, k: (i, j))
    if row_block0:
        shift = lambda sp: pl.BlockSpec(sp.block_shape,
                                        lambda i, j, k: sp.index_map(i + row_block0, j, k))
        in_specs = [shift(sp) for sp in in_specs]
        out_spec = shift(out_spec)
    return pl.pallas_call(
        functools.partial(_mm_body, nk=nk, n_pro=len(pro_args), pro_fn=pro_fn,
                          n_epi=len(epi_args), epi_fn=epi_fn, dual=dual),
        grid=(m // tm - row_block0, n_out // tn, nk),
        in_specs=in_specs,
        out_specs=out_spec,
        out_shape=jax.ShapeDtypeStruct((m, n_out), out_dtype),
        scratch_shapes=scratch,
        compiler_params=_cparams(("parallel", "arbitrary", "arbitrary")),
        name=name,
    )(*args)


def _halo_specs(tm, width, col_map, m):
    per = tm // GRID_W
    last = m // GRID_W - 1
    return [pl.BlockSpec((tm, width), lambda i, *j: (i, col_map(*j))),
            pl.BlockSpec((GRID_W, width), lambda i, *j: (jnp.maximum(i * per - 1, 0), col_map(*j))),
            pl.BlockSpec((GRID_W, width), lambda i, *j: (jnp.minimum((i + 1) * per, last), col_map(*j)))]


def _mix_body(x_ref, up_ref, dn_ref, g_ref, sc_ref, sh_ref, mu_ref, *out_refs, tm, nbc, bpb, d):
    i = pl.program_id(0)
    g, sc, sh = g_ref[...], sc_ref[0], sh_ref[0]
    h = _rms_mod(x_ref[...].astype(F32), g, sc, sh)
    row = lax.broadcasted_iota(jnp.int32, (tm, 1), 0)

    def emit(lo, hi, shifted):
        hseg = h[:, lo:hi]
        xx = shifted - hseg
        for n, o_ref in enumerate(out_refs):
            o_ref[:, lo:hi] = (hseg + xx * mu_ref[n:n + 1, lo:hi]).astype(o_ref.dtype)

    @pl.when(i < nbc)
    def _():
        half = d // 2
        emit(0, half, jnp.where(row == 0, 0.0, pltpu.roll(h[:, :half], 1, axis=0)))
        emit(half, d, jnp.where(row == tm - 1, 0.0, pltpu.roll(h[:, half:], tm - 1, axis=0)))

    @pl.when(i >= nbc)
    def _():
        jb = (i - nbc) % bpb
        q = d // 4
        col = row % GRID_W
        emit(0, q, jnp.where(col == 0, 0.0, pltpu.roll(h[:, :q], 1, axis=0)))
        emit(q, 2 * q, jnp.where(col == GRID_W - 1, 0.0, pltpu.roll(h[:, q:2 * q], tm - 1, axis=0)))
        hu = _rms_mod(up_ref[...].astype(F32), g, sc, sh)[:, 2 * q:3 * q]
        hu = hu * jnp.where(jb > 0, 1.0, 0.0)
        emit(2 * q, 3 * q, jnp.concatenate([hu, h[:tm - GRID_W, 2 * q:3 * q]], axis=0))
        hd = _rms_mod(dn_ref[...].astype(F32), g, sc, sh)[:, 3 * q:]
        hd = hd * jnp.where(jb < bpb - 1, 1.0, 0.0)
        emit(3 * q, d, jnp.concatenate([h[GRID_W:, 3 * q:], hd], axis=0))


def _rwkv_mix(xs, g, sc, sh, mu, *, dm):
    m, d = xs.shape
    tm = dm.n_ctx
    nbc, bpb = dm.mc // tm, dm.n_lat // tm
    mod_spec = pl.BlockSpec((1, 1, d), lambda i: (dm.mod_index(i, tm), 0, 0))
    tm_spec = pl.BlockSpec((tm, d), lambda i: dm.time_major_block(i))
    bm_spec = pl.BlockSpec((None, tm, d), lambda i: dm.time_major_block(i)[::-1] + (0,))
    return pl.pallas_call(
        functools.partial(_mix_body, tm=tm, nbc=nbc, bpb=bpb, d=d),
        grid=(m // tm,),
        in_specs=_halo_specs(tm, d, lambda: 0, m) + [
            pl.BlockSpec((1, d), lambda i: (0, 0)), mod_spec, mod_spec,
            pl.BlockSpec(mu.shape, lambda i: (0, 0))],
        out_specs=[tm_spec] * 5 + [bm_spec],
        out_shape=[jax.ShapeDtypeStruct((dm.s_len, dm.nb * d), BF16)] * 5
        + [jax.ShapeDtypeStruct((dm.nb, dm.s_len, d), BF16)],
        compiler_params=_cparams(("parallel",)),
        name="rwkv_mix",
    )(xs, xs, xs, g, sc, sh, mu)


def _conv_body(x_ref, up_ref, dn_ref, w_ref, b_ref, o_ref, *, tm, nbc, bpb, n_qcols, q_scale):
    i = pl.program_id(0)
    j = pl.program_id(1)
    row = lax.broadcasted_iota(jnp.int32, (tm, 1), 0)
    x = x_ref[...]
    scale = jnp.where(j < n_qcols, q_scale, 1.0)

    def finish(y):
        y = y + b_ref[...]
        o_ref[...] = (y * jax.nn.sigmoid(y) * scale).astype(o_ref.dtype)

    @pl.when(i < nbc)
    def _():
        prev = jnp.where(row == 0, 0.0, pltpu.roll(x, 1, axis=0))
        nxt = jnp.where(row == tm - 1, 0.0, pltpu.roll(x, tm - 1, axis=0))
        finish(prev * w_ref[3:4, :] + x * w_ref[4:5, :] + nxt * w_ref[5:6, :])

    @pl.when(i >= nbc)
    def _():
        jb = (i - nbc) % bpb
        te = tm + 2 * GRID_W
        ext = jnp.concatenate([up_ref[...] * jnp.where(jb > 0, 1.0, 0.0), x,
                               dn_ref[...] * jnp.where(jb < bpb - 1, 1.0, 0.0)], axis=0)
        col = lax.broadcasted_iota(jnp.int32, (te, 1), 0) % GRID_W
        taps = (jnp.where(col == 0, 0.0, pltpu.roll(ext, 1, axis=0)), ext,
                jnp.where(col == GRID_W - 1, 0.0, pltpu.roll(ext, te - 1, axis=0)))
        y = jnp.zeros((tm, x.shape[1]), F32)
        for di in range(3):
            for dj in range(3):
                y = y + taps[dj][di * GRID_W:di * GRID_W + tm] * w_ref[3 * di + dj:3 * di + dj + 1, :]
        finish(y)


def _mlstm_conv(u, conv_w, conv_b, *, dm, width, tc, q_scale):
    m = u.shape[0]
    tm = dm.n_ctx
    nbc, bpb = dm.mc // tm, dm.n_lat // tm
    w9 = conv_w.reshape(9, width)
    return pl.pallas_call(
        functools.partial(_conv_body, tm=tm, nbc=nbc, bpb=bpb, n_qcols=width // 2 // tc,
                          q_scale=q_scale),
        grid=(m // tm, width // tc),
        in_specs=_halo_specs(tm, tc, lambda j: j, m) + [
            pl.BlockSpec((9, tc), lambda i, j: (0, j)), pl.BlockSpec((1, tc), lambda i, j: (0, j))],
        out_specs=pl.BlockSpec((tm, tc), lambda i, j: (i, j)),
        out_shape=jax.ShapeDtypeStruct((m, width), F32),
        compiler_params=_cparams(("parallel", "parallel")),
        name="mlstm_conv",
    )(u, u, u, w9, conv_b[None, :])


def _split_parity(ref, t, n):
    x = ref[t]
    hpn = x.shape[1] // (2 * n)
    rows = jnp.concatenate([x[:, hp * 2 * n:(hp + 1) * 2 * n] for hp in range(hpn)], axis=0)
    xt = rows.T
    return xt[:n], xt[n:]


def _rwkv_scan_body(r_ref, k_ref, v_ref, wp_ref, ap_ref, kk_ref, ka_ref, rkp_ref, w0_ref, a0_ref,
                    y_ref, rko_ref, s_ref, sz_ref, r_s, v_s, w_s, k_s, z_s, b_s, *, tblk, n):
    zdir = pl.program_id(0)
    tb = pl.program_id(2)

    @pl.when(tb == 0)
    def _():
        s_ref[...] = jnp.zeros_like(s_ref)

    def prep(t, carry):
        r2, k2, v2 = _split_parity(r_ref, t, n), _split_parity(k_ref, t, n), _split_parity(v_ref, t, n)
        wp2, ap2 = _split_parity(wp_ref, t, n), _split_parity(ap_ref, t, n)
        for g in range(2):
            k_t = k2[g]
            a = jax.nn.sigmoid(a0_ref[g] + ap2[g])
            w_s[t, g] = jnp.exp(-jnp.exp(-0.5) * jax.nn.sigmoid(w0_ref[g] + wp2[g]))
            kr = k_t * kk_ref[g]
            nrm = jnp.sqrt(jnp.sum(kr * kr, axis=0, keepdims=True))
            kkn = kr / jnp.maximum(nrm, 1e-12)
            z_s[t, g] = -kkn
            b_s[t, g] = kkn * a
            km = k_t * (1.0 + (a - 1.0) * ka_ref[g])
            k_s[t, g] = km
            r_s[t, g] = r2[g]
            v_s[t, g] = v2[g]
            rko_ref[t, g] = jnp.sum(r2[g] * km * rkp_ref[g], axis=0, keepdims=True)
        return carry

    lax.fori_loop(0, tblk, prep, 0, unroll=4)

    bwd = zdir == 1
    t_first = jnp.where(bwd, tblk - 1, 0)
    t_step = jnp.where(bwd, -1, 1)

    def row(ref, t, g, kk):
        return jnp.broadcast_to(ref[t, g, pl.ds(kk, 1), :], (n, LANES))

    for g in range(2):
        acc = jnp.zeros((n, LANES), F32)
        for kk in range(n):
            acc = acc + s_ref[g, kk] * row(z_s, t_first, g, kk)
        sz_ref[g] = acc

    def step(i, carry):
        t = t_first + i * t_step
        t_next = jnp.clip(t + t_step, 0, tblk - 1)
        for g in range(2):
            sz = sz_ref[g]
            v_t = v_s[t, g]
            y = jnp.zeros((n, LANES), F32)
            sz_next = jnp.zeros((n, LANES), F32)
            for kk in range(n):
                s_new = (s_ref[g, kk] * row(w_s, t, g, kk) + sz * row(b_s, t, g, kk)
                         + v_t * row(k_s, t, g, kk))
                s_ref[g, kk] = s_new
                y = y + s_new * row(r_s, t, g, kk)
                sz_next = sz_next + s_new * row(z_s, t_next, g, kk)
            y_ref[t, g] = y
            sz_ref[g] = sz_next
        return carry

    lax.fori_loop(0, tblk, step, 0)


def _rwkv_scan(r, k, v, wp, ap, kk_p, ka_p, rk_p, w0_p, a0_p, *, n_ctx, tblk):
    s_len, nbat, d = r.shape
    n = RWKV_HEAD
    nbg = LANES // (d // (2 * n))
    assert 2 * n == LANES and nbat % nbg == 0 and n_ctx % tblk == 0 and s_len % tblk == 0
    ngrp = nbat // nbg
    nbc, nb = n_ctx // tblk, s_len // tblk

    def tmap(zd, p):
        rev = jnp.where(p < nbc, nbc - 1 - p, nb - 1 - (p - nbc))
        return jnp.where(zd == 0, p, rev)

    shared = pl.BlockSpec((tblk, nbg, d), lambda zd, g, p: (tmap(zd, p), g, 0))
    perdir = pl.BlockSpec((tblk, nbg, d), lambda zd, g, p: (tmap(zd, p), g, zd))
    par = pl.BlockSpec((2, n, LANES), lambda zd, g, p: (0, 0, 0))
    par_dir = pl.BlockSpec((None, 2, n, LANES), lambda zd, g, p: (zd, 0, 0, 0))
    step_buf = pltpu.VMEM((tblk, 2, n, LANES), F32)
    return pl.pallas_call(
        functools.partial(_rwkv_scan_body, tblk=tblk, n=n),
        grid=(2, ngrp, nb),
        in_specs=[shared, shared, shared, perdir, perdir, par, par, par, par_dir, par_dir],
        out_specs=[pl.BlockSpec((None, tblk, 2, n, LANES), lambda zd, g, p: (zd, tmap(zd, p), g, 0, 0)),
                   pl.BlockSpec((None, tblk, 2, 1, LANES), lambda zd, g, p: (zd, tmap(zd, p), g, 0, 0))],
        out_shape=[jax.ShapeDtypeStruct((2, s_len, 2 * ngrp, n, LANES), F32),
                   jax.ShapeDtypeStruct((2, s_len, 2 * ngrp, 1, LANES), F32)],
        scratch_shapes=[pltpu.VMEM((2, n, n, LANES), F32), pltpu.VMEM((2, n, LANES), F32)]
        + [step_buf] * 6,
        compiler_params=_cparams(("parallel", "parallel", "arbitrary")),
        name="rwkv_scan",
    )(r, k, v, wp, ap, kk_p, ka_p, rk_p, w0_p, a0_p)


def _rwkv_norm_body(y_ref, v_ref, rk_ref, lw_ref, lb_ref, o_ref, *, eps, tblk, n):
    def step(t, carry):
        v2 = _split_parity(v_ref, t, n)
        outs = []
        for g in range(2):
            ys = y_ref[0, t, g] + y_ref[1, t, g]
            mu = jnp.mean(ys, axis=0, keepdims=True)
            var = jnp.mean(jnp.square(ys - mu), axis=0, keepdims=True)
            yn = (ys - mu) * lax.rsqrt(var + eps) * lw_ref[g] + lb_ref[g]
            outs.append(yn + (rk_ref[0, t, g] + rk_ref[1, t, g]) * v2[g])
        rows = jnp.concatenate(outs, axis=0).T
        nbg = o_ref.shape[0]
        o_ref[:, t, :] = jnp.concatenate([rows[hp * nbg:(hp + 1) * nbg] for hp in range(LANES // nbg)],
                                         axis=1)
        return carry

    lax.fori_loop(0, tblk, step, 0, unroll=4)


def _rwkv_norm(y, v, rk, ln_w_p, ln_b_p, *, tblk, eps):
    s_len, nbat, d = v.shape
    n = RWKV_HEAD
    nbg = LANES // (d // (2 * n))
    par = pl.BlockSpec((2, n, LANES), lambda p, g: (0, 0, 0))
    nat = pl.BlockSpec((tblk, nbg, d), lambda p, g: (p, g, 0))
    return pl.pallas_call(
        functools.partial(_rwkv_norm_body, eps=eps, tblk=tblk, n=n),
        grid=(s_len // tblk, nbat // nbg),
        in_specs=[pl.BlockSpec((2, tblk, 2, n, LANES), lambda p, g: (0, p, g, 0, 0)), nat,
                  pl.BlockSpec((2, tblk, 2, 1, LANES), lambda p, g: (0, p, g, 0, 0)), par, par],
        out_specs=pl.BlockSpec((nbg, tblk, d), lambda p, g: (g, p, 0)),
        out_shape=jax.ShapeDtypeStruct((nbat, s_len, d), F32),
        compiler_params=_cparams(("parallel", "parallel")),
        name="rwkv_norm",
    )(y, v, rk, ln_w_p, ln_b_p)


def _mlstm_body(q_ref, k_ref, v_ref, ic_ref, fc_ref, ir_ref, fr_ref, o_ref,
                c_ref, n_ref, m_ref, *, nh, dk, dv):
    zdir = pl.program_id(0)
    p = pl.program_id(2)

    @pl.when(p == 0)
    def _():
        c_ref[...] = jnp.zeros_like(c_ref)
        n_ref[...] = jnp.zeros_like(n_ref)
        m_ref[...] = jnp.zeros_like(m_ref)

    ti = lax.broadcasted_iota(jnp.int32, (CHUNK, CHUNK), 0)
    tj = lax.broadcasted_iota(jnp.int32, (CHUNK, CHUNK), 1)
    mask = (ti - tj) * (1 - 2 * zdir) >= 0
    tri = mask.astype(F32)
    fcol = fc_ref[0]
    icol = ic_ref[0]
    frow = fr_ref[0, 0]
    irow = ir_ref[0, 0]
    hi = lax.Precision.HIGHEST
    bcum_col = jnp.dot(tri, fcol, precision=hi, preferred_element_type=F32)
    bcum_row = lax.dot_general(frow, tri, (((1,), (1,)), ((), ())), precision=hi,
                               preferred_element_type=F32)

    hs = range(nh)
    q32 = [q_ref[:, h * dk:(h + 1) * dk].astype(F32) for h in hs]
    qb = [x.astype(BF16) for x in q32]
    k32 = [k_ref[:, h * dk:(h + 1) * dk].astype(F32) for h in hs]
    vb = [v_ref[:, h * dv:(h + 1) * dv].astype(BF16) for h in hs]
    c_st = [c_ref[h] for h in hs]
    n_st = [n_ref[h] for h in hs]
    m_st = [m_ref[h][:, :1] for h in hs]
    bc = [bcum_col[:, h:h + 1] for h in hs]
    log_d = [jnp.where(mask, bc[h] - bcum_row[h:h + 1, :] + irow[h:h + 1, :], -jnp.inf) for h in hs]
    log_inter = [bc[h] + m_st[h] for h in hs]
    m_t = [jnp.maximum(log_inter[h], jnp.max(log_d[h], axis=-1, keepdims=True)) for h in hs]
    b_end = [jnp.sum(frow[h:h + 1, :], axis=-1, keepdims=True) for h in hs]
    a_col = [b_end[h] - bc[h] + icol[:, h:h + 1] for h in hs]
    m_new = [jnp.maximum(b_end[h] + m_st[h], jnp.max(a_col[h], axis=0, keepdims=True)) for h in hs]
    qk = [lax.dot_general(qb[h], k32[h].astype(BF16), (((1,), (1,)), ((), ())),
                          preferred_element_type=F32) for h in hs]
    qc = [jnp.dot(qb[h], c_st[h].astype(BF16), preferred_element_type=F32) for h in hs]
    s = [qk[h] * jnp.exp(log_d[h] - m_t[h]) for h in hs]
    sv = [jnp.dot(s[h].astype(BF16), vb[h], preferred_element_type=F32) for h in hs]
    wk = [jnp.exp(a_col[h] - m_new[h]) * k32[h] for h in hs]
    kv = [lax.dot_general(wk[h].astype(BF16), vb[h], (((0,), (0,)), ((), ())),
                          preferred_element_type=F32) for h in hs]
    for h in hs:
        w_inter = jnp.exp(log_inter[h] - m_t[h])
        qn = jnp.sum(q32[h] * n_st[h], axis=-1, keepdims=True)
        den = w_inter * qn + jnp.sum(s[h], axis=-1, keepdims=True)
        num = w_inter * qc[h] + sv[h]
        o_ref[0, :, h * dv:(h + 1) * dv] = num / jnp.maximum(jnp.abs(den), jnp.exp(-m_t[h]))
        dec = jnp.exp(b_end[h] + m_st[h] - m_new[h])
        c_ref[h] = dec * c_st[h] + kv[h]
        n_ref[h] = dec * n_st[h] + jnp.sum(wk[h], axis=0, keepdims=True)
        m_ref[h] = jnp.broadcast_to(m_new[h], (1, LANES))


def _mlstm_scan(qk, u, v_col0, ig, lf, *, nb, n_ctx, n_lat):
    m = qk.shape[0]
    nh = MLSTM_HEADS
    dkt = qk.shape[1] // 2
    dk = dkt // nh
    dvt = 2 * dkt
    dv = dvt // nh
    ncc, ncl = n_ctx // CHUNK, n_lat // CHUNK
    nchunks = ncc + ncl
    ctx_blocks = nb * ncc
    assert v_col0 % dvt == 0

    def rmap(zd, bb, p):
        rev = jnp.where(p < ncc, ncc - 1 - p, nchunks - 1 - (p - ncc))
        ch = jnp.where(zd == 0, p, rev)
        return jnp.where(ch < ncc, bb * ncc + ch, ctx_blocks + bb * ncl + (ch - ncc))

    ig_row = jnp.swapaxes(ig.reshape(2, m // CHUNK, CHUNK, nh), 2, 3)
    lf_row = jnp.swapaxes(lf.reshape(2, m // CHUNK, CHUNK, nh), 2, 3)
    col_spec = pl.BlockSpec((1, CHUNK, nh), lambda zd, bb, p: (zd, rmap(zd, bb, p), 0))
    row_spec = pl.BlockSpec((1, 1, nh, CHUNK), lambda zd, bb, p: (zd, rmap(zd, bb, p), 0, 0))
    return pl.pallas_call(
        functools.partial(_mlstm_body, nh=nh, dk=dk, dv=dv),
        grid=(2, nb, nchunks),
        in_specs=[pl.BlockSpec((CHUNK, dkt), lambda zd, bb, p: (rmap(zd, bb, p), 0)),
                  pl.BlockSpec((CHUNK, dkt), lambda zd, bb, p: (rmap(zd, bb, p), 1)),
                  pl.BlockSpec((CHUNK, dvt), lambda zd, bb, p: (rmap(zd, bb, p), v_col0 // dvt)),
                  col_spec, col_spec, row_spec, row_spec],
        out_specs=pl.BlockSpec((1, CHUNK, dvt), lambda zd, bb, p: (zd, rmap(zd, bb, p), 0)),
        out_shape=jax.ShapeDtypeStruct((2, m, dvt), F32),
        scratch_shapes=[pltpu.VMEM((nh, dk, dv), F32), pltpu.VMEM((nh, 1, dk), F32),
                        pltpu.VMEM((nh, 1, LANES), F32)],
        compiler_params=_cparams(("parallel", "parallel", "arbitrary")),
        name="mlstm_scan",
    )(qk, qk, u, ig, lf, ig_row, lf_row)


class _Dims:
    def __init__(self, nb, n_ctx, n_lat, d):
        self.nb, self.n_ctx, self.n_lat, self.d = nb, n_ctx, n_lat, d
        self.mc, self.mx = nb * n_ctx, nb * n_lat
        self.m = self.mc + self.mx
        self.s_len = n_ctx + n_lat
        self.tm = 1024 if (self.mc % 1024 == 0 and n_lat % 1024 == 0) else n_ctx
        self.tm_half = max(self.tm // 2, n_ctx) if self.tm > n_ctx else self.tm
        assert self.mc % self.tm == 0 and n_lat % self.tm == 0 and n_ctx % GRID_W == 0

    def mod_index(self, i, tm):
        nbc = self.mc // tm
        bpb = self.n_lat // tm
        return jnp.where(i < nbc, self.nb, (i - nbc) // bpb)

    def time_major_block(self, i):
        nbc = self.mc // self.n_ctx
        bpb = self.n_lat // self.n_ctx
        return (jnp.where(i < nbc, 0, 1 + (i - nbc) % bpb), jnp.where(i < nbc, i, (i - nbc) // bpb))

    def time_view(self, a):
        return a.reshape(self.s_len, self.nb, -1)

    def chan_to_scan(self, p):
        lead = p.shape[:-1]
        hpn = p.shape[-1] // (2 * RWKV_HEAD)
        pt = jnp.moveaxis(p.reshape(lead + (hpn, 2, RWKV_HEAD)), -3, -1)
        pt = jnp.broadcast_to(pt[..., None], lead + (2, RWKV_HEAD, hpn, LANES // hpn))
        return pt.reshape(lead + (2, RWKV_HEAD, LANES))


def _pad_to(a, axis, mult):
    pad = -a.shape[axis] % mult
    if pad == 0:
        return a
    widths = [(0, 0)] * a.ndim
    widths[axis] = (0, pad)
    return jnp.pad(a, widths)


def _rwkv_layer(dm, xs, g1, sc1, sh1, p, v_first):
    m, d, tm = dm.m, dm.d, dm.tm
    xr, xw, xk, xv, xa, xg = [a.reshape(m, d) for a in _rwkv_mix(xs, g1, sc1, sh1, p['mu'], dm=dm)]
    mm = functools.partial(_matmul, m=m, tm=tm)
    r = mm(xr, p['w_r'], kdim=d, tn=512, name="mm_r")
    k = mm(xk, p['w_k'], kdim=d, tn=512, name="mm_k")
    v = mm(xv, p['w_v'], kdim=d, tn=512, name="mm_v")
    row_spec = pl.BlockSpec((tm, 512), lambda i, j, k: (i, j))
    vec_spec = pl.BlockSpec((1, 512), lambda i, j, k: (0, j))
    if v_first is not None:
        lv = mm(xv, p['v1'], kdim=d, tn=LANES, name="mm_v1")
        v = mm(lv, p['v2'], kdim=LANES, tn=512, name="mm_vres",
               epi=(_epi_vres, [v, v_first, p['v0']], [row_spec, row_spec, vec_spec]))
    lw = mm(xw, p['w1'], kdim=d, tn=2 * LANES, out_dtype=BF16, name="mm_w1", epi=(jnp.tanh, [], []))
    la = mm(xa, p['a1'], kdim=d, tn=2 * LANES, out_dtype=BF16, name="mm_a1")
    lg = mm(xg, p['g1'], kdim=d, tn=p['g1'].shape[1], out_dtype=BF16, name="mm_g1",
            epi=(jax.nn.sigmoid, [], []))
    g = mm(lg, p['g2'], kdim=p['g2'].shape[0], tn=512, name="mm_g2")
    wp = mm(lw, p['w2'], kdim=2 * LANES, tn=512, name="mm_w2")
    ap = mm(la, p['a2'], kdim=2 * LANES, tn=512, name="mm_a2")
    cs, hv = dm.chan_to_scan, dm.time_view
    y, rk = _rwkv_scan(hv(r), hv(k), hv(v), hv(wp), hv(ap),
                       cs(p['k_k']), cs(p['k_a']), cs(p['r_k']), cs(p['w0']), cs(p['a0']),
                       n_ctx=dm.n_ctx, tblk=32)
    pre = _rwkv_norm(y, hv(v), rk, cs(p['ln_w']), cs(p['ln_b']), tblk=32, eps=RWKV_HEAD * 1e-5)
    return pre, g, v


def _mlstm_layer(dm, xs, g1, sc1, sh1, p):
    m, d, tm = dm.m, dm.d, dm.tm
    nh = MLSTM_HEADS
    mod_spec = pl.BlockSpec((1, 1, d), lambda i, j, k: (dm.mod_index(i, tm), 0, 0))
    u = _matmul(None, p['w_in'], m=m, kdim=d, tm=tm, tn=896, name="mm_mlstm_in",
                pro=(_pro_normmod, [xs, g1, sc1, sh1],
                     [pl.BlockSpec((tm, d), lambda i, j, k: (i, 0)),
                      pl.BlockSpec((1, d), lambda i, j, k: (0, 0)), mod_spec, mod_spec]))
    qk = _mlstm_conv(u, p['conv_w'], p['conv_b'], dm=dm, width=d, tc=min(512, d // 2),
                     q_scale=float(d // 2 // nh) ** -0.5)
    gates = u[:, 3 * d:3 * d + 4 * nh].reshape(m, 2, 2, nh) + p['b_gate']
    gates = GATE_CAP * jnp.tanh(gates / GATE_CAP)
    ig = jnp.moveaxis(gates[:, :, 0, :], 1, 0)
    lf = jnp.moveaxis(jax.nn.log_sigmoid(gates[:, :, 1, :]), 1, 0)
    hs = _mlstm_scan(qk, u, d, ig, lf, nb=dm.nb, n_ctx=dm.n_ctx, n_lat=dm.n_lat)
    return hs, u


def kernel(x, c, ctx, c_ctx, mod_w, mod_b, norm_g, final_g, rwkv_mu, rwkv_w_r, rwkv_w_k, rwkv_w_v, rwkv_w_o, rwkv_w0, rwkv_w1, rwkv_w2, rwkv_a0, rwkv_a1, rwkv_a2, rwkv_g1, rwkv_g2, rwkv_k_k, rwkv_k_a, rwkv_r_k, rwkv_ln_w, rwkv_ln_b, rwkv_v0, rwkv_v1, rwkv_v2, mlstm_w_in, mlstm_b_gate, mlstm_conv_w, mlstm_conv_b, mlstm_norm_w, mlstm_w_out, ffn_w_in, ffn_w_out):
    nb, n_lat, d = x.shape
    n_ctx = ctx.shape[1]
    depth = mod_w.shape[0]
    d_ff = ffn_w_out.shape[1]
    dm = _Dims(nb, n_ctx, n_lat, d)
    m, tm, tmh = dm.m, dm.tm, dm.tm_half
    bf = lambda a: a.astype(BF16)

    cond = jax.nn.silu(jnp.concatenate([c, c_ctx[None, :]], axis=0))
    rows = cond.shape[0]
    cond = _pad_to(cond, 0, 2 * SUBLANES)
    mod_w2 = mod_w.reshape(depth * d, 6 * d)
    mods = []
    for i in range(depth):
        mo = _matmul(cond, mod_w2, m=cond.shape[0], kdim=d, tm=cond.shape[0], tn=512,
                     w_row0=i * d, name="mm_mod") + mod_b[i]
        mods.append(mo[:rows].reshape(rows, 6, 1, d))

    def gate_res(res, gate, tm_, tn_=512):
        return (_epi_gate_res, [res, gate],
                [pl.BlockSpec((tm_, tn_), lambda i, j, k: (i, j)),
                 pl.BlockSpec((1, 1, tn_), lambda i, j, k: (dm.mod_index(i, tm_), 0, j))])

    xs = jnp.concatenate([ctx.reshape(dm.mc, d), x.reshape(dm.mx, d)], axis=0)
    v_first = None
    for i in range(depth):
        j = i // 2
        sh1, sc1, gt1, sh2, sc2, gt2 = [mods[i][:, n] for n in range(6)]
        g1 = norm_g[i, 0][None, :]
        g2 = norm_g[i, 1][None, :]
        full_k = lambda tm_: pl.BlockSpec((tm_, d), lambda i, j, k: (i, 0))
        skip = (lambda tm_: dm.mc // tm_) if i == depth - 1 else (lambda tm_: 0)
        if i % 2 == 0:
            lora = lambda a: bf(jnp.concatenate([_pad_to(a[0], 1, LANES), _pad_to(a[1], 1, LANES)], axis=1))

            def lora_up(a):
                ap_ = _pad_to(a, 1, LANES)
                zero = jnp.zeros_like(ap_[0])
                return bf(jnp.concatenate([jnp.concatenate([ap_[0], zero], axis=1),
                                           jnp.concatenate([zero, ap_[1]], axis=1)], axis=0))

            p = {'mu': rwkv_mu[j], 'w_r': bf(rwkv_w_r[j]), 'w_k': bf(rwkv_w_k[j]), 'w_v': bf(rwkv_w_v[j]),
                 'w0': rwkv_w0[j], 'w1': lora(rwkv_w1[j]), 'w2': lora_up(rwkv_w2[j]),
                 'a0': rwkv_a0[j], 'a1': lora(rwkv_a1[j]), 'a2': lora_up(rwkv_a2[j]),
                 'g1': bf(rwkv_g1[j]), 'g2': bf(rwkv_g2[j]),
                 'k_k': rwkv_k_k[j], 'k_a': rwkv_k_a[j], 'r_k': rwkv_r_k[j].reshape(-1),
                 'ln_w': rwkv_ln_w[j], 'ln_b': rwkv_ln_b[j]}
            if j > 0:
                p['v0'] = rwkv_v0[j - 1][None, :]
                p['v1'] = bf(_pad_to(rwkv_v1[j - 1], 1, LANES))
                p['v2'] = bf(_pad_to(rwkv_v2[j - 1], 0, LANES))
            pre, g, v_cur = _rwkv_layer(dm, xs, g1, sc1, sh1, p, v_first if j > 0 else None)
            if j == 0:
                v_first = v_cur
            tmo = n_ctx
            tm_spec = pl.BlockSpec((None, tmo, d), lambda i, j, k: dm.time_major_block(i)[::-1] + (0,))
            xs = _matmul(None, bf(rwkv_w_o[j]), m=m, kdim=d, tm=tmo, tn=d, name="mm_rwkv_out",
                         row_block0=skip(tmo),
                         pro=(_pro_mulg, [pre, g.reshape(nb, dm.s_len, d)],
                              [tm_spec, tm_spec]),
                         epi=gate_res(xs, gt1, tmo, d))
        else:
            p = {'w_in': bf(_pad_to(mlstm_w_in[j], 1, 896)),
                 'b_gate': mlstm_b_gate[j], 'conv_w': mlstm_conv_w[j], 'conv_b': mlstm_conv_b[j]}
            hs, u = _mlstm_layer(dm, xs, g1, sc1, sh1, p)
            xs = _matmul(None, bf(mlstm_w_out[j]), m=m, kdim=d, tm=tmh, tn=512, name="mm_mlstm_out",
                         row_block0=skip(tmh),
                         pro=(_pro_mlstm_read, [hs, u, mlstm_norm_w[j][None, :]],
                              [pl.BlockSpec((2, tmh, d), lambda i, j, k: (0, i, 0)),
                               pl.BlockSpec((tmh, d), lambda i, j, k: (i, 2)),
                               pl.BlockSpec((1, d), lambda i, j, k: (0, 0))]),
                         epi=gate_res(xs, gt1, tmh))
        mod_spec = pl.BlockSpec((1, 1, d), lambda i, j, k: (dm.mod_index(i, tm), 0, 0))
        hid = _matmul(None, bf(ffn_w_in[i]), m=m, kdim=d, tm=tm, tn=512, out_dtype=BF16,
                      n_out=d_ff, w2_col0=d_ff, name="mm_ffn_in", row_block0=skip(tm),
                      pro=(_pro_normmod, [xs, g2, sc2, sh2],
                           [full_k(tm), pl.BlockSpec((1, d), lambda i, j, k: (0, 0)), mod_spec, mod_spec]),
                      epi=(_epi_swiglu, [], []))
        xs = _matmul(hid, bf(ffn_w_out[i]), m=m, kdim=d_ff, tm=tm, tn=512, row_block0=skip(tm),
                     name="mm_ffn_out", epi=gate_res(xs, gt2, tm))
    xl = xs[dm.mc:].reshape(nb, n_lat, d)
    return xl * lax.rsqrt(jnp.mean(xl * xl, axis=-1, keepdims=True) + NORM_EPS) * final_g
```

```python
import functools

import jax
import jax.numpy as jnp
from jax import lax
from jax.experimental import pallas as pl
from jax.experimental.pallas import tpu as pltpu

F32 = jnp.float32
BF16 = jnp.bfloat16

GRID_W = 64
NORM_EPS = 1e-6
RWKV_HEAD = 64
MLSTM_HEADS = 8
CHUNK = 64
GATE_CAP = 15.0
LANES = 128
SUBLANES = 8
VMEM_LIMIT = 56 * 1024 * 1024


def _cparams(sem):
    return pltpu.CompilerParams(dimension_semantics=sem, vmem_limit_bytes=VMEM_LIMIT)


def _rms_mod(x, g, sc, sh):
    ms = jnp.mean(x * x, axis=-1, keepdims=True)
    return x * lax.rsqrt(ms + NORM_EPS) * g * (1.0 + sc) + sh


def _head_ln_lanes(y, nheads, eps):
    hd = y.shape[-1] // nheads
    out = []
    for h in range(nheads):
        seg = y[:, h * hd:(h + 1) * hd]
        mu = jnp.mean(seg, axis=-1, keepdims=True)
        var = jnp.mean(jnp.square(seg - mu), axis=-1, keepdims=True)
        out.append((seg - mu) * lax.rsqrt(var + eps))
    return jnp.concatenate(out, axis=-1)


def _pro_normmod(x_ref, g_ref, sc_ref, sh_ref):
    return _rms_mod(x_ref[...].astype(F32), g_ref[...], sc_ref[0], sh_ref[0]).astype(BF16)


def _pro_mulg(x_ref, g_ref):
    return (x_ref[...] * g_ref[...]).astype(BF16)


def _pro_mlstm_read(h_ref, o_ref, nw_ref):
    hn = _head_ln_lanes(h_ref[0] + h_ref[1], MLSTM_HEADS, NORM_EPS) * nw_ref[...]
    return (hn * jax.nn.sigmoid(o_ref[...])).astype(BF16)


def _epi_swiglu(acc, acc2):
    return acc * jax.nn.sigmoid(acc) * acc2


def _epi_gate_res(acc, res_ref, gate_ref):
    return res_ref[...] + gate_ref[0] * acc


def _epi_vres(acc, v_ref, vf_ref, v0_ref):
    v = v_ref[...]
    return v + (vf_ref[...] - v) * jax.nn.sigmoid(v0_ref[...] + acc)


def _mm_body(*refs, nk, n_pro, pro_fn, n_epi, epi_fn, dual):
    it = iter(refs)
    pro_refs = [next(it) for _ in range(n_pro)]
    w_ref = next(it)
    w2_ref = next(it) if dual else None
    epi_refs = [next(it) for _ in range(n_epi)]
    o_ref = next(it)
    acc_ref = next(it)
    acc2_ref = next(it) if dual else None
    xb_ref = next(it) if pro_fn is not None else None

    j = pl.program_id(1)
    k = pl.program_id(2)

    if pro_fn is not None:
        @pl.when(j == 0)
        def _():
            xb_ref[...] = pro_fn(*pro_refs)
        xb = xb_ref[...]
    else:
        xb = pro_refs[0][...].astype(BF16)

    @pl.when(k == 0)
    def _():
        acc_ref[...] = jnp.zeros_like(acc_ref)
        if dual:
            acc2_ref[...] = jnp.zeros_like(acc2_ref)

    acc_ref[...] += jnp.dot(xb, w_ref[...].astype(BF16), preferred_element_type=F32)
    if dual:
        acc2_ref[...] += jnp.dot(xb, w2_ref[...].astype(BF16), preferred_element_type=F32)

    @pl.when(k == nk - 1)
    def _():
        acc = acc_ref[...]
        if dual:
            acc = epi_fn(acc, acc2_ref[...], *epi_refs)
        elif epi_fn is not None:
            acc = epi_fn(acc, *epi_refs)
        o_ref[...] = acc.astype(o_ref.dtype)


def _matmul(x, w, *, m, kdim, tm, tn, tk=None, out_dtype=F32, n_out=None, w_col0=0, w2_col0=None,
            w_row0=0, x_col0=0, row_block0=0, pro=None, epi=None, name="mm"):
    n_out = w.shape[1] if n_out is None else n_out
    tk = kdim if tk is None else tk
    dual = w2_col0 is not None
    assert m % tm == 0 and n_out % tn == 0 and kdim % tk == 0
    assert w_col0 % tn == 0 and w_row0 % tk == 0 and x_col0 % tk == 0
    nk = kdim // tk
    c0, r0, xc0 = w_col0 // tn, w_row0 // tk, x_col0 // tk
    pro_fn, pro_args, pro_specs = pro if pro is not None else (None, [x], [
        pl.BlockSpec((tm, tk), lambda i, j, k: (i, k + xc0))])
    if pro_fn is not None:
        assert nk == 1
    epi_fn, epi_args, epi_specs = epi if epi is not None else (None, [], [])
    in_specs = list(pro_specs) + [pl.BlockSpec((tk, tn), lambda i, j, k: (k + r0, j + c0))]
    args = list(pro_args) + [w]
    if dual:
        assert w2_col0 % tn == 0
        c2 = w2_col0 // tn
        in_specs.append(pl.BlockSpec((tk, tn), lambda i, j, k: (k + r0, j + c2)))
        args.append(w)
    in_specs += list(epi_specs)
    args += list(epi_args)
    scratch = [pltpu.VMEM((tm, tn), F32)]
    if dual:
        scratch.append(pltpu.VMEM((tm, tn), F32))
    if pro_fn is not None:
        scratch.append(pltpu.VMEM((tm, tk), BF16))
    out_spec = pl.BlockSpec((tm, tn), lambda i, j, k: (i, j))
    if row_block0:
        shift = lambda sp: pl.BlockSpec(sp.block_shape,
                                        lambda i, j, k: sp.index_map(i + row_block0, j, k))
        in_specs = [shift(sp) for sp in in_specs]
        out_spec = shift(out_spec)
    return pl.pallas_call(
        functools.partial(_mm_body, nk=nk, n_pro=len(pro_args), pro_fn=pro_fn,
                          n_epi=len(epi_args), epi_fn=epi_fn, dual=dual),
        grid=(m // tm - row_block0, n_out // tn, nk),
        in_specs=in_specs,
        out_specs=out_spec,
        out_shape=jax.ShapeDtypeStruct((m, n_out), out_dtype),
        scratch_shapes=scratch,
        compiler_params=_cparams(("parallel", "arbitrary", "arbitrary")),
        name=name,
    )(*args)


def _rmsnorm_body(x_ref, g_ref, o_ref):
    x = x_ref[...]
    ms = jnp.mean(x * x, axis=-1, keepdims=True)
    o_ref[...] = x * lax.rsqrt(ms + NORM_EPS) * g_ref[...]


def _rmsnorm_rows(x, g, *, tm, row_block0):
    m, d = x.shape
    nblk = m // tm - row_block0
    return pl.pallas_call(
        _rmsnorm_body,
        grid=(nblk,),
        in_specs=[pl.BlockSpec((tm, d), lambda i: (i + row_block0, 0)),
                  pl.BlockSpec((1, d), lambda i: (0, 0))],
        out_specs=pl.BlockSpec((tm, d), lambda i: (i, 0)),
        out_shape=jax.ShapeDtypeStruct((nblk * tm, d), x.dtype),
        compiler_params=_cparams(("parallel",)),
        name="final_norm",
    )(x, g)


def _mm_tmajor_body(x_ref, w_ref, o_ref, *, nbg, kdim, tms, act):
    xb = jnp.concatenate([x_ref[:, b * kdim:(b + 1) * kdim] for b in range(nbg)], axis=0)
    acc = jnp.dot(xb, w_ref[...], preferred_element_type=F32)
    if act is not None:
        acc = act(acc)
    for b in range(nbg):
        o_ref[:, b, :] = acc[b * tms:(b + 1) * tms].astype(o_ref.dtype)


def _mm_tmajor(x, w, *, nb, tms, tn, act=None, name="mm_tmajor"):
    s_len = x.shape[0]
    kdim, n = w.shape
    nbg = SUBLANES
    assert x.shape[1] == nb * kdim and nb % nbg == 0 and s_len % tms == 0 and n % tn == 0
    return pl.pallas_call(
        functools.partial(_mm_tmajor_body, nbg=nbg, kdim=kdim, tms=tms, act=act),
        grid=(s_len // tms, nb // nbg, n // tn),
        in_specs=[pl.BlockSpec((tms, nbg * kdim), lambda i, g, j: (i, g)),
                  pl.BlockSpec((kdim, tn), lambda i, g, j: (0, j))],
        out_specs=pl.BlockSpec((tms, nbg, tn), lambda i, g, j: (i, g, j)),
        out_shape=jax.ShapeDtypeStruct((s_len, nb, n), F32),
        compiler_params=_cparams(("parallel", "parallel", "arbitrary")),
        name=name,
    )(x, w)


def _halo_specs(tm, width, col_map, m):
    per = tm // GRID_W
    last = m // GRID_W - 1
    return [pl.BlockSpec((tm, width), lambda i, *j: (i, col_map(*j))),
            pl.BlockSpec((GRID_W, width), lambda i, *j: (jnp.maximum(i * per - 1, 0), col_map(*j))),
            pl.BlockSpec((GRID_W, width), lambda i, *j: (jnp.minimum((i + 1) * per, last), col_map(*j)))]


def _mix_body(x_ref, up_ref, dn_ref, g_ref, sc_ref, sh_ref, mu_ref, *out_refs, tm, nbc, bpb, d):
    i = pl.program_id(0)
    g, sc, sh = g_ref[...], sc_ref[0], sh_ref[0]
    h = _rms_mod(x_ref[...].astype(F32), g, sc, sh)
    row = lax.broadcasted_iota(jnp.int32, (tm, 1), 0)

    def emit(lo, hi, shifted):
        hseg = h[:, lo:hi]
        xx = shifted - hseg
        for n, o_ref in enumerate(out_refs):
            o_ref[:, lo:hi] = (hseg + xx * mu_ref[n:n + 1, lo:hi]).astype(o_ref.dtype)

    @pl.when(i < nbc)
    def _():
        half = d // 2
        emit(0, half, jnp.where(row == 0, 0.0, pltpu.roll(h[:, :half], 1, axis=0)))
        emit(half, d, jnp.where(row == tm - 1, 0.0, pltpu.roll(h[:, half:], tm - 1, axis=0)))

    @pl.when(i >= nbc)
    def _():
        jb = (i - nbc) % bpb
        q = d // 4
        col = row % GRID_W
        emit(0, q, jnp.where(col == 0, 0.0, pltpu.roll(h[:, :q], 1, axis=0)))
        emit(q, 2 * q, jnp.where(col == GRID_W - 1, 0.0, pltpu.roll(h[:, q:2 * q], tm - 1, axis=0)))
        hu = _rms_mod(up_ref[...].astype(F32), g, sc, sh)[:, 2 * q:3 * q]
        hu = hu * jnp.where(jb > 0, 1.0, 0.0)
        emit(2 * q, 3 * q, jnp.concatenate([hu, h[:tm - GRID_W, 2 * q:3 * q]], axis=0))
        hd = _rms_mod(dn_ref[...].astype(F32), g, sc, sh)[:, 3 * q:]
        hd = hd * jnp.where(jb < bpb - 1, 1.0, 0.0)
        emit(3 * q, d, jnp.concatenate([h[GRID_W:, 3 * q:], hd], axis=0))


def _rwkv_mix(xs, g, sc, sh, mu, *, dm):
    m, d = xs.shape
    tm = dm.n_ctx
    nbc, bpb = dm.mc // tm, dm.n_lat // tm
    mod_spec = pl.BlockSpec((1, 1, d), lambda i: (dm.mod_index(i, tm), 0, 0))
    tm_spec = pl.BlockSpec((tm, d), lambda i: dm.time_major_block(i))
    bm_spec = pl.BlockSpec((None, tm, d), lambda i: dm.time_major_block(i)[::-1] + (0,))
    return pl.pallas_call(
        functools.partial(_mix_body, tm=tm, nbc=nbc, bpb=bpb, d=d),
        grid=(m // tm,),
        in_specs=_halo_specs(tm, d, lambda: 0, m) + [
            pl.BlockSpec((1, d), lambda i: (0, 0)), mod_spec, mod_spec,
            pl.BlockSpec(mu.shape, lambda i: (0, 0))],
        out_specs=[tm_spec] * 5 + [bm_spec],
        out_shape=[jax.ShapeDtypeStruct((dm.s_len, dm.nb * d), BF16)] * 5
        + [jax.ShapeDtypeStruct((dm.nb, dm.s_len, d), BF16)],
        compiler_params=_cparams(("parallel",)),
        name="rwkv_mix",
    )(xs, xs, xs, g, sc, sh, mu)


def _conv_body(x_ref, up_ref, dn_ref, w_ref, b_ref, o_ref, *, tm, nbc, bpb, n_qcols, q_scale):
    i = pl.program_id(0)
    j = pl.program_id(1)
    row = lax.broadcasted_iota(jnp.int32, (tm, 1), 0)
    x = x_ref[...]
    scale = jnp.where(j < n_qcols, q_scale, 1.0)

    def finish(y):
        y = y + b_ref[...]
        o_ref[...] = (y * jax.nn.sigmoid(y) * scale).astype(o_ref.dtype)

    @pl.when(i < nbc)
    def _():
        prev = jnp.where(row == 0, 0.0, pltpu.roll(x, 1, axis=0))
        nxt = jnp.where(row == tm - 1, 0.0, pltpu.roll(x, tm - 1, axis=0))
        finish(prev * w_ref[3:4, :] + x * w_ref[4:5, :] + nxt * w_ref[5:6, :])

    @pl.when(i >= nbc)
    def _():
        jb = (i - nbc) % bpb
        te = tm + 2 * GRID_W
        ext = jnp.concatenate([up_ref[...] * jnp.where(jb > 0, 1.0, 0.0), x,
                               dn_ref[...] * jnp.where(jb < bpb - 1, 1.0, 0.0)], axis=0)
        col = lax.broadcasted_iota(jnp.int32, (te, 1), 0) % GRID_W
        taps = (jnp.where(col == 0, 0.0, pltpu.roll(ext, 1, axis=0)), ext,
                jnp.where(col == GRID_W - 1, 0.0, pltpu.roll(ext, te - 1, axis=0)))
        y = jnp.zeros((tm, x.shape[1]), F32)
        for di in range(3):
            for dj in range(3):
                y = y + taps[dj][di * GRID_W:di * GRID_W + tm] * w_ref[3 * di + dj:3 * di + dj + 1, :]
        finish(y)


def _mlstm_conv(u, conv_w, conv_b, *, dm, width, tc, q_scale):
    m = u.shape[0]
    tm = dm.n_ctx
    nbc, bpb = dm.mc // tm, dm.n_lat // tm
    w9 = conv_w.reshape(9, width)
    return pl.pallas_call(
        functools.partial(_conv_body, tm=tm, nbc=nbc, bpb=bpb, n_qcols=width // 2 // tc,
                          q_scale=q_scale),
        grid=(m // tm, width // tc),
        in_specs=_halo_specs(tm, tc, lambda j: j, m) + [
            pl.BlockSpec((9, tc), lambda i, j: (0, j)), pl.BlockSpec((1, tc), lambda i, j: (0, j))],
        out_specs=pl.BlockSpec((tm, tc), lambda i, j: (i, j)),
        out_shape=jax.ShapeDtypeStruct((m, width), F32),
        compiler_params=_cparams(("parallel", "parallel")),
        name="mlstm_conv",
    )(u, u, u, w9, conv_b[None, :])


def _split_parity(ref, t, n):
    x = ref[t]
    hpn = x.shape[1] // (2 * n)
    rows = jnp.concatenate([x[:, hp * 2 * n:(hp + 1) * 2 * n] for hp in range(hpn)], axis=0)
    xt = rows.T
    return xt[:n], xt[n:]


def _rwkv_scan_body(r_ref, k_ref, v_ref, wp_ref, ap_ref, kk_ref, ka_ref, rkp_ref, w0_ref, a0_ref,
                    y_ref, rko_ref, s_ref, sz_ref, r_s, v_s, w_s, k_s, z_s, b_s, *, tblk, n):
    zdir = pl.program_id(0)
    tb = pl.program_id(2)

    @pl.when(tb == 0)
    def _():
        s_ref[...] = jnp.zeros_like(s_ref)

    def prep(t, carry):
        r2, k2, v2 = _split_parity(r_ref, t, n), _split_parity(k_ref, t, n), _split_parity(v_ref, t, n)
        wp2, ap2 = _split_parity(wp_ref, t, n), _split_parity(ap_ref, t, n)
        for g in range(2):
            k_t = k2[g]
            a = jax.nn.sigmoid(a0_ref[g] + ap2[g])
            w_s[t, g] = jnp.exp(-jnp.exp(-0.5) * jax.nn.sigmoid(w0_ref[g] + wp2[g]))
            kr = k_t * kk_ref[g]
            nrm = jnp.sqrt(jnp.sum(kr * kr, axis=0, keepdims=True))
            kkn = kr / jnp.maximum(nrm, 1e-12)
            z_s[t, g] = -kkn
            b_s[t, g] = kkn * a
            km = k_t * (1.0 + (a - 1.0) * ka_ref[g])
            k_s[t, g] = km
            r_s[t, g] = r2[g]
            v_s[t, g] = v2[g]
            rko_ref[t, g] = jnp.sum(r2[g] * km * rkp_ref[g], axis=0, keepdims=True)
        return carry

    lax.fori_loop(0, tblk, prep, 0, unroll=4)

    bwd = zdir == 1
    t_first = jnp.where(bwd, tblk - 1, 0)
    t_step = jnp.where(bwd, -1, 1)

    def row(ref, t, g, kk):
        return jnp.broadcast_to(ref[t, g, pl.ds(kk, 1), :], (n, LANES))

    for g in range(2):
        acc = jnp.zeros((n, LANES), F32)
        for kk in range(n):
            acc = acc + s_ref[g, kk] * row(z_s, t_first, g, kk)
        sz_ref[g] = acc

    def step(i, carry):
        t = t_first + i * t_step
        t_next = jnp.clip(t + t_step, 0, tblk - 1)
        for g in range(2):
            sz = sz_ref[g]
            v_t = v_s[t, g]
            y = jnp.zeros((n, LANES), F32)
            sz_next = jnp.zeros((n, LANES), F32)
            for kk in range(n):
                s_new = (s_ref[g, kk] * row(w_s, t, g, kk) + sz * row(b_s, t, g, kk)
                         + v_t * row(k_s, t, g, kk))
                s_ref[g, kk] = s_new
                y = y + s_new * row(r_s, t, g, kk)
                sz_next = sz_next + s_new * row(z_s, t_next, g, kk)
            y_ref[t, g] = y
            sz_ref[g] = sz_next
        return carry

    lax.fori_loop(0, tblk, step, 0)


def _rwkv_scan(r, k, v, wp, ap, kk_p, ka_p, rk_p, w0_p, a0_p, *, n_ctx, tblk):
    s_len, nbat, d = r.shape
    n = RWKV_HEAD
    nbg = LANES // (d // (2 * n))
    assert 2 * n == LANES and nbat % nbg == 0 and n_ctx % tblk == 0 and s_len % tblk == 0
    ngrp = nbat // nbg
    nbc, nb = n_ctx // tblk, s_len // tblk

    def tmap(zd, p):
        rev = jnp.where(p < nbc, nbc - 1 - p, nb - 1 - (p - nbc))
        return jnp.where(zd == 0, p, rev)

    shared = pl.BlockSpec((tblk, nbg, d), lambda zd, g, p: (tmap(zd, p), g, 0))
    perdir = pl.BlockSpec((tblk, nbg, d), lambda zd, g, p: (tmap(zd, p), g, zd))
    par = pl.BlockSpec((2, n, LANES), lambda zd, g, p: (0, 0, 0))
    par_dir = pl.BlockSpec((None, 2, n, LANES), lambda zd, g, p: (zd, 0, 0, 0))
    step_buf = pltpu.VMEM((tblk, 2, n, LANES), F32)
    return pl.pallas_call(
        functools.partial(_rwkv_scan_body, tblk=tblk, n=n),
        grid=(2, ngrp, nb),
        in_specs=[shared, shared, shared, perdir, perdir, par, par, par, par_dir, par_dir],
        out_specs=[pl.BlockSpec((None, tblk, 2, n, LANES), lambda zd, g, p: (zd, tmap(zd, p), g, 0, 0)),
                   pl.BlockSpec((None, tblk, 2, 1, LANES), lambda zd, g, p: (zd, tmap(zd, p), g, 0, 0))],
        out_shape=[jax.ShapeDtypeStruct((2, s_len, 2 * ngrp, n, LANES), F32),
                   jax.ShapeDtypeStruct((2, s_len, 2 * ngrp, 1, LANES), F32)],
        scratch_shapes=[pltpu.VMEM((2, n, n, LANES), F32), pltpu.VMEM((2, n, LANES), F32)]
        + [step_buf] * 6,
        compiler_params=_cparams(("parallel", "parallel", "arbitrary")),
        name="rwkv_scan",
    )(r, k, v, wp, ap, kk_p, ka_p, rk_p, w0_p, a0_p)


def _rwkv_norm_body(y_ref, v_ref, rk_ref, lw_ref, lb_ref, o_ref, *, eps, tblk, n):
    def step(t, carry):
        v2 = _split_parity(v_ref, t, n)
        outs = []
        for g in range(2):
            ys = y_ref[0, t, g] + y_ref[1, t, g]
            mu = jnp.mean(ys, axis=0, keepdims=True)
            var = jnp.mean(jnp.square(ys - mu), axis=0, keepdims=True)
            yn = (ys - mu) * lax.rsqrt(var + eps) * lw_ref[g] + lb_ref[g]
            outs.append(yn + (rk_ref[0, t, g] + rk_ref[1, t, g]) * v2[g])
        rows = jnp.concatenate(outs, axis=0).T
        nbg = o_ref.shape[0]
        o_ref[:, t, :] = jnp.concatenate([rows[hp * nbg:(hp + 1) * nbg] for hp in range(LANES // nbg)],
                                         axis=1)
        return carry

    lax.fori_loop(0, tblk, step, 0, unroll=4)


def _rwkv_norm(y, v, rk, ln_w_p, ln_b_p, *, tblk, eps):
    s_len, nbat, d = v.shape
    n = RWKV_HEAD
    nbg = LANES // (d // (2 * n))
    par = pl.BlockSpec((2, n, LANES), lambda p, g: (0, 0, 0))
    nat = pl.BlockSpec((tblk, nbg, d), lambda p, g: (p, g, 0))
    return pl.pallas_call(
        functools.partial(_rwkv_norm_body, eps=eps, tblk=tblk, n=n),
        grid=(s_len // tblk, nbat // nbg),
        in_specs=[pl.BlockSpec((2, tblk, 2, n, LANES), lambda p, g: (0, p, g, 0, 0)), nat,
                  pl.BlockSpec((2, tblk, 2, 1, LANES), lambda p, g: (0, p, g, 0, 0)), par, par],
        out_specs=pl.BlockSpec((nbg, tblk, d), lambda p, g: (g, p, 0)),
        out_shape=jax.ShapeDtypeStruct((nbat, s_len, d), F32),
        compiler_params=_cparams(("parallel", "parallel")),
        name="rwkv_norm",
    )(y, v, rk, ln_w_p, ln_b_p)


def _mlstm_body(q_ref, k_ref, v_ref, ic_ref, fc_ref, ir_ref, fr_ref, o_ref,
                c_ref, n_ref, m_ref, *, nh, dk, dv):
    zdir = pl.program_id(0)
    p = pl.program_id(2)

    @pl.when(p == 0)
    def _():
        c_ref[...] = jnp.zeros_like(c_ref)
        n_ref[...] = jnp.zeros_like(n_ref)
        m_ref[...] = jnp.zeros_like(m_ref)

    ti = lax.broadcasted_iota(jnp.int32, (CHUNK, CHUNK), 0)
    tj = lax.broadcasted_iota(jnp.int32, (CHUNK, CHUNK), 1)
    mask = (ti - tj) * (1 - 2 * zdir) >= 0
    tri = mask.astype(F32)
    fcol = fc_ref[0]
    icol = ic_ref[0]
    frow = fr_ref[0, 0]
    irow = ir_ref[0, 0]
    hi = lax.Precision.HIGHEST
    bcum_col = jnp.dot(tri, fcol, precision=hi, preferred_element_type=F32)
    bcum_row = lax.dot_general(frow, tri, (((1,), (1,)), ((), ())), precision=hi,
                               preferred_element_type=F32)

    hs = range(nh)
    q32 = [q_ref[:, h * dk:(h + 1) * dk].astype(F32) for h in hs]
    qb = [x.astype(BF16) for x in q32]
    k32 = [k_ref[:, h * dk:(h + 1) * dk].astype(F32) for h in hs]
    vb = [v_ref[:, h * dv:(h + 1) * dv].astype(BF16) for h in hs]
    c_st = [c_ref[h] for h in hs]
    n_st = [n_ref[h] for h in hs]
    m_st = [m_ref[h][:, :1] for h in hs]
    bc = [bcum_col[:, h:h + 1] for h in hs]
    log_d = [jnp.where(mask, bc[h] - bcum_row[h:h + 1, :] + irow[h:h + 1, :], -jnp.inf) for h in hs]
    log_inter = [bc[h] + m_st[h] for h in hs]
    m_t = [jnp.maximum(log_inter[h], jnp.max(log_d[h], axis=-1, keepdims=True)) for h in hs]
    b_end = [jnp.sum(frow[h:h + 1, :], axis=-1, keepdims=True) for h in hs]
    a_col = [b_end[h] - bc[h] + icol[:, h:h + 1] for h in hs]
    m_new = [jnp.maximum(b_end[h] + m_st[h], jnp.max(a_col[h], axis=0, keepdims=True)) for h in hs]
    qk = [lax.dot_general(qb[h], k32[h].astype(BF16), (((1,), (1,)), ((), ())),
                          preferred_element_type=F32) for h in hs]
    qc = [jnp.dot(qb[h], c_st[h].astype(BF16), preferred_element_type=F32) for h in hs]
    s = [qk[h] * jnp.exp(log_d[h] - m_t[h]) for h in hs]
    sv = [jnp.dot(s[h].astype(BF16), vb[h], preferred_element_type=F32) for h in hs]
    wk = [jnp.exp(a_col[h] - m_new[h]) * k32[h] for h in hs]
    kv = [lax.dot_general(wk[h].astype(BF16), vb[h], (((0,), (0,)), ((), ())),
                          preferred_element_type=F32) for h in hs]
    for h in hs:
        w_inter = jnp.exp(log_inter[h] - m_t[h])
        qn = jnp.sum(q32[h] * n_st[h], axis=-1, keepdims=True)
        den = w_inter * qn + jnp.sum(s[h], axis=-1, keepdims=True)
        num = w_inter * qc[h] + sv[h]
        o_ref[0, :, h * dv:(h + 1) * dv] = num / jnp.maximum(jnp.abs(den), jnp.exp(-m_t[h]))
        dec = jnp.exp(b_end[h] + m_st[h] - m_new[h])
        c_ref[h] = dec * c_st[h] + kv[h]
        n_ref[h] = dec * n_st[h] + jnp.sum(wk[h], axis=0, keepdims=True)
        m_ref[h] = jnp.broadcast_to(m_new[h], (1, LANES))


def _mlstm_scan(qk, u, v_col0, ig, lf, *, nb, n_ctx, n_lat):
    m = qk.shape[0]
    nh = MLSTM_HEADS
    dkt = qk.shape[1] // 2
    dk = dkt // nh
    dvt = 2 * dkt
    dv = dvt // nh
    ncc, ncl = n_ctx // CHUNK, n_lat // CHUNK
    nchunks = ncc + ncl
    ctx_blocks = nb * ncc
    assert v_col0 % dvt == 0

    def rmap(zd, bb, p):
        rev = jnp.where(p < ncc, ncc - 1 - p, nchunks - 1 - (p - ncc))
        ch = jnp.where(zd == 0, p, rev)
        return jnp.where(ch < ncc, bb * ncc + ch, ctx_blocks + bb * ncl + (ch - ncc))

    ig_row = jnp.swapaxes(ig.reshape(2, m // CHUNK, CHUNK, nh), 2, 3)
    lf_row = jnp.swapaxes(lf.reshape(2, m // CHUNK, CHUNK, nh), 2, 3)
    col_spec = pl.BlockSpec((1, CHUNK, nh), lambda zd, bb, p: (zd, rmap(zd, bb, p), 0))
    row_spec = pl.BlockSpec((1, 1, nh, CHUNK), lambda zd, bb, p: (zd, rmap(zd, bb, p), 0, 0))
    return pl.pallas_call(
        functools.partial(_mlstm_body, nh=nh, dk=dk, dv=dv),
        grid=(2, nb, nchunks),
        in_specs=[pl.BlockSpec((CHUNK, dkt), lambda zd, bb, p: (rmap(zd, bb, p), 0)),
                  pl.BlockSpec((CHUNK, dkt), lambda zd, bb, p: (rmap(zd, bb, p), 1)),
                  pl.BlockSpec((CHUNK, dvt), lambda zd, bb, p: (rmap(zd, bb, p), v_col0 // dvt)),
                  col_spec, col_spec, row_spec, row_spec],
        out_specs=pl.BlockSpec((1, CHUNK, dvt), lambda zd, bb, p: (zd, rmap(zd, bb, p), 0)),
        out_shape=jax.ShapeDtypeStruct((2, m, dvt), F32),
        scratch_shapes=[pltpu.VMEM((nh, dk, dv), F32), pltpu.VMEM((nh, 1, dk), F32),
                        pltpu.VMEM((nh, 1, LANES), F32)],
        compiler_params=_cparams(("parallel", "parallel", "arbitrary")),
        name="mlstm_scan",
    )(qk, qk, u, ig, lf, ig_row, lf_row)


class _Dims:
    def __init__(self, nb, n_ctx, n_lat, d):
        self.nb, self.n_ctx, self.n_lat, self.d = nb, n_ctx, n_lat, d
        self.mc, self.mx = nb * n_ctx, nb * n_lat
        self.m = self.mc + self.mx
        self.s_len = n_ctx + n_lat
        self.tm = 1024 if (self.mc % 1024 == 0 and n_lat % 1024 == 0) else n_ctx
        self.tm_half = max(self.tm // 2, n_ctx) if self.tm > n_ctx else self.tm
        assert self.mc % self.tm == 0 and n_lat % self.tm == 0 and n_ctx % GRID_W == 0

    def mod_index(self, i, tm):
        nbc = self.mc // tm
        bpb = self.n_lat // tm
        return jnp.where(i < nbc, self.nb, (i - nbc) // bpb)

    def time_major_block(self, i):
        nbc = self.mc // self.n_ctx
        bpb = self.n_lat // self.n_ctx
        return (jnp.where(i < nbc, 0, 1 + (i - nbc) % bpb), jnp.where(i < nbc, i, (i - nbc) // bpb))

    def time_view(self, a):
        return a.reshape(self.s_len, self.nb, -1)

    def chan_to_scan(self, p):
        lead = p.shape[:-1]
        hpn = p.shape[-1] // (2 * RWKV_HEAD)
        pt = jnp.moveaxis(p.reshape(lead + (hpn, 2, RWKV_HEAD)), -3, -1)
        pt = jnp.broadcast_to(pt[..., None], lead + (2, RWKV_HEAD, hpn, LANES // hpn))
        return pt.reshape(lead + (2, RWKV_HEAD, LANES))


def _pad_to(a, axis, mult):
    pad = -a.shape[axis] % mult
    if pad == 0:
        return a
    widths = [(0, 0)] * a.ndim
    widths[axis] = (0, pad)
    return jnp.pad(a, widths)


def _rwkv_layer(dm, xs, g1, sc1, sh1, p, v_first):
    m, d, tm = dm.m, dm.d, dm.tm
    xr, xw, xk, xv, xa, xg = _rwkv_mix(xs, g1, sc1, sh1, p['mu'], dm=dm)
    mm = functools.partial(_matmul, m=m, tm=tm)
    mt = functools.partial(_mm_tmajor, nb=dm.nb, tms=2 * GRID_W)
    rows = lambda a: a.reshape(m, -1)
    r = mt(xr, p['w_r'], tn=512, name="mm_r")
    k = mt(xk, p['w_k'], tn=512, name="mm_k")
    v = rows(mt(xv, p['w_v'], tn=512, name="mm_v"))
    row_spec = pl.BlockSpec((tm, 512), lambda i, j, k: (i, j))
    vec_spec = pl.BlockSpec((1, 512), lambda i, j, k: (0, j))
    if v_first is not None:
        lv = rows(mt(xv, p['v1'], tn=LANES, name="mm_v1"))
        v = mm(lv, p['v2'], kdim=LANES, tn=512, name="mm_vres",
               epi=(_epi_vres, [v, v_first, p['v0']], [row_spec, row_spec, vec_spec]))
    lw = rows(mt(xw, p['w1'], tn=2 * LANES, act=jnp.tanh, name="mm_w1"))
    la = rows(mt(xa, p['a1'], tn=2 * LANES, name="mm_a1"))
    lg = mm(xg.reshape(m, d), p['g1'], kdim=d, tn=p['g1'].shape[1], out_dtype=BF16, name="mm_g1",
            epi=(jax.nn.sigmoid, [], []))
    g = mm(lg, p['g2'], kdim=p['g2'].shape[0], tn=512, name="mm_g2")
    wp = mm(lw, p['w2'], kdim=2 * LANES, tn=512, name="mm_w2")
    ap = mm(la, p['a2'], kdim=2 * LANES, tn=512, name="mm_a2")
    cs, hv = dm.chan_to_scan, dm.time_view
    y, rk = _rwkv_scan(hv(r), hv(k), hv(v), hv(wp), hv(ap),
                       cs(p['k_k']), cs(p['k_a']), cs(p['r_k']), cs(p['w0']), cs(p['a0']),
                       n_ctx=dm.n_ctx, tblk=32)
    pre = _rwkv_norm(y, hv(v), rk, cs(p['ln_w']), cs(p['ln_b']), tblk=32, eps=RWKV_HEAD * 1e-5)
    return pre, g, v


def _mlstm_layer(dm, xs, g1, sc1, sh1, p):
    m, d, tm = dm.m, dm.d, dm.tm
    nh = MLSTM_HEADS
    mod_spec = pl.BlockSpec((1, 1, d), lambda i, j, k: (dm.mod_index(i, tm), 0, 0))
    u = _matmul(None, p['w_in'], m=m, kdim=d, tm=tm, tn=896, name="mm_mlstm_in",
                pro=(_pro_normmod, [xs, g1, sc1, sh1],
                     [pl.BlockSpec((tm, d), lambda i, j, k: (i, 0)),
                      pl.BlockSpec((1, d), lambda i, j, k: (0, 0)), mod_spec, mod_spec]))
    qk = _mlstm_conv(u, p['conv_w'], p['conv_b'], dm=dm, width=d, tc=min(512, d // 2),
                     q_scale=float(d // 2 // nh) ** -0.5)
    gates = u[:, 3 * d:3 * d + 4 * nh].reshape(m, 2, 2, nh) + p['b_gate']
    gates = GATE_CAP * jnp.tanh(gates / GATE_CAP)
    ig = jnp.moveaxis(gates[:, :, 0, :], 1, 0)
    lf = jnp.moveaxis(jax.nn.log_sigmoid(gates[:, :, 1, :]), 1, 0)
    hs = _mlstm_scan(qk, u, d, ig, lf, nb=dm.nb, n_ctx=dm.n_ctx, n_lat=dm.n_lat)
    return hs, u


def kernel(x, c, ctx, c_ctx, mod_w, mod_b, norm_g, final_g, rwkv_mu, rwkv_w_r, rwkv_w_k, rwkv_w_v, rwkv_w_o, rwkv_w0, rwkv_w1, rwkv_w2, rwkv_a0, rwkv_a1, rwkv_a2, rwkv_g1, rwkv_g2, rwkv_k_k, rwkv_k_a, rwkv_r_k, rwkv_ln_w, rwkv_ln_b, rwkv_v0, rwkv_v1, rwkv_v2, mlstm_w_in, mlstm_b_gate, mlstm_conv_w, mlstm_conv_b, mlstm_norm_w, mlstm_w_out, ffn_w_in, ffn_w_out):
    nb, n_lat, d = x.shape
    n_ctx = ctx.shape[1]
    depth = mod_w.shape[0]
    d_ff = ffn_w_out.shape[1]
    dm = _Dims(nb, n_ctx, n_lat, d)
    m, tm, tmh = dm.m, dm.tm, dm.tm_half
    bf = lambda a: a.astype(BF16)

    cond = jax.nn.silu(jnp.concatenate([c, c_ctx[None, :]], axis=0))
    rows = cond.shape[0]
    cond = _pad_to(cond, 0, 2 * SUBLANES)
    mod_w2 = mod_w.reshape(depth * d, 6 * d)
    mods = []
    for i in range(depth):
        mo = _matmul(cond, mod_w2, m=cond.shape[0], kdim=d, tm=cond.shape[0], tn=512,
                     w_row0=i * d, name="mm_mod") + mod_b[i]
        mods.append(mo[:rows].reshape(rows, 6, 1, d))

    def gate_res(res, gate, tm_, tn_=512):
        return (_epi_gate_res, [res, gate],
                [pl.BlockSpec((tm_, tn_), lambda i, j, k: (i, j)),
                 pl.BlockSpec((1, 1, tn_), lambda i, j, k: (dm.mod_index(i, tm_), 0, j))])

    xs = jnp.concatenate([ctx.reshape(dm.mc, d), x.reshape(dm.mx, d)], axis=0)
    v_first = None
    for i in range(depth):
        j = i // 2
        sh1, sc1, gt1, sh2, sc2, gt2 = [mods[i][:, n] for n in range(6)]
        g1 = norm_g[i, 0][None, :]
        g2 = norm_g[i, 1][None, :]
        full_k = lambda tm_: pl.BlockSpec((tm_, d), lambda i, j, k: (i, 0))
        skip = (lambda tm_: dm.mc // tm_) if i == depth - 1 else (lambda tm_: 0)
        if i % 2 == 0:
            lora = lambda a: bf(jnp.concatenate([_pad_to(a[0], 1, LANES), _pad_to(a[1], 1, LANES)], axis=1))

            def lora_up(a):
                ap_ = _pad_to(a, 1, LANES)
                zero = jnp.zeros_like(ap_[0])
                return bf(jnp.concatenate([jnp.concatenate([ap_[0], zero], axis=1),
                                           jnp.concatenate([zero, ap_[1]], axis=1)], axis=0))

            p = {'mu': rwkv_mu[j], 'w_r': bf(rwkv_w_r[j]), 'w_k': bf(rwkv_w_k[j]), 'w_v': bf(rwkv_w_v[j]),
                 'w0': rwkv_w0[j], 'w1': lora(rwkv_w1[j]), 'w2': lora_up(rwkv_w2[j]),
                 'a0': rwkv_a0[j], 'a1': lora(rwkv_a1[j]), 'a2': lora_up(rwkv_a2[j]),
                 'g1': bf(rwkv_g1[j]), 'g2': bf(rwkv_g2[j]),
                 'k_k': rwkv_k_k[j], 'k_a': rwkv_k_a[j], 'r_k': rwkv_r_k[j].reshape(-1),
                 'ln_w': rwkv_ln_w[j], 'ln_b': rwkv_ln_b[j]}
            if j > 0:
                p['v0'] = rwkv_v0[j - 1][None, :]
                p['v1'] = bf(_pad_to(rwkv_v1[j - 1], 1, LANES))
                p['v2'] = bf(_pad_to(rwkv_v2[j - 1], 0, LANES))
            pre, g, v_cur = _rwkv_layer(dm, xs, g1, sc1, sh1, p, v_first if j > 0 else None)
            if j == 0:
                v_first = v_cur
            tmo = n_ctx
            tm_spec = pl.BlockSpec((None, tmo, d), lambda i, j, k: dm.time_major_block(i)[::-1] + (0,))
            xs = _matmul(None, bf(rwkv_w_o[j]), m=m, kdim=d, tm=tmo, tn=d, name="mm_rwkv_out",
                         row_block0=skip(tmo),
                         pro=(_pro_mulg, [pre, g.reshape(nb, dm.s_len, d)],
                              [tm_spec, tm_spec]),
                         epi=gate_res(xs, gt1, tmo, d))
        else:
            p = {'w_in': bf(_pad_to(mlstm_w_in[j], 1, 896)),
                 'b_gate': mlstm_b_gate[j], 'conv_w': mlstm_conv_w[j], 'conv_b': mlstm_conv_b[j]}
            hs, u = _mlstm_layer(dm, xs, g1, sc1, sh1, p)
            xs = _matmul(None, bf(mlstm_w_out[j]), m=m, kdim=d, tm=tmh, tn=512, name="mm_mlstm_out",
                         row_block0=skip(tmh),
                         pro=(_pro_mlstm_read, [hs, u, mlstm_norm_w[j][None, :]],
                              [pl.BlockSpec((2, tmh, d), lambda i, j, k: (0, i, 0)),
                               pl.BlockSpec((tmh, d), lambda i, j, k: (i, 2)),
                               pl.BlockSpec((1, d), lambda i, j, k: (0, 0))]),
                         epi=gate_res(xs, gt1, tmh))
        mod_spec = pl.BlockSpec((1, 1, d), lambda i, j, k: (dm.mod_index(i, tm), 0, 0))
        hid = _matmul(None, bf(ffn_w_in[i]), m=m, kdim=d, tm=tm, tn=512, out_dtype=BF16,
                      n_out=d_ff, w2_col0=d_ff, name="mm_ffn_in", row_block0=skip(tm),
                      pro=(_pro_normmod, [xs, g2, sc2, sh2],
                           [full_k(tm), pl.BlockSpec((1, d), lambda i, j, k: (0, 0)), mod_spec, mod_spec]),
                      epi=(_epi_swiglu, [], []))
        xs = _matmul(hid, bf(ffn_w_out[i]), m=m, kdim=d_ff, tm=tm, tn=512, row_block0=skip(tm),
                     name="mm_ffn_out", epi=gate_res(xs, gt2, tm))
    out = _rmsnorm_rows(xs, final_g[None, :], tm=tm, row_block0=dm.mc // tm)
    return out.reshape(nb, n_lat, d)
```

```python
import functools

import jax
import jax.numpy as jnp
from jax import lax
from jax.experimental import pallas as pl
from jax.experimental.pallas import tpu as pltpu

F32 = jnp.float32
BF16 = jnp.bfloat16

GRID_W = 64
NORM_EPS = 1e-6
RWKV_HEAD = 64
MLSTM_HEADS = 8
CHUNK = 64
GATE_CAP = 15.0
LANES = 128
SUBLANES = 8
VMEM_LIMIT = 56 * 1024 * 1024
MLSTM_IN_TN = 1280


def _cparams(sem):
    return pltpu.CompilerParams(dimension_semantics=sem, vmem_limit_bytes=VMEM_LIMIT)


def _rms_mod(x, g, sc, sh):
    ms = jnp.mean(x * x, axis=-1, keepdims=True)
    return x * lax.rsqrt(ms + NORM_EPS) * g * (1.0 + sc) + sh


def _head_ln_lanes(y, nheads, eps):
    hd = y.shape[-1] // nheads
    out = []
    for h in range(nheads):
        seg = y[:, h * hd:(h + 1) * hd]
        mu = jnp.mean(seg, axis=-1, keepdims=True)
        var = jnp.mean(jnp.square(seg - mu), axis=-1, keepdims=True)
        out.append((seg - mu) * lax.rsqrt(var + eps))
    return jnp.concatenate(out, axis=-1)


def _pro_normmod(x_ref, g_ref, sc_ref, sh_ref):
    return _rms_mod(x_ref[...].astype(F32), g_ref[...], sc_ref[0], sh_ref[0]).astype(BF16)


def _pro_mulg(x_ref, g_ref):
    return (x_ref[...] * g_ref[...]).astype(BF16)


def _pro_mlstm_read(h_ref, o_ref, nw_ref):
    hn = _head_ln_lanes(h_ref[0] + h_ref[1], MLSTM_HEADS, NORM_EPS) * nw_ref[...]
    return (hn * jax.nn.sigmoid(o_ref[...])).astype(BF16)


def _epi_swiglu(acc, acc2):
    return acc * jax.nn.sigmoid(acc) * acc2


def _epi_gate_res(acc, res_ref, gate_ref):
    return res_ref[...] + gate_ref[0] * acc


def _epi_vres(acc, v_ref, vf_ref, v0_ref):
    v = v_ref[...]
    return v + (vf_ref[...] - v) * jax.nn.sigmoid(v0_ref[...] + acc)


def _mm_body(*refs, nk, n_pro, pro_fn, n_epi, epi_fn, dual):
    it = iter(refs)
    pro_refs = [next(it) for _ in range(n_pro)]
    w_ref = next(it)
    w2_ref = next(it) if dual else None
    epi_refs = [next(it) for _ in range(n_epi)]
    o_ref = next(it)
    acc_ref = next(it)
    acc2_ref = next(it) if dual else None
    xb_ref = next(it) if pro_fn is not None else None

    j = pl.program_id(1)
    k = pl.program_id(2)

    if pro_fn is not None:
        @pl.when(j == 0)
        def _():
            xb_ref[...] = pro_fn(*pro_refs)
        xb = xb_ref[...]
    else:
        xb = pro_refs[0][...].astype(BF16)

    @pl.when(k == 0)
    def _():
        acc_ref[...] = jnp.zeros_like(acc_ref)
        if dual:
            acc2_ref[...] = jnp.zeros_like(acc2_ref)

    acc_ref[...] += jnp.dot(xb, w_ref[...].astype(BF16), preferred_element_type=F32)
    if dual:
        acc2_ref[...] += jnp.dot(xb, w2_ref[...].astype(BF16), preferred_element_type=F32)

    @pl.when(k == nk - 1)
    def _():
        acc = acc_ref[...]
        if dual:
            acc = epi_fn(acc, acc2_ref[...], *epi_refs)
        elif epi_fn is not None:
            acc = epi_fn(acc, *epi_refs)
        o_ref[...] = acc.astype(o_ref.dtype)


def _matmul(x, w, *, m, kdim, tm, tn, tk=None, out_dtype=F32, n_out=None, w_col0=0, w2_col0=None,
            w_row0=0, x_col0=0, row_block0=0, pro=None, epi=None, name="mm"):
    n_out = w.shape[1] if n_out is None else n_out
    tk = kdim if tk is None else tk
    dual = w2_col0 is not None
    assert m % tm == 0 and n_out % tn == 0 and kdim % tk == 0
    assert w_col0 % tn == 0 and w_row0 % tk == 0 and x_col0 % tk == 0
    nk = kdim // tk
    c0, r0, xc0 = w_col0 // tn, w_row0 // tk, x_col0 // tk
    pro_fn, pro_args, pro_specs = pro if pro is not None else (None, [x], [
        pl.BlockSpec((tm, tk), lambda i, j, k: (i, k + xc0))])
    if pro_fn is not None:
        assert nk == 1
    epi_fn, epi_args, epi_specs = epi if epi is not None else (None, [], [])
    in_specs = list(pro_specs) + [pl.BlockSpec((tk, tn), lambda i, j, k: (k + r0, j + c0))]
    args = list(pro_args) + [w]
    if dual:
        assert w2_col0 % tn == 0
        c2 = w2_col0 // tn
        in_specs.append(pl.BlockSpec((tk, tn), lambda i, j, k: (k + r0, j + c2)))
        args.append(w)
    in_specs += list(epi_specs)
    args += list(epi_args)
    scratch = [pltpu.VMEM((tm, tn), F32)]
    if dual:
        scratch.append(pltpu.VMEM((tm, tn), F32))
    if pro_fn is not None:
        scratch.append(pltpu.VMEM((tm, tk), BF16))
    out_spec = pl.BlockSpec((tm, tn), lambda i, j, k: (i, j))
    if row_block0:
        shift = lambda sp: pl.BlockSpec(sp.block_shape,
                                        lambda i, j, k: sp.index_map(i + row_block0, j, k))
        in_specs = [shift(sp) for sp in in_specs]
        out_spec = shift(out_spec)
    return pl.pallas_call(
        functools.partial(_mm_body, nk=nk, n_pro=len(pro_args), pro_fn=pro_fn,
                          n_epi=len(epi_args), epi_fn=epi_fn, dual=dual),
        grid=(m // tm - row_block0, n_out // tn, nk),
        in_specs=in_specs,
        out_specs=out_spec,
        out_shape=jax.ShapeDtypeStruct((m, n_out), out_dtype),
        scratch_shapes=scratch,
        compiler_params=_cparams(("parallel", "arbitrary", "arbitrary")),
        name=name,
    )(*args)


def _rmsnorm_body(x_ref, g_ref, o_ref):
    x = x_ref[...]
    ms = jnp.mean(x * x, axis=-1, keepdims=True)
    o_ref[...] = x * lax.rsqrt(ms + NORM_EPS) * g_ref[...]


def _rmsnorm_rows(x, g, *, tm, row_block0):
    m, d = x.shape
    nblk = m // tm - row_block0
    return pl.pallas_call(
        _rmsnorm_body,
        grid=(nblk,),
        in_specs=[pl.BlockSpec((tm, d), lambda i: (i + row_block0, 0)),
                  pl.BlockSpec((1, d), lambda i: (0, 0))],
        out_specs=pl.BlockSpec((tm, d), lambda i: (i, 0)),
        out_shape=jax.ShapeDtypeStruct((nblk * tm, d), x.dtype),
        compiler_params=_cparams(("parallel",)),
        name="final_norm",
    )(x, g)


def _mm_tmajor_body(x_ref, w_ref, o_ref, *, nbg, kdim, tms, act):
    xb = jnp.concatenate([x_ref[:, b * kdim:(b + 1) * kdim] for b in range(nbg)], axis=0)
    acc = jnp.dot(xb, w_ref[...], preferred_element_type=F32)
    if act is not None:
        acc = act(acc)
    for b in range(nbg):
        o_ref[:, b, :] = acc[b * tms:(b + 1) * tms].astype(o_ref.dtype)


def _mm_tmajor(x, w, *, nb, tms, tn, act=None, name="mm_tmajor"):
    s_len = x.shape[0]
    kdim, n = w.shape
    nbg = SUBLANES
    assert x.shape[1] == nb * kdim and nb % nbg == 0 and s_len % tms == 0 and n % tn == 0
    return pl.pallas_call(
        functools.partial(_mm_tmajor_body, nbg=nbg, kdim=kdim, tms=tms, act=act),
        grid=(s_len // tms, nb // nbg, n // tn),
        in_specs=[pl.BlockSpec((tms, nbg * kdim), lambda i, g, j: (i, g)),
                  pl.BlockSpec((kdim, tn), lambda i, g, j: (0, j))],
        out_specs=pl.BlockSpec((tms, nbg, tn), lambda i, g, j: (i, g, j)),
        out_shape=jax.ShapeDtypeStruct((s_len, nb, n), F32),
        compiler_params=_cparams(("parallel", "parallel", "arbitrary")),
        name=name,
    )(x, w)


def _halo_specs(tm, width, col_map, m):
    per = tm // GRID_W
    last = m // GRID_W - 1
    return [pl.BlockSpec((tm, width), lambda i, *j: (i, col_map(*j))),
            pl.BlockSpec((GRID_W, width), lambda i, *j: (jnp.maximum(i * per - 1, 0), col_map(*j))),
            pl.BlockSpec((GRID_W, width), lambda i, *j: (jnp.minimum((i + 1) * per, last), col_map(*j)))]


def _mix_body(x_ref, up_ref, dn_ref, g_ref, sc_ref, sh_ref, mu_ref, *out_refs, tm, nbc, bpb, d):
    i = pl.program_id(0)
    g, sc, sh = g_ref[...], sc_ref[0], sh_ref[0]
    h = _rms_mod(x_ref[...].astype(F32), g, sc, sh)
    row = lax.broadcasted_iota(jnp.int32, (tm, 1), 0)

    def emit(lo, hi, shifted):
        hseg = h[:, lo:hi]
        xx = shifted - hseg
        for n, o_ref in enumerate(out_refs):
            o_ref[:, lo:hi] = (hseg + xx * mu_ref[n:n + 1, lo:hi]).astype(o_ref.dtype)

    @pl.when(i < nbc)
    def _():
        half = d // 2
        emit(0, half, jnp.where(row == 0, 0.0, pltpu.roll(h[:, :half], 1, axis=0)))
        emit(half, d, jnp.where(row == tm - 1, 0.0, pltpu.roll(h[:, half:], tm - 1, axis=0)))

    @pl.when(i >= nbc)
    def _():
        jb = (i - nbc) % bpb
        q = d // 4
        col = row % GRID_W
        emit(0, q, jnp.where(col == 0, 0.0, pltpu.roll(h[:, :q], 1, axis=0)))
        emit(q, 2 * q, jnp.where(col == GRID_W - 1, 0.0, pltpu.roll(h[:, q:2 * q], tm - 1, axis=0)))
        hu = _rms_mod(up_ref[...].astype(F32), g, sc, sh)[:, 2 * q:3 * q]
        hu = hu * jnp.where(jb > 0, 1.0, 0.0)
        emit(2 * q, 3 * q, jnp.concatenate([hu, h[:tm - GRID_W, 2 * q:3 * q]], axis=0))
        hd = _rms_mod(dn_ref[...].astype(F32), g, sc, sh)[:, 3 * q:]
        hd = hd * jnp.where(jb < bpb - 1, 1.0, 0.0)
        emit(3 * q, d, jnp.concatenate([h[GRID_W:, 3 * q:], hd], axis=0))


def _rwkv_mix(xs, g, sc, sh, mu, *, dm):
    m, d = xs.shape
    tm = dm.n_ctx
    nbc, bpb = dm.mc // tm, dm.n_lat // tm
    mod_spec = pl.BlockSpec((1, 1, d), lambda i: (dm.mod_index(i, tm), 0, 0))
    tm_spec = pl.BlockSpec((tm, d), lambda i: dm.time_major_block(i))
    bm_spec = pl.BlockSpec((None, tm, d), lambda i: dm.time_major_block(i)[::-1] + (0,))
    return pl.pallas_call(
        functools.partial(_mix_body, tm=tm, nbc=nbc, bpb=bpb, d=d),
        grid=(m // tm,),
        in_specs=_halo_specs(tm, d, lambda: 0, m) + [
            pl.BlockSpec((1, d), lambda i: (0, 0)), mod_spec, mod_spec,
            pl.BlockSpec(mu.shape, lambda i: (0, 0))],
        out_specs=[tm_spec] * 5 + [bm_spec],
        out_shape=[jax.ShapeDtypeStruct((dm.s_len, dm.nb * d), BF16)] * 5
        + [jax.ShapeDtypeStruct((dm.nb, dm.s_len, d), BF16)],
        compiler_params=_cparams(("parallel",)),
        name="rwkv_mix",
    )(xs, xs, xs, g, sc, sh, mu)


def _conv_body(x_ref, up_ref, dn_ref, w_ref, b_ref, o_ref, *, tm, nbc, bpb, n_qcols, q_scale):
    i = pl.program_id(0)
    j = pl.program_id(1)
    row = lax.broadcasted_iota(jnp.int32, (tm, 1), 0)
    x = x_ref[...]
    scale = jnp.where(j < n_qcols, q_scale, 1.0)

    def finish(y):
        y = y + b_ref[...]
        o_ref[...] = (y * jax.nn.sigmoid(y) * scale).astype(o_ref.dtype)

    @pl.when(i < nbc)
    def _():
        prev = jnp.where(row == 0, 0.0, pltpu.roll(x, 1, axis=0))
        nxt = jnp.where(row == tm - 1, 0.0, pltpu.roll(x, tm - 1, axis=0))
        finish(prev * w_ref[3:4, :] + x * w_ref[4:5, :] + nxt * w_ref[5:6, :])

    @pl.when(i >= nbc)
    def _():
        jb = (i - nbc) % bpb
        te = tm + 2 * GRID_W
        ext = jnp.concatenate([up_ref[...] * jnp.where(jb > 0, 1.0, 0.0), x,
                               dn_ref[...] * jnp.where(jb < bpb - 1, 1.0, 0.0)], axis=0)
        col = lax.broadcasted_iota(jnp.int32, (te, 1), 0) % GRID_W
        taps = (jnp.where(col == 0, 0.0, pltpu.roll(ext, 1, axis=0)), ext,
                jnp.where(col == GRID_W - 1, 0.0, pltpu.roll(ext, te - 1, axis=0)))
        y = jnp.zeros((tm, x.shape[1]), F32)
        for di in range(3):
            for dj in range(3):
                y = y + taps[dj][di * GRID_W:di * GRID_W + tm] * w_ref[3 * di + dj:3 * di + dj + 1, :]
        finish(y)


def _mlstm_conv(u, conv_w, conv_b, *, dm, width, tc, q_scale):
    m = u.shape[0]
    tm = dm.n_ctx
    nbc, bpb = dm.mc // tm, dm.n_lat // tm
    w9 = conv_w.reshape(9, width)
    return pl.pallas_call(
        functools.partial(_conv_body, tm=tm, nbc=nbc, bpb=bpb, n_qcols=width // 2 // tc,
                          q_scale=q_scale),
        grid=(m // tm, width // tc),
        in_specs=_halo_specs(tm, tc, lambda j: j, m) + [
            pl.BlockSpec((9, tc), lambda i, j: (0, j)), pl.BlockSpec((1, tc), lambda i, j: (0, j))],
        out_specs=pl.BlockSpec((tm, tc), lambda i, j: (i, j)),
        out_shape=jax.ShapeDtypeStruct((m, width), F32),
        compiler_params=_cparams(("parallel", "parallel")),
        name="mlstm_conv",
    )(u, u, u, w9, conv_b[None, :])


def _split_parity(ref, t, n):
    x = ref[t]
    hpn = x.shape[1] // (2 * n)
    rows = jnp.concatenate([x[:, hp * 2 * n:(hp + 1) * 2 * n] for hp in range(hpn)], axis=0)
    xt = rows.T
    return xt[:n], xt[n:]


def _rwkv_scan_body(r_ref, k_ref, v_ref, wp_ref, ap_ref, kk_ref, ka_ref, rkp_ref, w0_ref, a0_ref,
                    y_ref, rko_ref, s_ref, sz_ref, gam_s, r_s, v_s, k_s, z_s, b_s, *, tblk, n):
    zdir = pl.program_id(0)
    tb = pl.program_id(2)

    @pl.when(tb == 0)
    def _():
        s_ref[...] = jnp.zeros_like(s_ref)

    bwd = zdir == 1
    t_first = jnp.where(bwd, tblk - 1, 0)
    t_step = jnp.where(bwd, -1, 1)

    def prep(i, gam):
        t = t_first + i * t_step
        r2, k2, v2 = _split_parity(r_ref, t, n), _split_parity(k_ref, t, n), _split_parity(v_ref, t, n)
        wp2, ap2 = _split_parity(wp_ref, t, n), _split_parity(ap_ref, t, n)
        out = []
        for g in range(2):
            k_t = k2[g]
            a = jax.nn.sigmoid(a0_ref[g] + ap2[g])
            gcur = gam[g] * jnp.exp(-jnp.exp(-0.5) * jax.nn.sigmoid(w0_ref[g] + wp2[g]))
            inv = 1.0 / gcur
            kr = k_t * kk_ref[g]
            nrm = jnp.sqrt(jnp.sum(kr * kr, axis=0, keepdims=True))
            kkn = kr / jnp.maximum(nrm, 1e-12)
            z_s[t, g] = -kkn * gam[g]
            b_s[t, g] = kkn * a * inv
            km = k_t * (1.0 + (a - 1.0) * ka_ref[g])
            k_s[t, g] = km * inv
            r_s[t, g] = r2[g] * gcur
            v_s[t, g] = v2[g]
            rko_ref[t, g] = jnp.sum(r2[g] * km * rkp_ref[g], axis=0, keepdims=True)
            out.append(gcur)
        return tuple(out)

    ones = jnp.ones((n, LANES), F32)
    gam_end = lax.fori_loop(0, tblk, prep, (ones, ones), unroll=4)
    for g in range(2):
        gam_s[g] = gam_end[g]

    def row(ref, t, g, kk):
        return jnp.broadcast_to(ref[t, g, pl.ds(kk, 1), :], (n, LANES))

    for g in range(2):
        acc = jnp.zeros((n, LANES), F32)
        for kk in range(n):
            acc = acc + s_ref[g, kk] * row(z_s, t_first, g, kk)
        sz_ref[g] = acc

    def step(i, carry):
        t = t_first + i * t_step
        t_next = jnp.clip(t + t_step, 0, tblk - 1)
        for g in range(2):
            sz = sz_ref[g]
            v_t = v_s[t, g]
            y = jnp.zeros((n, LANES), F32)
            sz_next = jnp.zeros((n, LANES), F32)
            for kk in range(n):
                s_new = s_ref[g, kk] + sz * row(b_s, t, g, kk) + v_t * row(k_s, t, g, kk)
                s_ref[g, kk] = s_new
                y = y + s_new * row(r_s, t, g, kk)
                sz_next = sz_next + s_new * row(z_s, t_next, g, kk)
            y_ref[t, g] = y
            sz_ref[g] = sz_next
        return carry

    lax.fori_loop(0, tblk, step, 0)

    for g in range(2):
        for kk in range(n):
            s_ref[g, kk] = s_ref[g, kk] * jnp.broadcast_to(gam_s[g, pl.ds(kk, 1), :], (n, LANES))


def _rwkv_scan(r, k, v, wp, ap, kk_p, ka_p, rk_p, w0_p, a0_p, *, n_ctx, tblk):
    s_len, nbat, d = r.shape
    n = RWKV_HEAD
    nbg = LANES // (d // (2 * n))
    assert 2 * n == LANES and nbat % nbg == 0 and n_ctx % tblk == 0 and s_len % tblk == 0
    ngrp = nbat // nbg
    nbc, nb = n_ctx // tblk, s_len // tblk

    def tmap(zd, p):
        rev = jnp.where(p < nbc, nbc - 1 - p, nb - 1 - (p - nbc))
        return jnp.where(zd == 0, p, rev)

    shared = pl.BlockSpec((tblk, nbg, d), lambda zd, g, p: (tmap(zd, p), g, 0))
    perdir = pl.BlockSpec((tblk, nbg, d), lambda zd, g, p: (tmap(zd, p), g, zd))
    par = pl.BlockSpec((2, n, LANES), lambda zd, g, p: (0, 0, 0))
    par_dir = pl.BlockSpec((None, 2, n, LANES), lambda zd, g, p: (zd, 0, 0, 0))
    step_buf = pltpu.VMEM((tblk, 2, n, LANES), F32)
    return pl.pallas_call(
        functools.partial(_rwkv_scan_body, tblk=tblk, n=n),
        grid=(2, ngrp, nb),
        in_specs=[shared, shared, shared, perdir, perdir, par, par, par, par_dir, par_dir],
        out_specs=[pl.BlockSpec((None, tblk, 2, n, LANES), lambda zd, g, p: (zd, tmap(zd, p), g, 0, 0)),
                   pl.BlockSpec((None, tblk, 2, 1, LANES), lambda zd, g, p: (zd, tmap(zd, p), g, 0, 0))],
        out_shape=[jax.ShapeDtypeStruct((2, s_len, 2 * ngrp, n, LANES), F32),
                   jax.ShapeDtypeStruct((2, s_len, 2 * ngrp, 1, LANES), F32)],
        scratch_shapes=[pltpu.VMEM((2, n, n, LANES), F32), pltpu.VMEM((2, n, LANES), F32),
                        pltpu.VMEM((2, n, LANES), F32)] + [step_buf] * 5,
        compiler_params=_cparams(("parallel", "parallel", "arbitrary")),
        name="rwkv_scan",
    )(r, k, v, wp, ap, kk_p, ka_p, rk_p, w0_p, a0_p)


def _rwkv_norm_body(y_ref, v_ref, rk_ref, lw_ref, lb_ref, o_ref, *, eps, tblk, n):
    def step(t, carry):
        v2 = _split_parity(v_ref, t, n)
        outs = []
        for g in range(2):
            ys = y_ref[0, t, g] + y_ref[1, t, g]
            mu = jnp.mean(ys, axis=0, keepdims=True)
            var = jnp.mean(jnp.square(ys - mu), axis=0, keepdims=True)
            yn = (ys - mu) * lax.rsqrt(var + eps) * lw_ref[g] + lb_ref[g]
            outs.append(yn + (rk_ref[0, t, g] + rk_ref[1, t, g]) * v2[g])
        rows = jnp.concatenate(outs, axis=0).T
        nbg = o_ref.shape[0]
        o_ref[:, t, :] = jnp.concatenate([rows[hp * nbg:(hp + 1) * nbg] for hp in range(LANES // nbg)],
                                         axis=1)
        return carry

    lax.fori_loop(0, tblk, step, 0, unroll=4)


def _rwkv_norm(y, v, rk, ln_w_p, ln_b_p, *, tblk, eps):
    s_len, nbat, d = v.shape
    n = RWKV_HEAD
    nbg = LANES // (d // (2 * n))
    par = pl.BlockSpec((2, n, LANES), lambda p, g: (0, 0, 0))
    nat = pl.BlockSpec((tblk, nbg, d), lambda p, g: (p, g, 0))
    return pl.pallas_call(
        functools.partial(_rwkv_norm_body, eps=eps, tblk=tblk, n=n),
        grid=(s_len // tblk, nbat // nbg),
        in_specs=[pl.BlockSpec((2, tblk, 2, n, LANES), lambda p, g: (0, p, g, 0, 0)), nat,
                  pl.BlockSpec((2, tblk, 2, 1, LANES), lambda p, g: (0, p, g, 0, 0)), par, par],
        out_specs=pl.BlockSpec((nbg, tblk, d), lambda p, g: (g, p, 0)),
        out_shape=jax.ShapeDtypeStruct((nbat, s_len, d), F32),
        compiler_params=_cparams(("parallel", "parallel")),
        name="rwkv_norm",
    )(y, v, rk, ln_w_p, ln_b_p)


def _mlstm_body(q_ref, k_ref, v_ref, ic_ref, fc_ref, ir_ref, fr_ref, o_ref,
                c_ref, n_ref, m_ref, *, nh, dk, dv):
    zdir = pl.program_id(0)
    p = pl.program_id(2)

    @pl.when(p == 0)
    def _():
        c_ref[...] = jnp.zeros_like(c_ref)
        n_ref[...] = jnp.zeros_like(n_ref)
        m_ref[...] = jnp.zeros_like(m_ref)

    ti = lax.broadcasted_iota(jnp.int32, (CHUNK, CHUNK), 0)
    tj = lax.broadcasted_iota(jnp.int32, (CHUNK, CHUNK), 1)
    mask = (ti - tj) * (1 - 2 * zdir) >= 0
    tri = mask.astype(F32)
    fcol = fc_ref[0]
    icol = ic_ref[0]
    frow = fr_ref[0, 0]
    irow = ir_ref[0, 0]
    hi = lax.Precision.HIGHEST
    bcum_col = jnp.dot(tri, fcol, precision=hi, preferred_element_type=F32)
    bcum_row = lax.dot_general(frow, tri, (((1,), (1,)), ((), ())), precision=hi,
                               preferred_element_type=F32)

    hs = range(nh)
    q32 = [q_ref[:, h * dk:(h + 1) * dk].astype(F32) for h in hs]
    qb = [x.astype(BF16) for x in q32]
    k32 = [k_ref[:, h * dk:(h + 1) * dk].astype(F32) for h in hs]
    vb = [v_ref[:, h * dv:(h + 1) * dv].astype(BF16) for h in hs]
    c_st = [c_ref[h] for h in hs]
    n_st = [n_ref[h] for h in hs]
    m_st = [m_ref[h][:, :1] for h in hs]
    bc = [bcum_col[:, h:h + 1] for h in hs]
    log_d = [jnp.where(mask, bc[h] - bcum_row[h:h + 1, :] + irow[h:h + 1, :], -jnp.inf) for h in hs]
    log_inter = [bc[h] + m_st[h] for h in hs]
    m_t = [jnp.maximum(log_inter[h], jnp.max(log_d[h], axis=-1, keepdims=True)) for h in hs]
    b_end = [jnp.sum(frow[h:h + 1, :], axis=-1, keepdims=True) for h in hs]
    a_col = [b_end[h] - bc[h] + icol[:, h:h + 1] for h in hs]
    m_new = [jnp.maximum(b_end[h] + m_st[h], jnp.max(a_col[h], axis=0, keepdims=True)) for h in hs]
    qk = [lax.dot_general(qb[h], k32[h].astype(BF16), (((1,), (1,)), ((), ())),
                          preferred_element_type=F32) for h in hs]
    qc = [jnp.dot(qb[h], c_st[h].astype(BF16), preferred_element_type=F32) for h in hs]
    s = [qk[h] * jnp.exp(log_d[h] - m_t[h]) for h in hs]
    sv = [jnp.dot(s[h].astype(BF16), vb[h], preferred_element_type=F32) for h in hs]
    wk = [jnp.exp(a_col[h] - m_new[h]) * k32[h] for h in hs]
    kv = [lax.dot_general(wk[h].astype(BF16), vb[h], (((0,), (0,)), ((), ())),
                          preferred_element_type=F32) for h in hs]
    for h in hs:
        w_inter = jnp.exp(log_inter[h] - m_t[h])
        qn = jnp.sum(q32[h] * n_st[h], axis=-1, keepdims=True)
        den = w_inter * qn + jnp.sum(s[h], axis=-1, keepdims=True)
        num = w_inter * qc[h] + sv[h]
        o_ref[0, :, h * dv:(h + 1) * dv] = num / jnp.maximum(jnp.abs(den), jnp.exp(-m_t[h]))
        dec = jnp.exp(b_end[h] + m_st[h] - m_new[h])
        c_ref[h] = dec * c_st[h] + kv[h]
        n_ref[h] = dec * n_st[h] + jnp.sum(wk[h], axis=0, keepdims=True)
        m_ref[h] = jnp.broadcast_to(m_new[h], (1, LANES))


def _mlstm_scan(qk, u, v_col0, ig, lf, *, nb, n_ctx, n_lat):
    m = qk.shape[0]
    nh = MLSTM_HEADS
    dkt = qk.shape[1] // 2
    dk = dkt // nh
    dvt = 2 * dkt
    dv = dvt // nh
    ncc, ncl = n_ctx // CHUNK, n_lat // CHUNK
    nchunks = ncc + ncl
    ctx_blocks = nb * ncc
    assert v_col0 % dvt == 0

    def rmap(zd, bb, p):
        rev = jnp.where(p < ncc, ncc - 1 - p, nchunks - 1 - (p - ncc))
        ch = jnp.where(zd == 0, p, rev)
        return jnp.where(ch < ncc, bb * ncc + ch, ctx_blocks + bb * ncl + (ch - ncc))

    ig_row = jnp.swapaxes(ig.reshape(2, m // CHUNK, CHUNK, nh), 2, 3)
    lf_row = jnp.swapaxes(lf.reshape(2, m // CHUNK, CHUNK, nh), 2, 3)
    col_spec = pl.BlockSpec((1, CHUNK, nh), lambda zd, bb, p: (zd, rmap(zd, bb, p), 0))
    row_spec = pl.BlockSpec((1, 1, nh, CHUNK), lambda zd, bb, p: (zd, rmap(zd, bb, p), 0, 0))
    return pl.pallas_call(
        functools.partial(_mlstm_body, nh=nh, dk=dk, dv=dv),
        grid=(2, nb, nchunks),
        in_specs=[pl.BlockSpec((CHUNK, dkt), lambda zd, bb, p: (rmap(zd, bb, p), 0)),
                  pl.BlockSpec((CHUNK, dkt), lambda zd, bb, p: (rmap(zd, bb, p), 1)),
                  pl.BlockSpec((CHUNK, dvt), lambda zd, bb, p: (rmap(zd, bb, p), v_col0 // dvt)),
                  col_spec, col_spec, row_spec, row_spec],
        out_specs=pl.BlockSpec((1, CHUNK, dvt), lambda zd, bb, p: (zd, rmap(zd, bb, p), 0)),
        out_shape=jax.ShapeDtypeStruct((2, m, dvt), F32),
        scratch_shapes=[pltpu.VMEM((nh, dk, dv), F32), pltpu.VMEM((nh, 1, dk), F32),
                        pltpu.VMEM((nh, 1, LANES), F32)],
        compiler_params=_cparams(("parallel", "parallel", "arbitrary")),
        name="mlstm_scan",
    )(qk, qk, u, ig, lf, ig_row, lf_row)


class _Dims:
    def __init__(self, nb, n_ctx, n_lat, d):
        self.nb, self.n_ctx, self.n_lat, self.d = nb, n_ctx, n_lat, d
        self.mc, self.mx = nb * n_ctx, nb * n_lat
        self.m = self.mc + self.mx
        self.s_len = n_ctx + n_lat
        self.tm = 1024 if (self.mc % 1024 == 0 and n_lat % 1024 == 0) else n_ctx
        self.tm_half = max(self.tm // 2, n_ctx) if self.tm > n_ctx else self.tm
        assert self.mc % self.tm == 0 and n_lat % self.tm == 0 and n_ctx % GRID_W == 0

    def mod_index(self, i, tm):
        nbc = self.mc // tm
        bpb = self.n_lat // tm
        return jnp.where(i < nbc, self.nb, (i - nbc) // bpb)

    def time_major_block(self, i):
        nbc = self.mc // self.n_ctx
        bpb = self.n_lat // self.n_ctx
        return (jnp.where(i < nbc, 0, 1 + (i - nbc) % bpb), jnp.where(i < nbc, i, (i - nbc) // bpb))

    def time_view(self, a):
        return a.reshape(self.s_len, self.nb, -1)

    def chan_to_scan(self, p):
        lead = p.shape[:-1]
        hpn = p.shape[-1] // (2 * RWKV_HEAD)
        pt = jnp.moveaxis(p.reshape(lead + (hpn, 2, RWKV_HEAD)), -3, -1)
        pt = jnp.broadcast_to(pt[..., None], lead + (2, RWKV_HEAD, hpn, LANES // hpn))
        return pt.reshape(lead + (2, RWKV_HEAD, LANES))


def _pad_to(a, axis, mult):
    pad = -a.shape[axis] % mult
    if pad == 0:
        return a
    widths = [(0, 0)] * a.ndim
    widths[axis] = (0, pad)
    return jnp.pad(a, widths)


def _rwkv_layer(dm, xs, g1, sc1, sh1, p, v_first):
    m, d, tm = dm.m, dm.d, dm.tm
    xr, xw, xk, xv, xa, xg = _rwkv_mix(xs, g1, sc1, sh1, p['mu'], dm=dm)
    mm = functools.partial(_matmul, m=m, tm=tm)
    mt = functools.partial(_mm_tmajor, nb=dm.nb, tms=2 * GRID_W)
    rows = lambda a: a.reshape(m, -1)
    r = mt(xr, p['w_r'], tn=512, name="mm_r")
    k = mt(xk, p['w_k'], tn=512, name="mm_k")
    v = rows(mt(xv, p['w_v'], tn=512, name="mm_v"))
    row_spec = pl.BlockSpec((tm, 512), lambda i, j, k: (i, j))
    vec_spec = pl.BlockSpec((1, 512), lambda i, j, k: (0, j))
    if v_first is not None:
        lv = rows(mt(xv, p['v1'], tn=LANES, name="mm_v1"))
        v = mm(lv, p['v2'], kdim=LANES, tn=512, name="mm_vres",
               epi=(_epi_vres, [v, v_first, p['v0']], [row_spec, row_spec, vec_spec]))
    lw = rows(mt(xw, p['w1'], tn=2 * LANES, act=jnp.tanh, name="mm_w1"))
    la = rows(mt(xa, p['a1'], tn=2 * LANES, name="mm_a1"))
    lg = mm(xg.reshape(m, d), p['g1'], kdim=d, tn=p['g1'].shape[1], out_dtype=BF16, name="mm_g1",
            epi=(jax.nn.sigmoid, [], []))
    g = mm(lg, p['g2'], kdim=p['g2'].shape[0], tn=512, name="mm_g2")
    wp = mm(lw, p['w2'], kdim=2 * LANES, tn=512, name="mm_w2")
    ap = mm(la, p['a2'], kdim=2 * LANES, tn=512, name="mm_a2")
    cs, hv = dm.chan_to_scan, dm.time_view
    y, rk = _rwkv_scan(hv(r), hv(k), hv(v), hv(wp), hv(ap),
                       cs(p['k_k']), cs(p['k_a']), cs(p['r_k']), cs(p['w0']), cs(p['a0']),
                       n_ctx=dm.n_ctx, tblk=32)
    pre = _rwkv_norm(y, hv(v), rk, cs(p['ln_w']), cs(p['ln_b']), tblk=32, eps=RWKV_HEAD * 1e-5)
    return pre, g, v


def _mlstm_layer(dm, xs, g1, sc1, sh1, p):
    m, d, tm = dm.m, dm.d, dm.tm
    nh = MLSTM_HEADS
    mod_spec = pl.BlockSpec((1, 1, d), lambda i, j, k: (dm.mod_index(i, tm), 0, 0))
    u = _matmul(None, p['w_in'], m=m, kdim=d, tm=tm, tn=MLSTM_IN_TN, name="mm_mlstm_in",
                pro=(_pro_normmod, [xs, g1, sc1, sh1],
                     [pl.BlockSpec((tm, d), lambda i, j, k: (i, 0)),
                      pl.BlockSpec((1, d), lambda i, j, k: (0, 0)), mod_spec, mod_spec]))
    qk = _mlstm_conv(u, p['conv_w'], p['conv_b'], dm=dm, width=d, tc=min(512, d // 2),
                     q_scale=float(d // 2 // nh) ** -0.5)
    gates = u[:, 3 * d:3 * d + 4 * nh].reshape(m, 2, 2, nh) + p['b_gate']
    gates = GATE_CAP * jnp.tanh(gates / GATE_CAP)
    ig = jnp.moveaxis(gates[:, :, 0, :], 1, 0)
    lf = jnp.moveaxis(jax.nn.log_sigmoid(gates[:, :, 1, :]), 1, 0)
    hs = _mlstm_scan(qk, u, d, ig, lf, nb=dm.nb, n_ctx=dm.n_ctx, n_lat=dm.n_lat)
    return hs, u


def kernel(x, c, ctx, c_ctx, mod_w, mod_b, norm_g, final_g, rwkv_mu, rwkv_w_r, rwkv_w_k, rwkv_w_v, rwkv_w_o, rwkv_w0, rwkv_w1, rwkv_w2, rwkv_a0, rwkv_a1, rwkv_a2, rwkv_g1, rwkv_g2, rwkv_k_k, rwkv_k_a, rwkv_r_k, rwkv_ln_w, rwkv_ln_b, rwkv_v0, rwkv_v1, rwkv_v2, mlstm_w_in, mlstm_b_gate, mlstm_conv_w, mlstm_conv_b, mlstm_norm_w, mlstm_w_out, ffn_w_in, ffn_w_out):
    nb, n_lat, d = x.shape
    n_ctx = ctx.shape[1]
    depth = mod_w.shape[0]
    d_ff = ffn_w_out.shape[1]
    dm = _Dims(nb, n_ctx, n_lat, d)
    m, tm, tmh = dm.m, dm.tm, dm.tm_half
    bf = lambda a: a.astype(BF16)

    cond = jax.nn.silu(jnp.concatenate([c, c_ctx[None, :]], axis=0))
    rows = cond.shape[0]
    cond = _pad_to(cond, 0, 2 * SUBLANES)
    mod_w2 = mod_w.reshape(depth * d, 6 * d)
    mods = []
    for i in range(depth):
        mo = _matmul(cond, mod_w2, m=cond.shape[0], kdim=d, tm=cond.shape[0], tn=512,
                     w_row0=i * d, name="mm_mod") + mod_b[i]
        mods.append(mo[:rows].reshape(rows, 6, 1, d))

    def gate_res(res, gate, tm_, tn_=512):
        return (_epi_gate_res, [res, gate],
                [pl.BlockSpec((tm_, tn_), lambda i, j, k: (i, j)),
                 pl.BlockSpec((1, 1, tn_), lambda i, j, k: (dm.mod_index(i, tm_), 0, j))])

    xs = jnp.concatenate([ctx.reshape(dm.mc, d), x.reshape(dm.mx, d)], axis=0)
    v_first = None
    for i in range(depth):
        j = i // 2
        sh1, sc1, gt1, sh2, sc2, gt2 = [mods[i][:, n] for n in range(6)]
        g1 = norm_g[i, 0][None, :]
        g2 = norm_g[i, 1][None, :]
        full_k = lambda tm_: pl.BlockSpec((tm_, d), lambda i, j, k: (i, 0))
        skip = (lambda tm_: dm.mc // tm_) if i == depth - 1 else (lambda tm_: 0)
        if i % 2 == 0:
            lora = lambda a: bf(jnp.concatenate([_pad_to(a[0], 1, LANES), _pad_to(a[1], 1, LANES)], axis=1))

            def lora_up(a):
                ap_ = _pad_to(a, 1, LANES)
                zero = jnp.zeros_like(ap_[0])
                return bf(jnp.concatenate([jnp.concatenate([ap_[0], zero], axis=1),
                                           jnp.concatenate([zero, ap_[1]], axis=1)], axis=0))

            p = {'mu': rwkv_mu[j], 'w_r': bf(rwkv_w_r[j]), 'w_k': bf(rwkv_w_k[j]), 'w_v': bf(rwkv_w_v[j]),
                 'w0': rwkv_w0[j], 'w1': lora(rwkv_w1[j]), 'w2': lora_up(rwkv_w2[j]),
                 'a0': rwkv_a0[j], 'a1': lora(rwkv_a1[j]), 'a2': lora_up(rwkv_a2[j]),
                 'g1': bf(rwkv_g1[j]), 'g2': bf(rwkv_g2[j]),
                 'k_k': rwkv_k_k[j], 'k_a': rwkv_k_a[j], 'r_k': rwkv_r_k[j].reshape(-1),
                 'ln_w': rwkv_ln_w[j], 'ln_b': rwkv_ln_b[j]}
            if j > 0:
                p['v0'] = rwkv_v0[j - 1][None, :]
                p['v1'] = bf(_pad_to(rwkv_v1[j - 1], 1, LANES))
                p['v2'] = bf(_pad_to(rwkv_v2[j - 1], 0, LANES))
            pre, g, v_cur = _rwkv_layer(dm, xs, g1, sc1, sh1, p, v_first if j > 0 else None)
            if j == 0:
                v_first = v_cur
            tmo = n_ctx
            tm_spec = pl.BlockSpec((None, tmo, d), lambda i, j, k: dm.time_major_block(i)[::-1] + (0,))
            xs = _matmul(None, bf(rwkv_w_o[j]), m=m, kdim=d, tm=tmo, tn=d, name="mm_rwkv_out",
                         row_block0=skip(tmo),
                         pro=(_pro_mulg, [pre, g.reshape(nb, dm.s_len, d)],
                              [tm_spec, tm_spec]),
                         epi=gate_res(xs, gt1, tmo, d))
        else:
            p = {'w_in': bf(_pad_to(mlstm_w_in[j], 1, MLSTM_IN_TN)),
                 'b_gate': mlstm_b_gate[j], 'conv_w': mlstm_conv_w[j], 'conv_b': mlstm_conv_b[j]}
            hs, u = _mlstm_layer(dm, xs, g1, sc1, sh1, p)
            xs = _matmul(None, bf(mlstm_w_out[j]), m=m, kdim=d, tm=tmh, tn=512, name="mm_mlstm_out",
                         row_block0=skip(tmh),
                         pro=(_pro_mlstm_read, [hs, u, mlstm_norm_w[j][None, :]],
                              [pl.BlockSpec((2, tmh, d), lambda i, j, k: (0, i, 0)),
                               pl.BlockSpec((tmh, d), lambda i, j, k: (i, 2)),
                               pl.BlockSpec((1, d), lambda i, j, k: (0, 0))]),
                         epi=gate_res(xs, gt1, tmh))
        mod_spec = pl.BlockSpec((1, 1, d), lambda i, j, k: (dm.mod_index(i, tm), 0, 0))
        hid = _matmul(None, bf(ffn_w_in[i]), m=m, kdim=d, tm=tm, tn=512, out_dtype=BF16,
                      n_out=d_ff, w2_col0=d_ff, name="mm_ffn_in", row_block0=skip(tm),
                      pro=(_pro_normmod, [xs, g2, sc2, sh2],
                           [full_k(tm), pl.BlockSpec((1, d), lambda i, j, k: (0, 0)), mod_spec, mod_spec]),
                      epi=(_epi_swiglu, [], []))
        xs = _matmul(hid, bf(ffn_w_out[i]), m=m, kdim=d_ff, tm=tm, tn=512, row_block0=skip(tm),
                     name="mm_ffn_out", epi=gate_res(xs, gt2, tm))
    out = _rmsnorm_rows(xs, final_g[None, :], tm=tm, row_block0=dm.mc // tm)
    return out.reshape(nb, n_lat, d)
```

```python
import functools

import jax
import jax.numpy as jnp
from jax import lax
from jax.experimental import pallas as pl
from jax.experimental.pallas import tpu as pltpu

F32 = jnp.float32
BF16 = jnp.bfloat16

GRID_W = 64
NORM_EPS = 1e-6
RWKV_HEAD = 64
MLSTM_HEADS = 8
CHUNK = 64
GATE_CAP = 15.0
LANES = 128
SUBLANES = 8
VMEM_LIMIT = 56 * 1024 * 1024
MLSTM_IN_TN = 1280


def _cparams(sem):
    return pltpu.CompilerParams(dimension_semantics=sem, vmem_limit_bytes=VMEM_LIMIT)


def _rms_mod(x, g, sc, sh):
    ms = jnp.mean(x * x, axis=-1, keepdims=True)
    return x * lax.rsqrt(ms + NORM_EPS) * g * (1.0 + sc) + sh


def _head_ln_lanes(y, nheads, eps):
    hd = y.shape[-1] // nheads
    out = []
    for h in range(nheads):
        seg = y[:, h * hd:(h + 1) * hd]
        mu = jnp.mean(seg, axis=-1, keepdims=True)
        var = jnp.mean(jnp.square(seg - mu), axis=-1, keepdims=True)
        out.append((seg - mu) * lax.rsqrt(var + eps))
    return jnp.concatenate(out, axis=-1)


def _pro_normmod(x_ref, g_ref, sc_ref, sh_ref):
    return _rms_mod(x_ref[...].astype(F32), g_ref[...], sc_ref[0], sh_ref[0]).astype(BF16)


def _pro_mulg(x_ref, g_ref):
    return (x_ref[...] * g_ref[...]).astype(BF16)


def _pro_mlstm_read(h_ref, o_ref, nw_ref):
    hn = _head_ln_lanes(h_ref[0] + h_ref[1], MLSTM_HEADS, NORM_EPS) * nw_ref[...]
    return (hn * jax.nn.sigmoid(o_ref[...])).astype(BF16)


def _epi_swiglu(acc, acc2):
    return acc * jax.nn.sigmoid(acc) * acc2


def _epi_gate_res(acc, res_ref, gate_ref):
    return res_ref[...] + gate_ref[0] * acc


def _epi_vres(acc, v_ref, vf_ref, v0_ref):
    v = v_ref[...]
    return v + (vf_ref[...] - v) * jax.nn.sigmoid(v0_ref[...] + acc)


def _mm_body(*refs, nk, n_pro, pro_fn, n_epi, epi_fn, dual):
    it = iter(refs)
    pro_refs = [next(it) for _ in range(n_pro)]
    w_ref = next(it)
    w2_ref = next(it) if dual else None
    epi_refs = [next(it) for _ in range(n_epi)]
    o_ref = next(it)
    acc_ref = next(it)
    acc2_ref = next(it) if dual else None
    xb_ref = next(it) if pro_fn is not None else None

    j = pl.program_id(1)
    k = pl.program_id(2)

    if pro_fn is not None:
        @pl.when(j == 0)
        def _():
            xb_ref[...] = pro_fn(*pro_refs)
        xb = xb_ref[...]
    else:
        xb = pro_refs[0][...].astype(BF16)

    @pl.when(k == 0)
    def _():
        acc_ref[...] = jnp.zeros_like(acc_ref)
        if dual:
            acc2_ref[...] = jnp.zeros_like(acc2_ref)

    acc_ref[...] += jnp.dot(xb, w_ref[...].astype(BF16), preferred_element_type=F32)
    if dual:
        acc2_ref[...] += jnp.dot(xb, w2_ref[...].astype(BF16), preferred_element_type=F32)

    @pl.when(k == nk - 1)
    def _():
        acc = acc_ref[...]
        if dual:
            acc = epi_fn(acc, acc2_ref[...], *epi_refs)
        elif epi_fn is not None:
            acc = epi_fn(acc, *epi_refs)
        o_ref[...] = acc.astype(o_ref.dtype)


def _matmul(x, w, *, m, kdim, tm, tn, tk=None, out_dtype=F32, n_out=None, w_col0=0, w2_col0=None,
            w_row0=0, x_col0=0, row_block0=0, pro=None, epi=None, name="mm"):
    n_out = w.shape[1] if n_out is None else n_out
    tk = kdim if tk is None else tk
    dual = w2_col0 is not None
    assert m % tm == 0 and n_out % tn == 0 and kdim % tk == 0
    assert w_col0 % tn == 0 and w_row0 % tk == 0 and x_col0 % tk == 0
    nk = kdim // tk
    c0, r0, xc0 = w_col0 // tn, w_row0 // tk, x_col0 // tk
    pro_fn, pro_args, pro_specs = pro if pro is not None else (None, [x], [
        pl.BlockSpec((tm, tk), lambda i, j, k: (i, k + xc0))])
    if pro_fn is not None:
        assert nk == 1
    epi_fn, epi_args, epi_specs = epi if epi is not None else (None, [], [])
    in_specs = list(pro_specs) + [pl.BlockSpec((tk, tn), lambda i, j, k: (k + r0, j + c0))]
    args = list(pro_args) + [w]
    if dual:
        assert w2_col0 % tn == 0
        c2 = w2_col0 // tn
        in_specs.append(pl.BlockSpec((tk, tn), lambda i, j, k: (k + r0, j + c2)))
        args.append(w)
    in_specs += list(epi_specs)
    args += list(epi_args)
    scratch = [pltpu.VMEM((tm, tn), F32)]
    if dual:
        scratch.append(pltpu.VMEM((tm, tn), F32))
    if pro_fn is not None:
        scratch.append(pltpu.VMEM((tm, tk), BF16))
    out_spec = pl.BlockSpec((tm, tn), lambda i, j, k: (i, j))
    if row_block0:
        shift = lambda sp: pl.BlockSpec(sp.block_shape,
                                        lambda i, j, k: sp.index_map(i + row_block0, j, k))
        in_specs = [shift(sp) for sp in in_specs]
        out_spec = shift(out_spec)
    return pl.pallas_call(
        functools.partial(_mm_body, nk=nk, n_pro=len(pro_args), pro_fn=pro_fn,
                          n_epi=len(epi_args), epi_fn=epi_fn, dual=dual),
        grid=(m // tm - row_block0, n_out // tn, nk),
        in_specs=in_specs,
        out_specs=out_spec,
        out_shape=jax.ShapeDtypeStruct((m, n_out), out_dtype),
        scratch_shapes=scratch,
        compiler_params=_cparams(("parallel", "arbitrary", "arbitrary")),
        name=name,
    )(*args)


def _rmsnorm_body(x_ref, g_ref, o_ref):
    x = x_ref[...]
    ms = jnp.mean(x * x, axis=-1, keepdims=True)
    o_ref[...] = x * lax.rsqrt(ms + NORM_EPS) * g_ref[...]


def _rmsnorm_rows(x, g, *, tm, row_block0):
    m, d = x.shape
    nblk = m // tm - row_block0
    return pl.pallas_call(
        _rmsnorm_body,
        grid=(nblk,),
        in_specs=[pl.BlockSpec((tm, d), lambda i: (i + row_block0, 0)),
                  pl.BlockSpec((1, d), lambda i: (0, 0))],
        out_specs=pl.BlockSpec((tm, d), lambda i: (i, 0)),
        out_shape=jax.ShapeDtypeStruct((nblk * tm, d), x.dtype),
        compiler_params=_cparams(("parallel",)),
        name="final_norm",
    )(x, g)


def _mm_tmajor_body(x_ref, w_ref, o_ref, *, nbg, kdim, tms, act):
    xb = jnp.concatenate([x_ref[:, b * kdim:(b + 1) * kdim] for b in range(nbg)], axis=0)
    acc = jnp.dot(xb, w_ref[...], preferred_element_type=F32)
    if act is not None:
        acc = act(acc)
    for b in range(nbg):
        o_ref[:, b, :] = acc[b * tms:(b + 1) * tms].astype(o_ref.dtype)


def _mm_tmajor(x, w, *, nb, tms, tn, act=None, name="mm_tmajor"):
    s_len = x.shape[0]
    kdim, n = w.shape
    nbg = SUBLANES
    assert x.shape[1] == nb * kdim and nb % nbg == 0 and s_len % tms == 0 and n % tn == 0
    return pl.pallas_call(
        functools.partial(_mm_tmajor_body, nbg=nbg, kdim=kdim, tms=tms, act=act),
        grid=(s_len // tms, nb // nbg, n // tn),
        in_specs=[pl.BlockSpec((tms, nbg * kdim), lambda i, g, j: (i, g)),
                  pl.BlockSpec((kdim, tn), lambda i, g, j: (0, j))],
        out_specs=pl.BlockSpec((tms, nbg, tn), lambda i, g, j: (i, g, j)),
        out_shape=jax.ShapeDtypeStruct((s_len, nb, n), F32),
        compiler_params=_cparams(("parallel", "parallel", "arbitrary")),
        name=name,
    )(x, w)


def _halo_specs(tm, width, col_map, m):
    per = tm // GRID_W
    last = m // GRID_W - 1
    return [pl.BlockSpec((tm, width), lambda i, *j: (i, col_map(*j))),
            pl.BlockSpec((GRID_W, width), lambda i, *j: (jnp.maximum(i * per - 1, 0), col_map(*j))),
            pl.BlockSpec((GRID_W, width), lambda i, *j: (jnp.minimum((i + 1) * per, last), col_map(*j)))]


def _mix_body(x_ref, up_ref, dn_ref, g_ref, sc_ref, sh_ref, mu_ref, *out_refs, tm, nbc, bpb, d):
    i = pl.program_id(0)
    g, sc, sh = g_ref[...], sc_ref[0], sh_ref[0]
    h = _rms_mod(x_ref[...].astype(F32), g, sc, sh)
    row = lax.broadcasted_iota(jnp.int32, (tm, 1), 0)

    def emit(lo, hi, shifted):
        hseg = h[:, lo:hi]
        xx = shifted - hseg
        for n, o_ref in enumerate(out_refs):
            o_ref[:, lo:hi] = (hseg + xx * mu_ref[n:n + 1, lo:hi]).astype(o_ref.dtype)

    @pl.when(i < nbc)
    def _():
        half = d // 2
        emit(0, half, jnp.where(row == 0, 0.0, pltpu.roll(h[:, :half], 1, axis=0)))
        emit(half, d, jnp.where(row == tm - 1, 0.0, pltpu.roll(h[:, half:], tm - 1, axis=0)))

    @pl.when(i >= nbc)
    def _():
        jb = (i - nbc) % bpb
        q = d // 4
        col = row % GRID_W
        emit(0, q, jnp.where(col == 0, 0.0, pltpu.roll(h[:, :q], 1, axis=0)))
        emit(q, 2 * q, jnp.where(col == GRID_W - 1, 0.0, pltpu.roll(h[:, q:2 * q], tm - 1, axis=0)))
        hu = _rms_mod(up_ref[...].astype(F32), g, sc, sh)[:, 2 * q:3 * q]
        hu = hu * jnp.where(jb > 0, 1.0, 0.0)
        emit(2 * q, 3 * q, jnp.concatenate([hu, h[:tm - GRID_W, 2 * q:3 * q]], axis=0))
        hd = _rms_mod(dn_ref[...].astype(F32), g, sc, sh)[:, 3 * q:]
        hd = hd * jnp.where(jb < bpb - 1, 1.0, 0.0)
        emit(3 * q, d, jnp.concatenate([h[GRID_W:, 3 * q:], hd], axis=0))


def _rwkv_mix(xs, g, sc, sh, mu, *, dm):
    m, d = xs.shape
    tm = dm.n_ctx
    nbc, bpb = dm.mc // tm, dm.n_lat // tm
    mod_spec = pl.BlockSpec((1, 1, d), lambda i: (dm.mod_index(i, tm), 0, 0))
    tm_spec = pl.BlockSpec((tm, d), lambda i: dm.time_major_block(i))
    bm_spec = pl.BlockSpec((None, tm, d), lambda i: dm.time_major_block(i)[::-1] + (0,))
    return pl.pallas_call(
        functools.partial(_mix_body, tm=tm, nbc=nbc, bpb=bpb, d=d),
        grid=(m // tm,),
        in_specs=_halo_specs(tm, d, lambda: 0, m) + [
            pl.BlockSpec((1, d), lambda i: (0, 0)), mod_spec, mod_spec,
            pl.BlockSpec(mu.shape, lambda i: (0, 0))],
        out_specs=[tm_spec] * 5 + [bm_spec],
        out_shape=[jax.ShapeDtypeStruct((dm.s_len, dm.nb * d), BF16)] * 5
        + [jax.ShapeDtypeStruct((dm.nb, dm.s_len, d), BF16)],
        compiler_params=_cparams(("parallel",)),
        name="rwkv_mix",
    )(xs, xs, xs, g, sc, sh, mu)


def _conv_body(x_ref, up_ref, dn_ref, w_ref, b_ref, o_ref, *, tm, nbc, bpb, n_qcols, q_scale):
    i = pl.program_id(0)
    j = pl.program_id(1)
    row = lax.broadcasted_iota(jnp.int32, (tm, 1), 0)
    x = x_ref[...]
    scale = jnp.where(j < n_qcols, q_scale, 1.0)

    def finish(y):
        y = y + b_ref[...]
        o_ref[...] = (y * jax.nn.sigmoid(y) * scale).astype(o_ref.dtype)

    @pl.when(i < nbc)
    def _():
        prev = jnp.where(row == 0, 0.0, pltpu.roll(x, 1, axis=0))
        nxt = jnp.where(row == tm - 1, 0.0, pltpu.roll(x, tm - 1, axis=0))
        finish(prev * w_ref[3:4, :] + x * w_ref[4:5, :] + nxt * w_ref[5:6, :])

    @pl.when(i >= nbc)
    def _():
        jb = (i - nbc) % bpb
        te = tm + 2 * GRID_W
        ext = jnp.concatenate([up_ref[...] * jnp.where(jb > 0, 1.0, 0.0), x,
                               dn_ref[...] * jnp.where(jb < bpb - 1, 1.0, 0.0)], axis=0)
        col = lax.broadcasted_iota(jnp.int32, (te, 1), 0) % GRID_W
        taps = (jnp.where(col == 0, 0.0, pltpu.roll(ext, 1, axis=0)), ext,
                jnp.where(col == GRID_W - 1, 0.0, pltpu.roll(ext, te - 1, axis=0)))
        y = jnp.zeros((tm, x.shape[1]), F32)
        for di in range(3):
            for dj in range(3):
                y = y + taps[dj][di * GRID_W:di * GRID_W + tm] * w_ref[3 * di + dj:3 * di + dj + 1, :]
        finish(y)


def _mlstm_conv(u, conv_w, conv_b, *, dm, width, tc, q_scale):
    m = u.shape[0]
    tm = dm.n_ctx
    nbc, bpb = dm.mc // tm, dm.n_lat // tm
    w9 = conv_w.reshape(9, width)
    return pl.pallas_call(
        functools.partial(_conv_body, tm=tm, nbc=nbc, bpb=bpb, n_qcols=width // 2 // tc,
                          q_scale=q_scale),
        grid=(m // tm, width // tc),
        in_specs=_halo_specs(tm, tc, lambda j: j, m) + [
            pl.BlockSpec((9, tc), lambda i, j: (0, j)), pl.BlockSpec((1, tc), lambda i, j: (0, j))],
        out_specs=pl.BlockSpec((tm, tc), lambda i, j: (i, j)),
        out_shape=jax.ShapeDtypeStruct((m, width), F32),
        compiler_params=_cparams(("parallel", "parallel")),
        name="mlstm_conv",
    )(u, u, u, w9, conv_b[None, :])


def _split_parity(ref, t, n):
    x = ref[t]
    hpn = x.shape[1] // (2 * n)
    rows = jnp.concatenate([x[:, hp * 2 * n:(hp + 1) * 2 * n] for hp in range(hpn)], axis=0)
    xt = rows.T
    return xt[:n], xt[n:]


def _rwkv_scan_body(r_ref, k_ref, v_ref, wp_ref, ap_ref, kk_ref, ka_ref, rkp_ref, w0_ref, a0_ref,
                    y_ref, rko_ref, s_ref, sz_ref, gam_s, r_s, v_s, k_s, z_s, b_s, *, tblk, n):
    zdir = pl.program_id(0)
    tb = pl.program_id(2)

    @pl.when(tb == 0)
    def _():
        s_ref[...] = jnp.zeros_like(s_ref)

    bwd = zdir == 1
    t_first = jnp.where(bwd, tblk - 1, 0)
    t_step = jnp.where(bwd, -1, 1)

    def prep(i, gam):
        t = t_first + i * t_step
        r2, k2, v2 = _split_parity(r_ref, t, n), _split_parity(k_ref, t, n), _split_parity(v_ref, t, n)
        wp2, ap2 = _split_parity(wp_ref, t, n), _split_parity(ap_ref, t, n)
        out = []
        for g in range(2):
            k_t = k2[g]
            a = jax.nn.sigmoid(a0_ref[g] + ap2[g])
            gcur = gam[g] * jnp.exp(-jnp.exp(-0.5) * jax.nn.sigmoid(w0_ref[g] + wp2[g]))
            inv = 1.0 / gcur
            kr = k_t * kk_ref[g]
            nrm = jnp.sqrt(jnp.sum(kr * kr, axis=0, keepdims=True))
            kkn = kr / jnp.maximum(nrm, 1e-12)
            z_s[t, g] = -kkn * gam[g]
            b_s[t, g] = kkn * a * inv
            km = k_t * (1.0 + (a - 1.0) * ka_ref[g])
            k_s[t, g] = km * inv
            r_s[t, g] = r2[g] * gcur
            v_s[t, g] = v2[g]
            rko_ref[t, g] = jnp.sum(r2[g] * km * rkp_ref[g], axis=0, keepdims=True)
            out.append(gcur)
        return tuple(out)

    ones = jnp.ones((n, LANES), F32)
    gam_end = lax.fori_loop(0, tblk, prep, (ones, ones), unroll=4)
    for g in range(2):
        gam_s[g] = gam_end[g]

    def row(ref, t, g, kk):
        return jnp.broadcast_to(ref[t, g, pl.ds(kk, 1), :], (n, LANES))

    for g in range(2):
        acc = jnp.zeros((n, LANES), F32)
        for kk in range(n):
            acc = acc + s_ref[g, kk] * row(z_s, t_first, g, kk)
        sz_ref[g] = acc

    def step(i, carry):
        t = t_first + i * t_step
        t_next = jnp.clip(t + t_step, 0, tblk - 1)
        for g in range(2):
            sz = sz_ref[g]
            v_t = v_s[t, g]
            y = jnp.zeros((n, LANES), F32)
            sz_next = jnp.zeros((n, LANES), F32)
            for kk in range(n):
                s_new = s_ref[g, kk] + sz * row(b_s, t, g, kk) + v_t * row(k_s, t, g, kk)
                s_ref[g, kk] = s_new
                y = y + s_new * row(r_s, t, g, kk)
                sz_next = sz_next + s_new * row(z_s, t_next, g, kk)
            y_ref[t, g] = y
            sz_ref[g] = sz_next
        return carry

    lax.fori_loop(0, tblk, step, 0)

    for g in range(2):
        for kk in range(n):
            s_ref[g, kk] = s_ref[g, kk] * jnp.broadcast_to(gam_s[g, pl.ds(kk, 1), :], (n, LANES))


def _rwkv_scan(r, k, v, wp, ap, kk_p, ka_p, rk_p, w0_p, a0_p, *, n_ctx, tblk):
    s_len, nbat, d = r.shape
    n = RWKV_HEAD
    nbg = LANES // (d // (2 * n))
    assert 2 * n == LANES and nbat % nbg == 0 and n_ctx % tblk == 0 and s_len % tblk == 0
    ngrp = nbat // nbg
    nbc, nb = n_ctx // tblk, s_len // tblk

    def tmap(zd, p):
        rev = jnp.where(p < nbc, nbc - 1 - p, nb - 1 - (p - nbc))
        return jnp.where(zd == 0, p, rev)

    shared = pl.BlockSpec((tblk, nbg, d), lambda zd, g, p: (tmap(zd, p), g, 0))
    perdir = pl.BlockSpec((tblk, nbg, d), lambda zd, g, p: (tmap(zd, p), g, zd))
    par = pl.BlockSpec((2, n, LANES), lambda zd, g, p: (0, 0, 0))
    par_dir = pl.BlockSpec((None, 2, n, LANES), lambda zd, g, p: (zd, 0, 0, 0))
    step_buf = pltpu.VMEM((tblk, 2, n, LANES), F32)
    return pl.pallas_call(
        functools.partial(_rwkv_scan_body, tblk=tblk, n=n),
        grid=(2, ngrp, nb),
        in_specs=[shared, shared, shared, perdir, perdir, par, par, par, par_dir, par_dir],
        out_specs=[pl.BlockSpec((None, tblk, 2, n, LANES), lambda zd, g, p: (zd, tmap(zd, p), g, 0, 0)),
                   pl.BlockSpec((None, tblk, 2, 1, LANES), lambda zd, g, p: (zd, tmap(zd, p), g, 0, 0))],
        out_shape=[jax.ShapeDtypeStruct((2, s_len, 2 * ngrp, n, LANES), F32),
                   jax.ShapeDtypeStruct((2, s_len, 2 * ngrp, 1, LANES), F32)],
        scratch_shapes=[pltpu.VMEM((2, n, n, LANES), F32), pltpu.VMEM((2, n, LANES), F32),
                        pltpu.VMEM((2, n, LANES), F32)] + [step_buf] * 5,
        compiler_params=_cparams(("parallel", "parallel", "arbitrary")),
        name="rwkv_scan",
    )(r, k, v, wp, ap, kk_p, ka_p, rk_p, w0_p, a0_p)


def _rwkv_norm_body(y_ref, v_ref, rk_ref, lw_ref, lb_ref, o_ref, *, eps, tblk, n):
    def step(t, carry):
        v2 = _split_parity(v_ref, t, n)
        outs = []
        for g in range(2):
            ys = y_ref[0, t, g] + y_ref[1, t, g]
            mu = jnp.mean(ys, axis=0, keepdims=True)
            var = jnp.mean(jnp.square(ys - mu), axis=0, keepdims=True)
            yn = (ys - mu) * lax.rsqrt(var + eps) * lw_ref[g] + lb_ref[g]
            outs.append(yn + (rk_ref[0, t, g] + rk_ref[1, t, g]) * v2[g])
        rows = jnp.concatenate(outs, axis=0).T
        nbg = o_ref.shape[0]
        o_ref[:, t, :] = jnp.concatenate([rows[hp * nbg:(hp + 1) * nbg] for hp in range(LANES // nbg)],
                                         axis=1)
        return carry

    lax.fori_loop(0, tblk, step, 0, unroll=4)


def _rwkv_norm(y, v, rk, ln_w_p, ln_b_p, *, tblk, eps):
    s_len, nbat, d = v.shape
    n = RWKV_HEAD
    nbg = LANES // (d // (2 * n))
    par = pl.BlockSpec((2, n, LANES), lambda p, g: (0, 0, 0))
    nat = pl.BlockSpec((tblk, nbg, d), lambda p, g: (p, g, 0))
    return pl.pallas_call(
        functools.partial(_rwkv_norm_body, eps=eps, tblk=tblk, n=n),
        grid=(s_len // tblk, nbat // nbg),
        in_specs=[pl.BlockSpec((2, tblk, 2, n, LANES), lambda p, g: (0, p, g, 0, 0)), nat,
                  pl.BlockSpec((2, tblk, 2, 1, LANES), lambda p, g: (0, p, g, 0, 0)), par, par],
        out_specs=pl.BlockSpec((nbg, tblk, d), lambda p, g: (g, p, 0)),
        out_shape=jax.ShapeDtypeStruct((nbat, s_len, d), F32),
        compiler_params=_cparams(("parallel", "parallel")),
        name="rwkv_norm",
    )(y, v, rk, ln_w_p, ln_b_p)


def _mlstm_body(q_ref, k_ref, v_ref, ic_ref, fc_ref, ir_ref, fr_ref, o_ref,
                c_ref, n_ref, m_ref, *, nh, dk, dv):
    zdir = pl.program_id(0)
    p = pl.program_id(2)

    @pl.when(p == 0)
    def _():
        c_ref[...] = jnp.zeros_like(c_ref)
        n_ref[...] = jnp.zeros_like(n_ref)
        m_ref[...] = jnp.zeros_like(m_ref)

    ti = lax.broadcasted_iota(jnp.int32, (CHUNK, CHUNK), 0)
    tj = lax.broadcasted_iota(jnp.int32, (CHUNK, CHUNK), 1)
    mask = (ti - tj) * (1 - 2 * zdir) >= 0
    tri = mask.astype(F32)
    fcol = fc_ref[0]
    icol = ic_ref[0]
    frow = fr_ref[0, 0]
    irow = ir_ref[0, 0]
    hi = lax.Precision.HIGHEST
    bcum_col = jnp.dot(tri, fcol, precision=hi, preferred_element_type=F32)
    bcum_row = lax.dot_general(frow, tri, (((1,), (1,)), ((), ())), precision=hi,
                               preferred_element_type=F32)

    hs = range(nh)
    q32 = [q_ref[:, h * dk:(h + 1) * dk].astype(F32) for h in hs]
    qb = [x.astype(BF16) for x in q32]
    k32 = [k_ref[:, h * dk:(h + 1) * dk].astype(F32) for h in hs]
    vb = [v_ref[:, h * dv:(h + 1) * dv].astype(BF16) for h in hs]
    c_st = [c_ref[h] for h in hs]
    n_st = [n_ref[h] for h in hs]
    m_st = [m_ref[h][:, :1] for h in hs]
    bc = [bcum_col[:, h:h + 1] for h in hs]
    log_d = [jnp.where(mask, bc[h] - bcum_row[h:h + 1, :] + irow[h:h + 1, :], -jnp.inf) for h in hs]
    log_inter = [bc[h] + m_st[h] for h in hs]
    m_t = [jnp.maximum(log_inter[h], jnp.max(log_d[h], axis=-1, keepdims=True)) for h in hs]
    b_end = [jnp.sum(frow[h:h + 1, :], axis=-1, keepdims=True) for h in hs]
    a_col = [b_end[h] - bc[h] + icol[:, h:h + 1] for h in hs]
    m_new = [jnp.maximum(b_end[h] + m_st[h], jnp.max(a_col[h], axis=0, keepdims=True)) for h in hs]
    qk = [lax.dot_general(qb[h], k32[h].astype(BF16), (((1,), (1,)), ((), ())),
                          preferred_element_type=F32) for h in hs]
    qc = [jnp.dot(qb[h], c_st[h].astype(BF16), preferred_element_type=F32) for h in hs]
    s = [qk[h] * jnp.exp(log_d[h] - m_t[h]) for h in hs]
    sv = [jnp.dot(s[h].astype(BF16), vb[h], preferred_element_type=F32) for h in hs]
    wk = [jnp.exp(a_col[h] - m_new[h]) * k32[h] for h in hs]
    kv = [lax.dot_general(wk[h].astype(BF16), vb[h], (((0,), (0,)), ((), ())),
                          preferred_element_type=F32) for h in hs]
    w_inter = [jnp.exp(log_inter[h] - m_t[h]) for h in hs]
    qn = [jnp.sum(q32[h] * n_st[h], axis=-1, keepdims=True) for h in hs]
    ssum = [jnp.sum(s[h], axis=-1, keepdims=True) for h in hs]
    wsum = [jnp.sum(wk[h], axis=0, keepdims=True) for h in hs]
    dec = [jnp.exp(b_end[h] + m_st[h] - m_new[h]) for h in hs]
    den = [w_inter[h] * qn[h] + ssum[h] for h in hs]
    inv = [1.0 / jnp.maximum(jnp.abs(den[h]), jnp.exp(-m_t[h])) for h in hs]
    for h in hs:
        o_ref[0, :, h * dv:(h + 1) * dv] = (w_inter[h] * qc[h] + sv[h]) * inv[h]
    for h in hs:
        c_ref[h] = dec[h] * c_st[h] + kv[h]
        n_ref[h] = dec[h] * n_st[h] + wsum[h]
        m_ref[h] = jnp.broadcast_to(m_new[h], (1, LANES))


def _mlstm_scan(qk, u, v_col0, ig, lf, *, nb, n_ctx, n_lat):
    m = qk.shape[0]
    nh = MLSTM_HEADS
    dkt = qk.shape[1] // 2
    dk = dkt // nh
    dvt = 2 * dkt
    dv = dvt // nh
    ncc, ncl = n_ctx // CHUNK, n_lat // CHUNK
    nchunks = ncc + ncl
    ctx_blocks = nb * ncc
    assert v_col0 % dvt == 0

    def rmap(zd, bb, p):
        rev = jnp.where(p < ncc, ncc - 1 - p, nchunks - 1 - (p - ncc))
        ch = jnp.where(zd == 0, p, rev)
        return jnp.where(ch < ncc, bb * ncc + ch, ctx_blocks + bb * ncl + (ch - ncc))

    ig_row = jnp.swapaxes(ig.reshape(2, m // CHUNK, CHUNK, nh), 2, 3)
    lf_row = jnp.swapaxes(lf.reshape(2, m // CHUNK, CHUNK, nh), 2, 3)
    col_spec = pl.BlockSpec((1, CHUNK, nh), lambda zd, bb, p: (zd, rmap(zd, bb, p), 0))
    row_spec = pl.BlockSpec((1, 1, nh, CHUNK), lambda zd, bb, p: (zd, rmap(zd, bb, p), 0, 0))
    return pl.pallas_call(
        functools.partial(_mlstm_body, nh=nh, dk=dk, dv=dv),
        grid=(2, nb, nchunks),
        in_specs=[pl.BlockSpec((CHUNK, dkt), lambda zd, bb, p: (rmap(zd, bb, p), 0)),
                  pl.BlockSpec((CHUNK, dkt), lambda zd, bb, p: (rmap(zd, bb, p), 1)),
                  pl.BlockSpec((CHUNK, dvt), lambda zd, bb, p: (rmap(zd, bb, p), v_col0 // dvt)),
                  col_spec, col_spec, row_spec, row_spec],
        out_specs=pl.BlockSpec((1, CHUNK, dvt), lambda zd, bb, p: (zd, rmap(zd, bb, p), 0)),
        out_shape=jax.ShapeDtypeStruct((2, m, dvt), F32),
        scratch_shapes=[pltpu.VMEM((nh, dk, dv), F32), pltpu.VMEM((nh, 1, dk), F32),
                        pltpu.VMEM((nh, 1, LANES), F32)],
        compiler_params=_cparams(("parallel", "parallel", "arbitrary")),
        name="mlstm_scan",
    )(qk, qk, u, ig, lf, ig_row, lf_row)


class _Dims:
    def __init__(self, nb, n_ctx, n_lat, d):
        self.nb, self.n_ctx, self.n_lat, self.d = nb, n_ctx, n_lat, d
        self.mc, self.mx = nb * n_ctx, nb * n_lat
        self.m = self.mc + self.mx
        self.s_len = n_ctx + n_lat
        self.tm = 1024 if (self.mc % 1024 == 0 and n_lat % 1024 == 0) else n_ctx
        self.tm_half = max(self.tm // 2, n_ctx) if self.tm > n_ctx else self.tm
        assert self.mc % self.tm == 0 and n_lat % self.tm == 0 and n_ctx % GRID_W == 0

    def mod_index(self, i, tm):
        nbc = self.mc // tm
        bpb = self.n_lat // tm
        return jnp.where(i < nbc, self.nb, (i - nbc) // bpb)

    def time_major_block(self, i):
        nbc = self.mc // self.n_ctx
        bpb = self.n_lat // self.n_ctx
        return (jnp.where(i < nbc, 0, 1 + (i - nbc) % bpb), jnp.where(i < nbc, i, (i - nbc) // bpb))

    def time_view(self, a):
        return a.reshape(self.s_len, self.nb, -1)

    def chan_to_scan(self, p):
        lead = p.shape[:-1]
        hpn = p.shape[-1] // (2 * RWKV_HEAD)
        pt = jnp.moveaxis(p.reshape(lead + (hpn, 2, RWKV_HEAD)), -3, -1)
        pt = jnp.broadcast_to(pt[..., None], lead + (2, RWKV_HEAD, hpn, LANES // hpn))
        return pt.reshape(lead + (2, RWKV_HEAD, LANES))


def _pad_to(a, axis, mult):
    pad = -a.shape[axis] % mult
    if pad == 0:
        return a
    widths = [(0, 0)] * a.ndim
    widths[axis] = (0, pad)
    return jnp.pad(a, widths)


def _rwkv_layer(dm, xs, g1, sc1, sh1, p, v_first):
    m, d, tm = dm.m, dm.d, dm.tm
    xr, xw, xk, xv, xa, xg = _rwkv_mix(xs, g1, sc1, sh1, p['mu'], dm=dm)
    mm = functools.partial(_matmul, m=m, tm=tm)
    mt = functools.partial(_mm_tmajor, nb=dm.nb, tms=2 * GRID_W)
    rows = lambda a: a.reshape(m, -1)
    r = mt(xr, p['w_r'], tn=512, name="mm_r")
    k = mt(xk, p['w_k'], tn=512, name="mm_k")
    v = rows(mt(xv, p['w_v'], tn=512, name="mm_v"))
    row_spec = pl.BlockSpec((tm, 512), lambda i, j, k: (i, j))
    vec_spec = pl.BlockSpec((1, 512), lambda i, j, k: (0, j))
    if v_first is not None:
        lv = rows(mt(xv, p['v1'], tn=LANES, name="mm_v1"))
        v = mm(lv, p['v2'], kdim=LANES, tn=512, name="mm_vres",
               epi=(_epi_vres, [v, v_first, p['v0']], [row_spec, row_spec, vec_spec]))
    lw = rows(mt(xw, p['w1'], tn=2 * LANES, act=jnp.tanh, name="mm_w1"))
    la = rows(mt(xa, p['a1'], tn=2 * LANES, name="mm_a1"))
    lg = mm(xg.reshape(m, d), p['g1'], kdim=d, tn=p['g1'].shape[1], out_dtype=BF16, name="mm_g1",
            epi=(jax.nn.sigmoid, [], []))
    g = mm(lg, p['g2'], kdim=p['g2'].shape[0], tn=512, name="mm_g2")
    wp = mm(lw, p['w2'], kdim=2 * LANES, tn=512, name="mm_w2")
    ap = mm(la, p['a2'], kdim=2 * LANES, tn=512, name="mm_a2")
    cs, hv = dm.chan_to_scan, dm.time_view
    y, rk = _rwkv_scan(hv(r), hv(k), hv(v), hv(wp), hv(ap),
                       cs(p['k_k']), cs(p['k_a']), cs(p['r_k']), cs(p['w0']), cs(p['a0']),
                       n_ctx=dm.n_ctx, tblk=32)
    pre = _rwkv_norm(y, hv(v), rk, cs(p['ln_w']), cs(p['ln_b']), tblk=32, eps=RWKV_HEAD * 1e-5)
    return pre, g, v


def _mlstm_layer(dm, xs, g1, sc1, sh1, p):
    m, d, tm = dm.m, dm.d, dm.tm
    nh = MLSTM_HEADS
    mod_spec = pl.BlockSpec((1, 1, d), lambda i, j, k: (dm.mod_index(i, tm), 0, 0))
    u = _matmul(None, p['w_in'], m=m, kdim=d, tm=tm, tn=MLSTM_IN_TN, name="mm_mlstm_in",
                pro=(_pro_normmod, [xs, g1, sc1, sh1],
                     [pl.BlockSpec((tm, d), lambda i, j, k: (i, 0)),
                      pl.BlockSpec((1, d), lambda i, j, k: (0, 0)), mod_spec, mod_spec]))
    qk = _mlstm_conv(u, p['conv_w'], p['conv_b'], dm=dm, width=d, tc=min(512, d // 2),
                     q_scale=float(d // 2 // nh) ** -0.5)
    gates = u[:, 3 * d:3 * d + 4 * nh].reshape(m, 2, 2, nh) + p['b_gate']
    gates = GATE_CAP * jnp.tanh(gates / GATE_CAP)
    ig = jnp.moveaxis(gates[:, :, 0, :], 1, 0)
    lf = jnp.moveaxis(jax.nn.log_sigmoid(gates[:, :, 1, :]), 1, 0)
    hs = _mlstm_scan(qk, u, d, ig, lf, nb=dm.nb, n_ctx=dm.n_ctx, n_lat=dm.n_lat)
    return hs, u


def kernel(x, c, ctx, c_ctx, mod_w, mod_b, norm_g, final_g, rwkv_mu, rwkv_w_r, rwkv_w_k, rwkv_w_v, rwkv_w_o, rwkv_w0, rwkv_w1, rwkv_w2, rwkv_a0, rwkv_a1, rwkv_a2, rwkv_g1, rwkv_g2, rwkv_k_k, rwkv_k_a, rwkv_r_k, rwkv_ln_w, rwkv_ln_b, rwkv_v0, rwkv_v1, rwkv_v2, mlstm_w_in, mlstm_b_gate, mlstm_conv_w, mlstm_conv_b, mlstm_norm_w, mlstm_w_out, ffn_w_in, ffn_w_out):
    nb, n_lat, d = x.shape
    n_ctx = ctx.shape[1]
    depth = mod_w.shape[0]
    d_ff = ffn_w_out.shape[1]
    dm = _Dims(nb, n_ctx, n_lat, d)
    m, tm, tmh = dm.m, dm.tm, dm.tm_half
    bf = lambda a: a.astype(BF16)

    cond = jax.nn.silu(jnp.concatenate([c, c_ctx[None, :]], axis=0))
    rows = cond.shape[0]
    cond = _pad_to(cond, 0, 2 * SUBLANES)
    mod_w2 = mod_w.reshape(depth * d, 6 * d)
    mods = []
    for i in range(depth):
        mo = _matmul(cond, mod_w2, m=cond.shape[0], kdim=d, tm=cond.shape[0], tn=512,
                     w_row0=i * d, name="mm_mod") + mod_b[i]
        mods.append(mo[:rows].reshape(rows, 6, 1, d))

    def gate_res(res, gate, tm_, tn_=512):
        return (_epi_gate_res, [res, gate],
                [pl.BlockSpec((tm_, tn_), lambda i, j, k: (i, j)),
                 pl.BlockSpec((1, 1, tn_), lambda i, j, k: (dm.mod_index(i, tm_), 0, j))])

    xs = jnp.concatenate([ctx.reshape(dm.mc, d), x.reshape(dm.mx, d)], axis=0)
    v_first = None
    for i in range(depth):
        j = i // 2
        sh1, sc1, gt1, sh2, sc2, gt2 = [mods[i][:, n] for n in range(6)]
        g1 = norm_g[i, 0][None, :]
        g2 = norm_g[i, 1][None, :]
        full_k = lambda tm_: pl.BlockSpec((tm_, d), lambda i, j, k: (i, 0))
        skip = (lambda tm_: dm.mc // tm_) if i == depth - 1 else (lambda tm_: 0)
        if i % 2 == 0:
            lora = lambda a: bf(jnp.concatenate([_pad_to(a[0], 1, LANES), _pad_to(a[1], 1, LANES)], axis=1))

            def lora_up(a):
                ap_ = _pad_to(a, 1, LANES)
                zero = jnp.zeros_like(ap_[0])
                return bf(jnp.concatenate([jnp.concatenate([ap_[0], zero], axis=1),
                                           jnp.concatenate([zero, ap_[1]], axis=1)], axis=0))

            p = {'mu': rwkv_mu[j], 'w_r': bf(rwkv_w_r[j]), 'w_k': bf(rwkv_w_k[j]), 'w_v': bf(rwkv_w_v[j]),
                 'w0': rwkv_w0[j], 'w1': lora(rwkv_w1[j]), 'w2': lora_up(rwkv_w2[j]),
                 'a0': rwkv_a0[j], 'a1': lora(rwkv_a1[j]), 'a2': lora_up(rwkv_a2[j]),
                 'g1': bf(rwkv_g1[j]), 'g2': bf(rwkv_g2[j]),
                 'k_k': rwkv_k_k[j], 'k_a': rwkv_k_a[j], 'r_k': rwkv_r_k[j].reshape(-1),
                 'ln_w': rwkv_ln_w[j], 'ln_b': rwkv_ln_b[j]}
            if j > 0:
                p['v0'] = rwkv_v0[j - 1][None, :]
                p['v1'] = bf(_pad_to(rwkv_v1[j - 1], 1, LANES))
                p['v2'] = bf(_pad_to(rwkv_v2[j - 1], 0, LANES))
            pre, g, v_cur = _rwkv_layer(dm, xs, g1, sc1, sh1, p, v_first if j > 0 else None)
            if j == 0:
                v_first = v_cur
            tmo = n_ctx
            tm_spec = pl.BlockSpec((None, tmo, d), lambda i, j, k: dm.time_major_block(i)[::-1] + (0,))
            xs = _matmul(None, bf(rwkv_w_o[j]), m=m, kdim=d, tm=tmo, tn=d, name="mm_rwkv_out",
                         row_block0=skip(tmo),
                         pro=(_pro_mulg, [pre, g.reshape(nb, dm.s_len, d)],
                              [tm_spec, tm_spec]),
                         epi=gate_res(xs, gt1, tmo, d))
        else:
            p = {'w_in': bf(_pad_to(mlstm_w_in[j], 1, MLSTM_IN_TN)),
                 'b_gate': mlstm_b_gate[j], 'conv_w': mlstm_conv_w[j], 'conv_b': mlstm_conv_b[j]}
            hs, u = _mlstm_layer(dm, xs, g1, sc1, sh1, p)
            xs = _matmul(None, bf(mlstm_w_out[j]), m=m, kdim=d, tm=tmh, tn=512, name="mm_mlstm_out",
                         row_block0=skip(tmh),
                         pro=(_pro_mlstm_read, [hs, u, mlstm_norm_w[j][None, :]],
                              [pl.BlockSpec((2, tmh, d), lambda i, j, k: (0, i, 0)),
                               pl.BlockSpec((tmh, d), lambda i, j, k: (i, 2)),
                               pl.BlockSpec((1, d), lambda i, j, k: (0, 0))]),
                         epi=gate_res(xs, gt1, tmh))
        mod_spec = pl.BlockSpec((1, 1, d), lambda i, j, k: (dm.mod_index(i, tm), 0, 0))
        hid = _matmul(None, bf(ffn_w_in[i]), m=m, kdim=d, tm=tm, tn=512, out_dtype=BF16,
                      n_out=d_ff, w2_col0=d_ff, name="mm_ffn_in", row_block0=skip(tm),
                      pro=(_pro_normmod, [xs, g2, sc2, sh2],
                           [full_k(tm), pl.BlockSpec((1, d), lambda i, j, k: (0, 0)), mod_spec, mod_spec]),
                      epi=(_epi_swiglu, [], []))
        xs = _matmul(hid, bf(ffn_w_out[i]), m=m, kdim=d_ff, tm=tm, tn=512, row_block0=skip(tm),
                     name="mm_ffn_out", epi=gate_res(xs, gt2, tm))
    out = _rmsnorm_rows(xs, final_g[None, :], tm=tm, row_block0=dm.mc // tm)
    return out.reshape(nb, n_lat, d)
```

```python
import functools

import jax
import jax.numpy as jnp
from jax import lax
from jax.experimental import pallas as pl
from jax.experimental.pallas import tpu as pltpu

F32 = jnp.float32
BF16 = jnp.bfloat16

GRID_W = 64
NORM_EPS = 1e-6
RWKV_HEAD = 64
MLSTM_HEADS = 8
CHUNK = 64
GATE_CAP = 15.0
LANES = 128
SUBLANES = 8
VMEM_LIMIT = 56 * 1024 * 1024
MLSTM_IN_TN = 1280


def _cparams(sem):
    return pltpu.CompilerParams(dimension_semantics=sem, vmem_limit_bytes=VMEM_LIMIT)


def _rms_mod(x, g, sc, sh):
    ms = jnp.mean(x * x, axis=-1, keepdims=True)
    return x * lax.rsqrt(ms + NORM_EPS) * g * (1.0 + sc) + sh


def _head_ln_lanes(y, nheads, eps):
    hd = y.shape[-1] // nheads
    out = []
    for h in range(nheads):
        seg = y[:, h * hd:(h + 1) * hd]
        mu = jnp.mean(seg, axis=-1, keepdims=True)
        var = jnp.mean(jnp.square(seg - mu), axis=-1, keepdims=True)
        out.append((seg - mu) * lax.rsqrt(var + eps))
    return jnp.concatenate(out, axis=-1)


def _pro_normmod(x_ref, g_ref, sc_ref, sh_ref):
    return _rms_mod(x_ref[...].astype(F32), g_ref[...], sc_ref[...], sh_ref[...]).astype(BF16)


def _pro_mulg(x_ref, g_ref):
    return (x_ref[...] * g_ref[...]).astype(BF16)


def _pro_mlstm_read(h_ref, o_ref, nw_ref):
    hn = _head_ln_lanes(h_ref[0] + h_ref[1], MLSTM_HEADS, NORM_EPS) * nw_ref[...]
    return (hn * jax.nn.sigmoid(o_ref[...])).astype(BF16)


def _epi_swiglu(acc, acc2):
    return acc * jax.nn.sigmoid(acc) * acc2


def _epi_gate_res(acc, res_ref, gate_ref):
    return res_ref[...] + gate_ref[...] * acc


def _epi_vres(acc, v_ref, vf_ref, v0_ref):
    v = v_ref[...]
    return v + (vf_ref[...] - v) * jax.nn.sigmoid(v0_ref[...] + acc)


def _mm_body(*refs, nk, n_pro, pro_fn, n_epi, epi_fn, dual):
    it = iter(refs)
    pro_refs = [next(it) for _ in range(n_pro)]
    w_ref = next(it)
    w2_ref = next(it) if dual else None
    epi_refs = [next(it) for _ in range(n_epi)]
    o_ref = next(it)
    acc_ref = next(it)
    acc2_ref = next(it) if dual else None
    xb_ref = next(it) if pro_fn is not None else None

    j = pl.program_id(1)
    k = pl.program_id(2)

    if pro_fn is not None:
        @pl.when(j == 0)
        def _():
            xb_ref[...] = pro_fn(*pro_refs)
        xb = xb_ref[...]
    else:
        xb = pro_refs[0][...].astype(BF16)

    @pl.when(k == 0)
    def _():
        acc_ref[...] = jnp.zeros_like(acc_ref)
        if dual:
            acc2_ref[...] = jnp.zeros_like(acc2_ref)

    acc_ref[...] += jnp.dot(xb, w_ref[...].astype(BF16), preferred_element_type=F32)
    if dual:
        acc2_ref[...] += jnp.dot(xb, w2_ref[...].astype(BF16), preferred_element_type=F32)

    @pl.when(k == nk - 1)
    def _():
        acc = acc_ref[...]
        if dual:
            acc = epi_fn(acc, acc2_ref[...], *epi_refs)
        elif epi_fn is not None:
            acc = epi_fn(acc, *epi_refs)
        o_ref[...] = acc.astype(o_ref.dtype)


def _matmul(x, w, *, m, kdim, tm, tn, tk=None, out_dtype=F32, n_out=None, w_col0=0, w2_col0=None,
            w_row0=0, x_col0=0, row_block0=0, pro=None, epi=None, name="mm"):
    n_out = w.shape[1] if n_out is None else n_out
    tk = kdim if tk is None else tk
    dual = w2_col0 is not None
    assert m % tm == 0 and n_out % tn == 0 and kdim % tk == 0
    assert w_col0 % tn == 0 and w_row0 % tk == 0 and x_col0 % tk == 0
    nk = kdim // tk
    c0, r0, xc0 = w_col0 // tn, w_row0 // tk, x_col0 // tk
    pro_fn, pro_args, pro_specs = pro if pro is not None else (None, [x], [
        pl.BlockSpec((tm, tk), lambda i, j, k: (i, k + xc0))])
    if pro_fn is not None:
        assert nk == 1
    epi_fn, epi_args, epi_specs = epi if epi is not None else (None, [], [])
    in_specs = list(pro_specs) + [pl.BlockSpec((tk, tn), lambda i, j, k: (k + r0, j + c0))]
    args = list(pro_args) + [w]
    if dual:
        assert w2_col0 % tn == 0
        c2 = w2_col0 // tn
        in_specs.append(pl.BlockSpec((tk, tn), lambda i, j, k: (k + r0, j + c2)))
        args.append(w)
    in_specs += list(epi_specs)
    args += list(epi_args)
    scratch = [pltpu.VMEM((tm, tn), F32)]
    if dual:
        scratch.append(pltpu.VMEM((tm, tn), F32))
    if pro_fn is not None:
        scratch.append(pltpu.VMEM((tm, tk), BF16))
    out_spec = pl.BlockSpec((tm, tn), lambda i, j, k: (i, j))
    if row_block0:
        shift = lambda sp: pl.BlockSpec(sp.block_shape,
                                        lambda i, j, k: sp.index_map(i + row_block0, j, k))
        in_specs = [shift(sp) for sp in in_specs]
        out_spec = shift(out_spec)
    return pl.pallas_call(
        functools.partial(_mm_body, nk=nk, n_pro=len(pro_args), pro_fn=pro_fn,
                          n_epi=len(epi_args), epi_fn=epi_fn, dual=dual),
        grid=(m // tm - row_block0, n_out // tn, nk),
        in_specs=in_specs,
        out_specs=out_spec,
        out_shape=jax.ShapeDtypeStruct((m, n_out), out_dtype),
        scratch_shapes=scratch,
        compiler_params=_cparams(("parallel", "arbitrary", "arbitrary")),
        name=name,
    )(*args)


def _rmsnorm_body(x_ref, g_ref, o_ref):
    x = x_ref[...]
    ms = jnp.mean(x * x, axis=-1, keepdims=True)
    o_ref[...] = x * lax.rsqrt(ms + NORM_EPS) * g_ref[...]


def _rmsnorm_rows(x, g, *, tm, row_block0):
    m, d = x.shape
    nblk = m // tm - row_block0
    return pl.pallas_call(
        _rmsnorm_body,
        grid=(nblk,),
        in_specs=[pl.BlockSpec((tm, d), lambda i: (i + row_block0, 0)),
                  pl.BlockSpec((1, d), lambda i: (0, 0))],
        out_specs=pl.BlockSpec((tm, d), lambda i: (i, 0)),
        out_shape=jax.ShapeDtypeStruct((nblk * tm, d), x.dtype),
        compiler_params=_cparams(("parallel",)),
        name="final_norm",
    )(x, g)


def _mm_tmajor_body(x_ref, w_ref, o_ref, *, nbg, kdim, tms, act):
    xb = jnp.concatenate([x_ref[:, b * kdim:(b + 1) * kdim] for b in range(nbg)], axis=0)
    acc = jnp.dot(xb, w_ref[...], preferred_element_type=F32)
    if act is not None:
        acc = act(acc)
    for b in range(nbg):
        o_ref[:, b, :] = acc[b * tms:(b + 1) * tms].astype(o_ref.dtype)


def _mm_tmajor(x, w, *, nb, tms, tn, act=None, name="mm_tmajor"):
    s_len = x.shape[0]
    kdim, n = w.shape
    nbg = SUBLANES
    assert x.shape[1] == nb * kdim and nb % nbg == 0 and s_len % tms == 0 and n % tn == 0
    return pl.pallas_call(
        functools.partial(_mm_tmajor_body, nbg=nbg, kdim=kdim, tms=tms, act=act),
        grid=(s_len // tms, nb // nbg, n // tn),
        in_specs=[pl.BlockSpec((tms, nbg * kdim), lambda i, g, j: (i, g)),
                  pl.BlockSpec((kdim, tn), lambda i, g, j: (0, j))],
        out_specs=pl.BlockSpec((tms, nbg, tn), lambda i, g, j: (i, g, j)),
        out_shape=jax.ShapeDtypeStruct((s_len, nb, n), F32),
        compiler_params=_cparams(("parallel", "parallel", "arbitrary")),
        name=name,
    )(x, w)


def _halo_specs(tm, width, col_map, m):
    per = tm // GRID_W
    last = m // GRID_W - 1
    return [pl.BlockSpec((tm, width), lambda i, *j: (i, col_map(*j))),
            pl.BlockSpec((GRID_W, width), lambda i, *j: (jnp.maximum(i * per - 1, 0), col_map(*j))),
            pl.BlockSpec((GRID_W, width), lambda i, *j: (jnp.minimum((i + 1) * per, last), col_map(*j)))]


def _mix_body(x_ref, up_ref, dn_ref, g_ref, sc_ref, sh_ref, mu_ref, *out_refs, tm, nbc, bpb, d):
    i = pl.program_id(0)
    g, sc, sh = g_ref[...], sc_ref[...], sh_ref[...]
    h = _rms_mod(x_ref[...].astype(F32), g, sc, sh)
    row = lax.broadcasted_iota(jnp.int32, (tm, 1), 0)

    def emit(lo, hi, shifted):
        hseg = h[:, lo:hi]
        xx = shifted - hseg
        for n, o_ref in enumerate(out_refs):
            o_ref[:, lo:hi] = (hseg + xx * mu_ref[n:n + 1, lo:hi]).astype(o_ref.dtype)

    @pl.when(i < nbc)
    def _():
        half = d // 2
        emit(0, half, jnp.where(row == 0, 0.0, pltpu.roll(h[:, :half], 1, axis=0)))
        emit(half, d, jnp.where(row == tm - 1, 0.0, pltpu.roll(h[:, half:], tm - 1, axis=0)))

    @pl.when(i >= nbc)
    def _():
        jb = (i - nbc) % bpb
        q = d // 4
        col = row % GRID_W
        emit(0, q, jnp.where(col == 0, 0.0, pltpu.roll(h[:, :q], 1, axis=0)))
        emit(q, 2 * q, jnp.where(col == GRID_W - 1, 0.0, pltpu.roll(h[:, q:2 * q], tm - 1, axis=0)))
        hu = _rms_mod(up_ref[...].astype(F32), g, sc, sh)[:, 2 * q:3 * q]
        hu = hu * jnp.where(jb > 0, 1.0, 0.0)
        emit(2 * q, 3 * q, jnp.concatenate([hu, h[:tm - GRID_W, 2 * q:3 * q]], axis=0))
        hd = _rms_mod(dn_ref[...].astype(F32), g, sc, sh)[:, 3 * q:]
        hd = hd * jnp.where(jb < bpb - 1, 1.0, 0.0)
        emit(3 * q, d, jnp.concatenate([h[GRID_W:, 3 * q:], hd], axis=0))


def _rwkv_mix(xs, g, sc, sh, mu, *, dm):
    m, d = xs.shape
    tm = dm.n_ctx
    nbc, bpb = dm.mc // tm, dm.n_lat // tm
    sc_spec, sh_spec = dm.mod_spec(sc[1], tm, d), dm.mod_spec(sh[1], tm, d)
    tm_spec = pl.BlockSpec((tm, d), lambda i: dm.time_major_block(i))
    bm_spec = pl.BlockSpec((None, tm, d), lambda i: dm.time_major_block(i)[::-1] + (0,))
    return pl.pallas_call(
        functools.partial(_mix_body, tm=tm, nbc=nbc, bpb=bpb, d=d),
        grid=(m // tm,),
        in_specs=_halo_specs(tm, d, lambda: 0, m) + [
            pl.BlockSpec((1, d), lambda i: (0, 0)), sc_spec, sh_spec,
            pl.BlockSpec(mu.shape, lambda i: (0, 0))],
        out_specs=[tm_spec] * 5 + [bm_spec],
        out_shape=[jax.ShapeDtypeStruct((dm.s_len, dm.nb * d), BF16)] * 5
        + [jax.ShapeDtypeStruct((dm.nb, dm.s_len, d), BF16)],
        compiler_params=_cparams(("parallel",)),
        name="rwkv_mix",
    )(xs, xs, xs, g, sc[0], sh[0], mu)


def _conv_body(x_ref, up_ref, dn_ref, w_ref, b_ref, o_ref, *, tm, nbc, bpb, n_qcols, q_scale):
    i = pl.program_id(0)
    j = pl.program_id(1)
    row = lax.broadcasted_iota(jnp.int32, (tm, 1), 0)
    x = x_ref[...]
    scale = jnp.where(j < n_qcols, q_scale, 1.0)

    def finish(y):
        y = y + b_ref[...]
        o_ref[...] = (y * jax.nn.sigmoid(y) * scale).astype(o_ref.dtype)

    @pl.when(i < nbc)
    def _():
        prev = jnp.where(row == 0, 0.0, pltpu.roll(x, 1, axis=0))
        nxt = jnp.where(row == tm - 1, 0.0, pltpu.roll(x, tm - 1, axis=0))
        finish(prev * w_ref[3:4, :] + x * w_ref[4:5, :] + nxt * w_ref[5:6, :])

    @pl.when(i >= nbc)
    def _():
        jb = (i - nbc) % bpb
        te = tm + 2 * GRID_W
        ext = jnp.concatenate([up_ref[...] * jnp.where(jb > 0, 1.0, 0.0), x,
                               dn_ref[...] * jnp.where(jb < bpb - 1, 1.0, 0.0)], axis=0)
        col = lax.broadcasted_iota(jnp.int32, (te, 1), 0) % GRID_W
        taps = (jnp.where(col == 0, 0.0, pltpu.roll(ext, 1, axis=0)), ext,
                jnp.where(col == GRID_W - 1, 0.0, pltpu.roll(ext, te - 1, axis=0)))
        y = jnp.zeros((tm, x.shape[1]), F32)
        for di in range(3):
            for dj in range(3):
                y = y + taps[dj][di * GRID_W:di * GRID_W + tm] * w_ref[3 * di + dj:3 * di + dj + 1, :]
        finish(y)


def _mlstm_conv(u, conv_w, conv_b, *, dm, width, tc, q_scale):
    m = u.shape[0]
    tm = dm.n_ctx
    nbc, bpb = dm.mc // tm, dm.n_lat // tm
    w9 = conv_w.reshape(9, width)
    return pl.pallas_call(
        functools.partial(_conv_body, tm=tm, nbc=nbc, bpb=bpb, n_qcols=width // 2 // tc,
                          q_scale=q_scale),
        grid=(m // tm, width // tc),
        in_specs=_halo_specs(tm, tc, lambda j: j, m) + [
            pl.BlockSpec((9, tc), lambda i, j: (0, j)), pl.BlockSpec((1, tc), lambda i, j: (0, j))],
        out_specs=pl.BlockSpec((tm, tc), lambda i, j: (i, j)),
        out_shape=jax.ShapeDtypeStruct((m, width), F32),
        compiler_params=_cparams(("parallel", "parallel")),
        name="mlstm_conv",
    )(u, u, u, w9, conv_b[None, :])


def _split_parity(ref, t, n):
    x = ref[t]
    hpn = x.shape[1] // (2 * n)
    rows = jnp.concatenate([x[:, hp * 2 * n:(hp + 1) * 2 * n] for hp in range(hpn)], axis=0)
    xt = rows.T
    return xt[:n], xt[n:]


def _rwkv_scan_body(r_ref, k_ref, v_ref, wp_ref, ap_ref, kk_ref, ka_ref, rkp_ref, w0_ref, a0_ref,
                    y_ref, rko_ref, s_ref, sz_ref, gam_s, r_s, v_s, k_s, z_s, b_s, *, tblk, n):
    zdir = pl.program_id(0)
    tb = pl.program_id(2)

    @pl.when(tb == 0)
    def _():
        s_ref[...] = jnp.zeros_like(s_ref)

    bwd = zdir == 1
    t_first = jnp.where(bwd, tblk - 1, 0)
    t_step = jnp.where(bwd, -1, 1)

    def prep(i, gam):
        t = t_first + i * t_step
        r2, k2, v2 = _split_parity(r_ref, t, n), _split_parity(k_ref, t, n), _split_parity(v_ref, t, n)
        wp2, ap2 = _split_parity(wp_ref, t, n), _split_parity(ap_ref, t, n)
        out = []
        for g in range(2):
            k_t = k2[g]
            a = jax.nn.sigmoid(a0_ref[g] + ap2[g])
            gcur = gam[g] * jnp.exp(-jnp.exp(-0.5) * jax.nn.sigmoid(w0_ref[g] + wp2[g]))
            inv = 1.0 / gcur
            kr = k_t * kk_ref[g]
            nrm = jnp.sqrt(jnp.sum(kr * kr, axis=0, keepdims=True))
            kkn = kr / jnp.maximum(nrm, 1e-12)
            z_s[t, g] = -kkn * gam[g]
            b_s[t, g] = kkn * a * inv
            km = k_t * (1.0 + (a - 1.0) * ka_ref[g])
            k_s[t, g] = km * inv
            r_s[t, g] = r2[g] * gcur
            v_s[t, g] = v2[g]
            rko_ref[t, g] = jnp.sum(r2[g] * km * rkp_ref[g], axis=0, keepdims=True)
            out.append(gcur)
        return tuple(out)

    ones = jnp.ones((n, LANES), F32)
    gam_end = lax.fori_loop(0, tblk, prep, (ones, ones), unroll=8)
    for g in range(2):
        gam_s[g] = gam_end[g]

    def row(ref, t, g, kk):
        return jnp.broadcast_to(ref[t, g, pl.ds(kk, 1), :], (n, LANES))

    for g in range(2):
        acc = jnp.zeros((n, LANES), F32)
        for kk in range(n):
            acc = acc + s_ref[g, kk] * row(z_s, t_first, g, kk)
        sz_ref[g] = acc

    def step(i, carry):
        t = t_first + i * t_step
        t_next = jnp.clip(t + t_step, 0, tblk - 1)
        for g in range(2):
            sz = sz_ref[g]
            v_t = v_s[t, g]
            y = jnp.zeros((n, LANES), F32)
            sz_next = jnp.zeros((n, LANES), F32)
            for kk in range(n):
                s_new = s_ref[g, kk] + sz * row(b_s, t, g, kk) + v_t * row(k_s, t, g, kk)
                s_ref[g, kk] = s_new
                y = y + s_new * row(r_s, t, g, kk)
                sz_next = sz_next + s_new * row(z_s, t_next, g, kk)
            y_ref[t, g] = y
            sz_ref[g] = sz_next
        return carry

    lax.fori_loop(0, tblk, step, 0)

    for g in range(2):
        for kk in range(n):
            s_ref[g, kk] = s_ref[g, kk] * jnp.broadcast_to(gam_s[g, pl.ds(kk, 1), :], (n, LANES))


def _rwkv_scan(r, k, v, wp, ap, kk_p, ka_p, rk_p, w0_p, a0_p, *, n_ctx, tblk):
    s_len, nbat, d = r.shape
    n = RWKV_HEAD
    nbg = LANES // (d // (2 * n))
    assert 2 * n == LANES and nbat % nbg == 0 and n_ctx % tblk == 0 and s_len % tblk == 0
    ngrp = nbat // nbg
    nbc, nb = n_ctx // tblk, s_len // tblk

    def tmap(zd, p):
        rev = jnp.where(p < nbc, nbc - 1 - p, nb - 1 - (p - nbc))
        return jnp.where(zd == 0, p, rev)

    shared = pl.BlockSpec((tblk, nbg, d), lambda zd, g, p: (tmap(zd, p), g, 0))
    perdir = pl.BlockSpec((tblk, nbg, d), lambda zd, g, p: (tmap(zd, p), g, zd))
    par = pl.BlockSpec((2, n, LANES), lambda zd, g, p: (0, 0, 0))
    par_dir = pl.BlockSpec((None, 2, n, LANES), lambda zd, g, p: (zd, 0, 0, 0))
    step_buf = pltpu.VMEM((tblk, 2, n, LANES), F32)
    return pl.pallas_call(
        functools.partial(_rwkv_scan_body, tblk=tblk, n=n),
        grid=(2, ngrp, nb),
        in_specs=[shared, shared, shared, perdir, perdir, par, par, par, par_dir, par_dir],
        out_specs=[pl.BlockSpec((None, tblk, 2, n, LANES), lambda zd, g, p: (zd, tmap(zd, p), g, 0, 0)),
                   pl.BlockSpec((None, tblk, 2, 1, LANES), lambda zd, g, p: (zd, tmap(zd, p), g, 0, 0))],
        out_shape=[jax.ShapeDtypeStruct((2, s_len, 2 * ngrp, n, LANES), F32),
                   jax.ShapeDtypeStruct((2, s_len, 2 * ngrp, 1, LANES), F32)],
        scratch_shapes=[pltpu.VMEM((2, n, n, LANES), F32), pltpu.VMEM((2, n, LANES), F32),
                        pltpu.VMEM((2, n, LANES), F32)] + [step_buf] * 5,
        compiler_params=_cparams(("parallel", "parallel", "arbitrary")),
        name="rwkv_scan",
    )(r, k, v, wp, ap, kk_p, ka_p, rk_p, w0_p, a0_p)


def _rwkv_norm_body(y_ref, v_ref, rk_ref, lw_ref, lb_ref, o_ref, *, eps, tblk, n):
    def step(t, carry):
        v2 = _split_parity(v_ref, t, n)
        outs = []
        for g in range(2):
            ys = y_ref[0, t, g] + y_ref[1, t, g]
            mu = jnp.mean(ys, axis=0, keepdims=True)
            var = jnp.mean(jnp.square(ys - mu), axis=0, keepdims=True)
            yn = (ys - mu) * lax.rsqrt(var + eps) * lw_ref[g] + lb_ref[g]
            outs.append(yn + (rk_ref[0, t, g] + rk_ref[1, t, g]) * v2[g])
        rows = jnp.concatenate(outs, axis=0).T
        nbg = o_ref.shape[0]
        o_ref[:, t, :] = jnp.concatenate([rows[hp * nbg:(hp + 1) * nbg] for hp in range(LANES // nbg)],
                                         axis=1)
        return carry

    lax.fori_loop(0, tblk, step, 0, unroll=4)


def _rwkv_norm(y, v, rk, ln_w_p, ln_b_p, *, tblk, eps):
    s_len, nbat, d = v.shape
    n = RWKV_HEAD
    nbg = LANES // (d // (2 * n))
    par = pl.BlockSpec((2, n, LANES), lambda p, g: (0, 0, 0))
    nat = pl.BlockSpec((tblk, nbg, d), lambda p, g: (p, g, 0))
    return pl.pallas_call(
        functools.partial(_rwkv_norm_body, eps=eps, tblk=tblk, n=n),
        grid=(s_len // tblk, nbat // nbg),
        in_specs=[pl.BlockSpec((2, tblk, 2, n, LANES), lambda p, g: (0, p, g, 0, 0)), nat,
                  pl.BlockSpec((2, tblk, 2, 1, LANES), lambda p, g: (0, p, g, 0, 0)), par, par],
        out_specs=pl.BlockSpec((nbg, tblk, d), lambda p, g: (g, p, 0)),
        out_shape=jax.ShapeDtypeStruct((nbat, s_len, d), F32),
        compiler_params=_cparams(("parallel", "parallel")),
        name="rwkv_norm",
    )(y, v, rk, ln_w_p, ln_b_p)


def _mlstm_body(q_ref, k_ref, v_ref, ic_ref, fc_ref, ir_ref, fr_ref, o_ref,
                c_ref, n_ref, m_ref, *, nh, dk, dv):
    zdir = pl.program_id(0)
    p = pl.program_id(2)

    @pl.when(p == 0)
    def _():
        c_ref[...] = jnp.zeros_like(c_ref)
        n_ref[...] = jnp.zeros_like(n_ref)
        m_ref[...] = jnp.zeros_like(m_ref)

    ti = lax.broadcasted_iota(jnp.int32, (CHUNK, CHUNK), 0)
    tj = lax.broadcasted_iota(jnp.int32, (CHUNK, CHUNK), 1)
    mask = (ti - tj) * (1 - 2 * zdir) >= 0
    tri = mask.astype(F32)
    fcol = fc_ref[0]
    icol = ic_ref[0]
    frow = fr_ref[0, 0]
    irow = ir_ref[0, 0]
    hi = lax.Precision.HIGHEST
    bcum_col = jnp.dot(tri, fcol, precision=hi, preferred_element_type=F32)
    bcum_row = lax.dot_general(frow, tri, (((1,), (1,)), ((), ())), precision=hi,
                               preferred_element_type=F32)

    hs = range(nh)
    q32 = [q_ref[:, h * dk:(h + 1) * dk].astype(F32) for h in hs]
    qb = [x.astype(BF16) for x in q32]
    k32 = [k_ref[:, h * dk:(h + 1) * dk].astype(F32) for h in hs]
    vb = [v_ref[:, h * dv:(h + 1) * dv].astype(BF16) for h in hs]
    c_st = [c_ref[h] for h in hs]
    n_st = [n_ref[h] for h in hs]
    m_st = [m_ref[h][:, :1] for h in hs]
    bc = [bcum_col[:, h:h + 1] for h in hs]
    log_d = [jnp.where(mask, bc[h] - bcum_row[h:h + 1, :] + irow[h:h + 1, :], -jnp.inf) for h in hs]
    log_inter = [bc[h] + m_st[h] for h in hs]
    m_t = [jnp.maximum(log_inter[h], jnp.max(log_d[h], axis=-1, keepdims=True)) for h in hs]
    b_end = [jnp.sum(frow[h:h + 1, :], axis=-1, keepdims=True) for h in hs]
    a_col = [b_end[h] - bc[h] + icol[:, h:h + 1] for h in hs]
    m_new = [jnp.maximum(b_end[h] + m_st[h], jnp.max(a_col[h], axis=0, keepdims=True)) for h in hs]
    qk = [lax.dot_general(qb[h], k32[h].astype(BF16), (((1,), (1,)), ((), ())),
                          preferred_element_type=F32) for h in hs]
    qc = [jnp.dot(qb[h], c_st[h].astype(BF16), preferred_element_type=F32) for h in hs]
    s = [qk[h] * jnp.exp(log_d[h] - m_t[h]) for h in hs]
    sv = [jnp.dot(s[h].astype(BF16), vb[h], preferred_element_type=F32) for h in hs]
    wk = [jnp.exp(a_col[h] - m_new[h]) * k32[h] for h in hs]
    kv = [lax.dot_general(wk[h].astype(BF16), vb[h], (((0,), (0,)), ((), ())),
                          preferred_element_type=F32) for h in hs]
    w_inter = [jnp.exp(log_inter[h] - m_t[h]) for h in hs]
    qn = [jnp.sum(q32[h] * n_st[h], axis=-1, keepdims=True) for h in hs]
    ssum = [jnp.sum(s[h], axis=-1, keepdims=True) for h in hs]
    wsum = [jnp.sum(wk[h], axis=0, keepdims=True) for h in hs]
    dec = [jnp.exp(b_end[h] + m_st[h] - m_new[h]) for h in hs]
    den = [w_inter[h] * qn[h] + ssum[h] for h in hs]
    inv = [1.0 / jnp.maximum(jnp.abs(den[h]), jnp.exp(-m_t[h])) for h in hs]
    for h in hs:
        o_ref[0, :, h * dv:(h + 1) * dv] = (w_inter[h] * qc[h] + sv[h]) * inv[h]
    for h in hs:
        c_ref[h] = dec[h] * c_st[h] + kv[h]
        n_ref[h] = dec[h] * n_st[h] + wsum[h]
        m_ref[h] = jnp.broadcast_to(m_new[h], (1, LANES))


def _mlstm_scan(qk, u, v_col0, ig, lf, *, nb, n_ctx, n_lat):
    m = qk.shape[0]
    nh = MLSTM_HEADS
    dkt = qk.shape[1] // 2
    dk = dkt // nh
    dvt = 2 * dkt
    dv = dvt // nh
    ncc, ncl = n_ctx // CHUNK, n_lat // CHUNK
    nchunks = ncc + ncl
    ctx_blocks = nb * ncc
    assert v_col0 % dvt == 0

    def rmap(zd, bb, p):
        rev = jnp.where(p < ncc, ncc - 1 - p, nchunks - 1 - (p - ncc))
        ch = jnp.where(zd == 0, p, rev)
        return jnp.where(ch < ncc, bb * ncc + ch, ctx_blocks + bb * ncl + (ch - ncc))

    ig_row = jnp.swapaxes(ig.reshape(2, m // CHUNK, CHUNK, nh), 2, 3)
    lf_row = jnp.swapaxes(lf.reshape(2, m // CHUNK, CHUNK, nh), 2, 3)
    col_spec = pl.BlockSpec((1, CHUNK, nh), lambda zd, bb, p: (zd, rmap(zd, bb, p), 0))
    row_spec = pl.BlockSpec((1, 1, nh, CHUNK), lambda zd, bb, p: (zd, rmap(zd, bb, p), 0, 0))
    return pl.pallas_call(
        functools.partial(_mlstm_body, nh=nh, dk=dk, dv=dv),
        grid=(2, nb, nchunks),
        in_specs=[pl.BlockSpec((CHUNK, dkt), lambda zd, bb, p: (rmap(zd, bb, p), 0)),
                  pl.BlockSpec((CHUNK, dkt), lambda zd, bb, p: (rmap(zd, bb, p), 1)),
                  pl.BlockSpec((CHUNK, dvt), lambda zd, bb, p: (rmap(zd, bb, p), v_col0 // dvt)),
                  col_spec, col_spec, row_spec, row_spec],
        out_specs=pl.BlockSpec((1, CHUNK, dvt), lambda zd, bb, p: (zd, rmap(zd, bb, p), 0)),
        out_shape=jax.ShapeDtypeStruct((2, m, dvt), F32),
        scratch_shapes=[pltpu.VMEM((nh, dk, dv), F32), pltpu.VMEM((nh, 1, dk), F32),
                        pltpu.VMEM((nh, 1, LANES), F32)],
        compiler_params=_cparams(("parallel", "parallel", "arbitrary")),
        name="mlstm_scan",
    )(qk, qk, u, ig, lf, ig_row, lf_row)


class _Dims:
    def __init__(self, nb, n_ctx, n_lat, d):
        self.nb, self.n_ctx, self.n_lat, self.d = nb, n_ctx, n_lat, d
        self.mc, self.mx = nb * n_ctx, nb * n_lat
        self.m = self.mc + self.mx
        self.s_len = n_ctx + n_lat
        self.tm = 1024 if (self.mc % 1024 == 0 and n_lat % 1024 == 0) else n_ctx
        self.tm_half = max(self.tm // 2, n_ctx) if self.tm > n_ctx else self.tm
        assert self.mc % self.tm == 0 and n_lat % self.tm == 0 and n_ctx % GRID_W == 0

    def mod_index(self, i, tm):
        nbc = self.mc // tm
        bpb = self.n_lat // tm
        return jnp.where(i < nbc, self.nb, (i - nbc) // bpb)

    def mod_spec(self, comp, tm, width, with_col=False):
        return pl.BlockSpec((None, None, 1, width),
                            lambda i, *r: (self.mod_index(i, tm), comp, 0, r[0] if with_col else 0))

    def time_major_block(self, i):
        nbc = self.mc // self.n_ctx
        bpb = self.n_lat // self.n_ctx
        return (jnp.where(i < nbc, 0, 1 + (i - nbc) % bpb), jnp.where(i < nbc, i, (i - nbc) // bpb))

    def time_view(self, a):
        return a.reshape(self.s_len, self.nb, -1)

    def chan_to_scan(self, p):
        lead = p.shape[:-1]
        hpn = p.shape[-1] // (2 * RWKV_HEAD)
        pt = jnp.moveaxis(p.reshape(lead + (hpn, 2, RWKV_HEAD)), -3, -1)
        pt = jnp.broadcast_to(pt[..., None], lead + (2, RWKV_HEAD, hpn, LANES // hpn))
        return pt.reshape(lead + (2, RWKV_HEAD, LANES))


def _pad_to(a, axis, mult):
    pad = -a.shape[axis] % mult
    if pad == 0:
        return a
    widths = [(0, 0)] * a.ndim
    widths[axis] = (0, pad)
    return jnp.pad(a, widths)


def _rwkv_layer(dm, xs, g1, sc1, sh1, p, v_first):
    m, d, tm = dm.m, dm.d, dm.tm
    xr, xw, xk, xv, xa, xg = _rwkv_mix(xs, g1, sc1, sh1, p['mu'], dm=dm)
    mm = functools.partial(_matmul, m=m, tm=tm)
    mt = functools.partial(_mm_tmajor, nb=dm.nb, tms=2 * GRID_W)
    rows = lambda a: a.reshape(m, -1)
    r = mt(xr, p['w_r'], tn=512, name="mm_r")
    k = mt(xk, p['w_k'], tn=512, name="mm_k")
    v = rows(mt(xv, p['w_v'], tn=512, name="mm_v"))
    row_spec = pl.BlockSpec((tm, 512), lambda i, j, k: (i, j))
    vec_spec = pl.BlockSpec((1, 512), lambda i, j, k: (0, j))
    if v_first is not None:
        lv = rows(mt(xv, p['v1'], tn=LANES, name="mm_v1"))
        v = mm(lv, p['v2'], kdim=LANES, tn=512, name="mm_vres",
               epi=(_epi_vres, [v, v_first, p['v0']], [row_spec, row_spec, vec_spec]))
    lw = rows(mt(xw, p['w1'], tn=2 * LANES, act=jnp.tanh, name="mm_w1"))
    la = rows(mt(xa, p['a1'], tn=2 * LANES, name="mm_a1"))
    lg = mm(xg.reshape(m, d), p['g1'], kdim=d, tn=p['g1'].shape[1], out_dtype=BF16, name="mm_g1",
            epi=(jax.nn.sigmoid, [], []))
    g = mm(lg, p['g2'], kdim=p['g2'].shape[0], tn=512, name="mm_g2")
    wp = mm(lw, p['w2'], kdim=2 * LANES, tn=512, name="mm_w2")
    ap = mm(la, p['a2'], kdim=2 * LANES, tn=512, name="mm_a2")
    cs, hv = dm.chan_to_scan, dm.time_view
    y, rk = _rwkv_scan(hv(r), hv(k), hv(v), hv(wp), hv(ap),
                       cs(p['k_k']), cs(p['k_a']), cs(p['r_k']), cs(p['w0']), cs(p['a0']),
                       n_ctx=dm.n_ctx, tblk=32)
    pre = _rwkv_norm(y, hv(v), rk, cs(p['ln_w']), cs(p['ln_b']), tblk=32, eps=RWKV_HEAD * 1e-5)
    return pre, g, v


def _mlstm_layer(dm, xs, g1, sc1, sh1, p):
    m, d, tm = dm.m, dm.d, dm.tm
    nh = MLSTM_HEADS
    u = _matmul(None, p['w_in'], m=m, kdim=d, tm=tm, tn=MLSTM_IN_TN, name="mm_mlstm_in",
                pro=(_pro_normmod, [xs, g1, sc1[0], sh1[0]],
                     [pl.BlockSpec((tm, d), lambda i, j, k: (i, 0)),
                      pl.BlockSpec((1, d), lambda i, j, k: (0, 0)),
                      dm.mod_spec(sc1[1], tm, d), dm.mod_spec(sh1[1], tm, d)]))
    qk = _mlstm_conv(u, p['conv_w'], p['conv_b'], dm=dm, width=d, tc=min(1024, d // 2),
                     q_scale=float(d // 2 // nh) ** -0.5)
    gates = u[:, 3 * d:3 * d + 4 * nh].reshape(m, 2, 2, nh) + p['b_gate']
    gates = GATE_CAP * jnp.tanh(gates / GATE_CAP)
    ig = jnp.moveaxis(gates[:, :, 0, :], 1, 0)
    lf = jnp.moveaxis(jax.nn.log_sigmoid(gates[:, :, 1, :]), 1, 0)
    hs = _mlstm_scan(qk, u, d, ig, lf, nb=dm.nb, n_ctx=dm.n_ctx, n_lat=dm.n_lat)
    return hs, u


def kernel(x, c, ctx, c_ctx, mod_w, mod_b, norm_g, final_g, rwkv_mu, rwkv_w_r, rwkv_w_k, rwkv_w_v, rwkv_w_o, rwkv_w0, rwkv_w1, rwkv_w2, rwkv_a0, rwkv_a1, rwkv_a2, rwkv_g1, rwkv_g2, rwkv_k_k, rwkv_k_a, rwkv_r_k, rwkv_ln_w, rwkv_ln_b, rwkv_v0, rwkv_v1, rwkv_v2, mlstm_w_in, mlstm_b_gate, mlstm_conv_w, mlstm_conv_b, mlstm_norm_w, mlstm_w_out, ffn_w_in, ffn_w_out):
    nb, n_lat, d = x.shape
    n_ctx = ctx.shape[1]
    depth = mod_w.shape[0]
    d_ff = ffn_w_out.shape[1]
    dm = _Dims(nb, n_ctx, n_lat, d)
    m, tm, tmh = dm.m, dm.tm, dm.tm_half
    bf = lambda a: a.astype(BF16)

    cond = jax.nn.silu(jnp.concatenate([c, c_ctx[None, :]], axis=0))
    rows = cond.shape[0]
    cond = _pad_to(cond, 0, 2 * SUBLANES)
    mod_w2 = mod_w.reshape(depth * d, 6 * d)
    mods = []
    for i in range(depth):
        mo = _matmul(cond, mod_w2, m=cond.shape[0], kdim=d, tm=cond.shape[0], tn=512,
                     w_row0=i * d, name="mm_mod") + mod_b[i]
        mods.append(mo[:rows].reshape(rows, 6, 1, d))

    def gate_res(res, gate, tm_, tn_=512):
        return (_epi_gate_res, [res, gate[0]],
                [pl.BlockSpec((tm_, tn_), lambda i, j, k: (i, j)),
                 dm.mod_spec(gate[1], tm_, tn_, with_col=True)])

    xs = jnp.concatenate([ctx.reshape(dm.mc, d), x.reshape(dm.mx, d)], axis=0)
    v_first = None
    for i in range(depth):
        j = i // 2
        sh1, sc1, gt1, sh2, sc2, gt2 = [(mods[i], n) for n in range(6)]
        g1 = norm_g[i, 0][None, :]
        g2 = norm_g[i, 1][None, :]
        full_k = lambda tm_: pl.BlockSpec((tm_, d), lambda i, j, k: (i, 0))
        skip = (lambda tm_: dm.mc // tm_) if i == depth - 1 else (lambda tm_: 0)
        if i % 2 == 0:
            lora = lambda a: bf(jnp.concatenate([_pad_to(a[0], 1, LANES), _pad_to(a[1], 1, LANES)], axis=1))

            def lora_up(a):
                ap_ = _pad_to(a, 1, LANES)
                zero = jnp.zeros_like(ap_[0])
                return bf(jnp.concatenate([jnp.concatenate([ap_[0], zero], axis=1),
                                           jnp.concatenate([zero, ap_[1]], axis=1)], axis=0))

            p = {'mu': rwkv_mu[j], 'w_r': bf(rwkv_w_r[j]), 'w_k': bf(rwkv_w_k[j]), 'w_v': bf(rwkv_w_v[j]),
                 'w0': rwkv_w0[j], 'w1': lora(rwkv_w1[j]), 'w2': lora_up(rwkv_w2[j]),
                 'a0': rwkv_a0[j], 'a1': lora(rwkv_a1[j]), 'a2': lora_up(rwkv_a2[j]),
                 'g1': bf(rwkv_g1[j]), 'g2': bf(rwkv_g2[j]),
                 'k_k': rwkv_k_k[j], 'k_a': rwkv_k_a[j], 'r_k': rwkv_r_k[j].reshape(-1),
                 'ln_w': rwkv_ln_w[j], 'ln_b': rwkv_ln_b[j]}
            if j > 0:
                p['v0'] = rwkv_v0[j - 1][None, :]
                p['v1'] = bf(_pad_to(rwkv_v1[j - 1], 1, LANES))
                p['v2'] = bf(_pad_to(rwkv_v2[j - 1], 0, LANES))
            pre, g, v_cur = _rwkv_layer(dm, xs, g1, sc1, sh1, p, v_first if j > 0 else None)
            if j == 0:
                v_first = v_cur
            tmo = n_ctx
            tm_spec = pl.BlockSpec((None, tmo, d), lambda i, j, k: dm.time_major_block(i)[::-1] + (0,))
            xs = _matmul(None, bf(rwkv_w_o[j]), m=m, kdim=d, tm=tmo, tn=d, name="mm_rwkv_out",
                         row_block0=skip(tmo),
                         pro=(_pro_mulg, [pre, g.reshape(nb, dm.s_len, d)],
                              [tm_spec, tm_spec]),
                         epi=gate_res(xs, gt1, tmo, d))
        else:
            p = {'w_in': bf(_pad_to(mlstm_w_in[j], 1, MLSTM_IN_TN)),
                 'b_gate': mlstm_b_gate[j], 'conv_w': mlstm_conv_w[j], 'conv_b': mlstm_conv_b[j]}
            hs, u = _mlstm_layer(dm, xs, g1, sc1, sh1, p)
            xs = _matmul(None, bf(mlstm_w_out[j]), m=m, kdim=d, tm=tmh, tn=512, name="mm_mlstm_out",
                         row_block0=skip(tmh),
                         pro=(_pro_mlstm_read, [hs, u, mlstm_norm_w[j][None, :]],
                              [pl.BlockSpec((2, tmh, d), lambda i, j, k: (0, i, 0)),
                               pl.BlockSpec((tmh, d), lambda i, j, k: (i, 2)),
                               pl.BlockSpec((1, d), lambda i, j, k: (0, 0))]),
                         epi=gate_res(xs, gt1, tmh))
        hid = _matmul(None, bf(ffn_w_in[i]), m=m, kdim=d, tm=tm, tn=512, out_dtype=BF16,
                      n_out=d_ff, w2_col0=d_ff, name="mm_ffn_in", row_block0=skip(tm),
                      pro=(_pro_normmod, [xs, g2, sc2[0], sh2[0]],
                           [full_k(tm), pl.BlockSpec((1, d), lambda i, j, k: (0, 0)),
                            dm.mod_spec(sc2[1], tm, d), dm.mod_spec(sh2[1], tm, d)]),
                      epi=(_epi_swiglu, [], []))
        xs = _matmul(hid, bf(ffn_w_out[i]), m=m, kdim=d_ff, tm=tm, tn=512, row_block0=skip(tm),
                     name="mm_ffn_out", epi=gate_res(xs, gt2, tm))
    out = _rmsnorm_rows(xs, final_g[None, :], tm=tm, row_block0=dm.mc // tm)
    return out.reshape(nb, n_lat, d)
```

```python
import functools

import jax
import jax.numpy as jnp
from jax import lax
from jax.experimental import pallas as pl
from jax.experimental.pallas import tpu as pltpu

F32 = jnp.float32
BF16 = jnp.bfloat16

GRID_W = 64
NORM_EPS = 1e-6
RWKV_HEAD = 64
MLSTM_HEADS = 8
CHUNK = 64
GATE_CAP = 15.0
LANES = 128
SUBLANES = 8
VMEM_LIMIT = 56 * 1024 * 1024
MLSTM_IN_TN = 1280


def _cparams(sem):
    return pltpu.CompilerParams(dimension_semantics=sem, vmem_limit_bytes=VMEM_LIMIT)


def _rms_mod(x, g, sc, sh):
    ms = jnp.mean(x * x, axis=-1, keepdims=True)
    return x * lax.rsqrt(ms + NORM_EPS) * g * (1.0 + sc) + sh


def _head_ln_lanes(y, nheads, eps):
    hd = y.shape[-1] // nheads
    out = []
    for h in range(nheads):
        seg = y[:, h * hd:(h + 1) * hd]
        mu = jnp.mean(seg, axis=-1, keepdims=True)
        var = jnp.mean(jnp.square(seg - mu), axis=-1, keepdims=True)
        out.append((seg - mu) * lax.rsqrt(var + eps))
    return jnp.concatenate(out, axis=-1)


def _pro_normmod(x_ref, g_ref, sc_ref, sh_ref):
    return _rms_mod(x_ref[...].astype(F32), g_ref[...], sc_ref[...], sh_ref[...]).astype(BF16)


def _pro_mulg(x_ref, g_ref):
    return (x_ref[...] * g_ref[...]).astype(BF16)


def _pro_mlstm_read(h_ref, o_ref, nw_ref):
    hn = _head_ln_lanes(h_ref[0] + h_ref[1], MLSTM_HEADS, NORM_EPS) * nw_ref[...]
    return (hn * jax.nn.sigmoid(o_ref[...])).astype(BF16)


def _post_normmod(y, g_ref, sc_ref, sh_ref):
    return _rms_mod(y, g_ref[...], sc_ref[...], sh_ref[...])


def _epi_swiglu(acc, acc2):
    return acc * jax.nn.sigmoid(acc) * acc2


def _epi_gate_res(acc, res_ref, gate_ref):
    return res_ref[...] + gate_ref[...] * acc


def _epi_vres(acc, v_ref, vf_ref, v0_ref):
    v = v_ref[...]
    return v + (vf_ref[...] - v) * jax.nn.sigmoid(v0_ref[...] + acc)


def _mm_body(*refs, nk, n_pro, pro_fn, n_epi, epi_fn, dual, n_post, post_fn):
    it = iter(refs)
    pro_refs = [next(it) for _ in range(n_pro)]
    w_ref = next(it)
    w2_ref = next(it) if dual else None
    epi_refs = [next(it) for _ in range(n_epi)]
    post_refs = [next(it) for _ in range(n_post)]
    o_ref = next(it)
    o2_ref = next(it) if post_fn is not None else None
    acc_ref = next(it)
    acc2_ref = next(it) if dual else None
    xb_ref = next(it) if pro_fn is not None else None

    j = pl.program_id(1)
    k = pl.program_id(2)

    if pro_fn is not None:
        @pl.when(j == 0)
        def _():
            xb_ref[...] = pro_fn(*pro_refs)
        xb = xb_ref[...]
    else:
        xb = pro_refs[0][...].astype(BF16)

    @pl.when(k == 0)
    def _():
        acc_ref[...] = jnp.zeros_like(acc_ref)
        if dual:
            acc2_ref[...] = jnp.zeros_like(acc2_ref)

    acc_ref[...] += jnp.dot(xb, w_ref[...].astype(BF16), preferred_element_type=F32)
    if dual:
        acc2_ref[...] += jnp.dot(xb, w2_ref[...].astype(BF16), preferred_element_type=F32)

    @pl.when(k == nk - 1)
    def _():
        acc = acc_ref[...]
        if dual:
            acc = epi_fn(acc, acc2_ref[...], *epi_refs)
        elif epi_fn is not None:
            acc = epi_fn(acc, *epi_refs)
        o_ref[...] = acc.astype(o_ref.dtype)
        if post_fn is not None:
            o2_ref[...] = post_fn(acc, *post_refs).astype(o2_ref.dtype)


def _matmul(x, w, *, m, kdim, tm, tn, tk=None, out_dtype=F32, n_out=None, w_col0=0, w2_col0=None,
            w_row0=0, x_col0=0, row_block0=0, pro=None, epi=None, post=None, name="mm"):
    n_out = w.shape[1] if n_out is None else n_out
    tk = kdim if tk is None else tk
    dual = w2_col0 is not None
    assert m % tm == 0 and n_out % tn == 0 and kdim % tk == 0
    assert w_col0 % tn == 0 and w_row0 % tk == 0 and x_col0 % tk == 0
    nk = kdim // tk
    c0, r0, xc0 = w_col0 // tn, w_row0 // tk, x_col0 // tk
    pro_fn, pro_args, pro_specs = pro if pro is not None else (None, [x], [
        pl.BlockSpec((tm, tk), lambda i, j, k: (i, k + xc0))])
    if pro_fn is not None:
        assert nk == 1
    epi_fn, epi_args, epi_specs = epi if epi is not None else (None, [], [])
    in_specs = list(pro_specs) + [pl.BlockSpec((tk, tn), lambda i, j, k: (k + r0, j + c0))]
    args = list(pro_args) + [w]
    if dual:
        assert w2_col0 % tn == 0
        c2 = w2_col0 // tn
        in_specs.append(pl.BlockSpec((tk, tn), lambda i, j, k: (k + r0, j + c2)))
        args.append(w)
    in_specs += list(epi_specs)
    args += list(epi_args)
    post_fn, post_args, post_specs, post_dtype = post if post is not None else (None, [], [], None)
    if post_fn is not None:
        assert tn == n_out and nk == 1
    in_specs += list(post_specs)
    args += list(post_args)
    scratch = [pltpu.VMEM((tm, tn), F32)]
    if dual:
        scratch.append(pltpu.VMEM((tm, tn), F32))
    if pro_fn is not None:
        scratch.append(pltpu.VMEM((tm, tk), BF16))
    out_spec = pl.BlockSpec((tm, tn), lambda i, j, k: (i, j))
    if row_block0:
        shift = lambda sp: pl.BlockSpec(sp.block_shape,
                                        lambda i, j, k: sp.index_map(i + row_block0, j, k))
        in_specs = [shift(sp) for sp in in_specs]
        out_spec = shift(out_spec)
    out_specs, out_shape = out_spec, jax.ShapeDtypeStruct((m, n_out), out_dtype)
    if post_fn is not None:
        out_specs, out_shape = [out_spec, out_spec], [out_shape, jax.ShapeDtypeStruct((m, n_out), post_dtype)]
    return pl.pallas_call(
        functools.partial(_mm_body, nk=nk, n_pro=len(pro_args), pro_fn=pro_fn,
                          n_epi=len(epi_args), epi_fn=epi_fn, dual=dual,
                          n_post=len(post_args), post_fn=post_fn),
        grid=(m // tm - row_block0, n_out // tn, nk),
        in_specs=in_specs,
        out_specs=out_specs,
        out_shape=out_shape,
        scratch_shapes=scratch,
        compiler_params=_cparams(("parallel", "arbitrary", "arbitrary")),
        name=name,
    )(*args)


def _rmsnorm_body(x_ref, g_ref, o_ref):
    x = x_ref[...]
    ms = jnp.mean(x * x, axis=-1, keepdims=True)
    o_ref[...] = x * lax.rsqrt(ms + NORM_EPS) * g_ref[...]


def _rmsnorm_rows(x, g, *, tm, row_block0):
    m, d = x.shape
    nblk = m // tm - row_block0
    return pl.pallas_call(
        _rmsnorm_body,
        grid=(nblk,),
        in_specs=[pl.BlockSpec((tm, d), lambda i: (i + row_block0, 0)),
                  pl.BlockSpec((1, d), lambda i: (0, 0))],
        out_specs=pl.BlockSpec((tm, d), lambda i: (i, 0)),
        out_shape=jax.ShapeDtypeStruct((nblk * tm, d), x.dtype),
        compiler_params=_cparams(("parallel",)),
        name="final_norm",
    )(x, g)


def _mm_tmajor_body(x_ref, w_ref, o_ref, *, nbg, kdim, tms, act):
    xb = jnp.concatenate([x_ref[:, b * kdim:(b + 1) * kdim] for b in range(nbg)], axis=0)
    acc = jnp.dot(xb, w_ref[...], preferred_element_type=F32)
    if act is not None:
        acc = act(acc)
    for b in range(nbg):
        o_ref[:, b, :] = acc[b * tms:(b + 1) * tms].astype(o_ref.dtype)


def _mm_tmajor(x, w, *, nb, tms, tn, act=None, name="mm_tmajor"):
    s_len = x.shape[0]
    kdim, n = w.shape
    nbg = SUBLANES
    assert x.shape[1] == nb * kdim and nb % nbg == 0 and s_len % tms == 0 and n % tn == 0
    return pl.pallas_call(
        functools.partial(_mm_tmajor_body, nbg=nbg, kdim=kdim, tms=tms, act=act),
        grid=(s_len // tms, nb // nbg, n // tn),
        in_specs=[pl.BlockSpec((tms, nbg * kdim), lambda i, g, j: (i, g)),
                  pl.BlockSpec((kdim, tn), lambda i, g, j: (0, j))],
        out_specs=pl.BlockSpec((tms, nbg, tn), lambda i, g, j: (i, g, j)),
        out_shape=jax.ShapeDtypeStruct((s_len, nb, n), F32),
        compiler_params=_cparams(("parallel", "parallel", "arbitrary")),
        name=name,
    )(x, w)


def _halo_specs(tm, width, col_map, m):
    per = tm // GRID_W
    last = m // GRID_W - 1
    return [pl.BlockSpec((tm, width), lambda i, *j: (i, col_map(*j))),
            pl.BlockSpec((GRID_W, width), lambda i, *j: (jnp.maximum(i * per - 1, 0), col_map(*j))),
            pl.BlockSpec((GRID_W, width), lambda i, *j: (jnp.minimum((i + 1) * per, last), col_map(*j)))]


def _mix_body(x_ref, up_ref, dn_ref, g_ref, sc_ref, sh_ref, mu_ref, *out_refs, tm, nbc, bpb, d):
    i = pl.program_id(0)
    g, sc, sh = g_ref[...], sc_ref[...], sh_ref[...]
    h = _rms_mod(x_ref[...].astype(F32), g, sc, sh)
    row = lax.broadcasted_iota(jnp.int32, (tm, 1), 0)

    def emit(lo, hi, shifted):
        hseg = h[:, lo:hi]
        xx = shifted - hseg
        for n, o_ref in enumerate(out_refs):
            o_ref[:, lo:hi] = (hseg + xx * mu_ref[n:n + 1, lo:hi]).astype(o_ref.dtype)

    @pl.when(i < nbc)
    def _():
        half = d // 2
        emit(0, half, jnp.where(row == 0, 0.0, pltpu.roll(h[:, :half], 1, axis=0)))
        emit(half, d, jnp.where(row == tm - 1, 0.0, pltpu.roll(h[:, half:], tm - 1, axis=0)))

    @pl.when(i >= nbc)
    def _():
        jb = (i - nbc) % bpb
        q = d // 4
        col = row % GRID_W
        emit(0, q, jnp.where(col == 0, 0.0, pltpu.roll(h[:, :q], 1, axis=0)))
        emit(q, 2 * q, jnp.where(col == GRID_W - 1, 0.0, pltpu.roll(h[:, q:2 * q], tm - 1, axis=0)))
        hu = _rms_mod(up_ref[...].astype(F32), g, sc, sh)[:, 2 * q:3 * q]
        hu = hu * jnp.where(jb > 0, 1.0, 0.0)
        emit(2 * q, 3 * q, jnp.concatenate([hu, h[:tm - GRID_W, 2 * q:3 * q]], axis=0))
        hd = _rms_mod(dn_ref[...].astype(F32), g, sc, sh)[:, 3 * q:]
        hd = hd * jnp.where(jb < bpb - 1, 1.0, 0.0)
        emit(3 * q, d, jnp.concatenate([h[GRID_W:, 3 * q:], hd], axis=0))


def _rwkv_mix(xs, g, sc, sh, mu, *, dm):
    m, d = xs.shape
    tm = dm.n_ctx
    nbc, bpb = dm.mc // tm, dm.n_lat // tm
    sc_spec, sh_spec = dm.mod_spec(sc[1], tm, d), dm.mod_spec(sh[1], tm, d)
    tm_spec = pl.BlockSpec((tm, d), lambda i: dm.time_major_block(i))
    bm_spec = pl.BlockSpec((None, tm, d), lambda i: dm.time_major_block(i)[::-1] + (0,))
    return pl.pallas_call(
        functools.partial(_mix_body, tm=tm, nbc=nbc, bpb=bpb, d=d),
        grid=(m // tm,),
        in_specs=_halo_specs(tm, d, lambda: 0, m) + [
            pl.BlockSpec((1, d), lambda i: (0, 0)), sc_spec, sh_spec,
            pl.BlockSpec(mu.shape, lambda i: (0, 0))],
        out_specs=[tm_spec] * 5 + [bm_spec],
        out_shape=[jax.ShapeDtypeStruct((dm.s_len, dm.nb * d), BF16)] * 5
        + [jax.ShapeDtypeStruct((dm.nb, dm.s_len, d), BF16)],
        compiler_params=_cparams(("parallel",)),
        name="rwkv_mix",
    )(xs, xs, xs, g, sc[0], sh[0], mu)


def _conv_body(x_ref, up_ref, dn_ref, w_ref, b_ref, o_ref, *, tm, nbc, bpb, n_qcols, q_scale):
    i = pl.program_id(0)
    j = pl.program_id(1)
    row = lax.broadcasted_iota(jnp.int32, (tm, 1), 0)
    x = x_ref[...]
    scale = jnp.where(j < n_qcols, q_scale, 1.0)

    def finish(y):
        y = y + b_ref[...]
        o_ref[...] = (y * jax.nn.sigmoid(y) * scale).astype(o_ref.dtype)

    @pl.when(i < nbc)
    def _():
        prev = jnp.where(row == 0, 0.0, pltpu.roll(x, 1, axis=0))
        nxt = jnp.where(row == tm - 1, 0.0, pltpu.roll(x, tm - 1, axis=0))
        finish(prev * w_ref[3:4, :] + x * w_ref[4:5, :] + nxt * w_ref[5:6, :])

    @pl.when(i >= nbc)
    def _():
        jb = (i - nbc) % bpb
        te = tm + 2 * GRID_W
        ext = jnp.concatenate([up_ref[...] * jnp.where(jb > 0, 1.0, 0.0), x,
                               dn_ref[...] * jnp.where(jb < bpb - 1, 1.0, 0.0)], axis=0)
        col = lax.broadcasted_iota(jnp.int32, (te, 1), 0) % GRID_W
        taps = (jnp.where(col == 0, 0.0, pltpu.roll(ext, 1, axis=0)), ext,
                jnp.where(col == GRID_W - 1, 0.0, pltpu.roll(ext, te - 1, axis=0)))
        y = jnp.zeros((tm, x.shape[1]), F32)
        for di in range(3):
            for dj in range(3):
                y = y + taps[dj][di * GRID_W:di * GRID_W + tm] * w_ref[3 * di + dj:3 * di + dj + 1, :]
        finish(y)


def _mlstm_conv(u, conv_w, conv_b, *, dm, width, tc, q_scale):
    m = u.shape[0]
    tm = dm.n_ctx
    nbc, bpb = dm.mc // tm, dm.n_lat // tm
    w9 = conv_w.reshape(9, width)
    return pl.pallas_call(
        functools.partial(_conv_body, tm=tm, nbc=nbc, bpb=bpb, n_qcols=width // 2 // tc,
                          q_scale=q_scale),
        grid=(m // tm, width // tc),
        in_specs=_halo_specs(tm, tc, lambda j: j, m) + [
            pl.BlockSpec((9, tc), lambda i, j: (0, j)), pl.BlockSpec((1, tc), lambda i, j: (0, j))],
        out_specs=pl.BlockSpec((tm, tc), lambda i, j: (i, j)),
        out_shape=jax.ShapeDtypeStruct((m, width), F32),
        compiler_params=_cparams(("parallel", "parallel")),
        name="mlstm_conv",
    )(u, u, u, w9, conv_b[None, :])


def _split_parity(ref, t, n):
    x = ref[t]
    hpn = x.shape[1] // (2 * n)
    rows = jnp.concatenate([x[:, hp * 2 * n:(hp + 1) * 2 * n] for hp in range(hpn)], axis=0)
    xt = rows.T
    return xt[:n], xt[n:]


def _rwkv_scan_body(r_ref, k_ref, v_ref, wp_ref, ap_ref, kk_ref, ka_ref, rkp_ref, w0_ref, a0_ref,
                    y_ref, rko_ref, s_ref, sz_ref, gam_s, r_s, v_s, k_s, z_s, b_s, *, tblk, n):
    zdir = pl.program_id(0)
    tb = pl.program_id(2)

    @pl.when(tb == 0)
    def _():
        s_ref[...] = jnp.zeros_like(s_ref)

    bwd = zdir == 1
    t_first = jnp.where(bwd, tblk - 1, 0)
    t_step = jnp.where(bwd, -1, 1)

    def prep(i, gam):
        t = t_first + i * t_step
        r2, k2, v2 = _split_parity(r_ref, t, n), _split_parity(k_ref, t, n), _split_parity(v_ref, t, n)
        wp2, ap2 = _split_parity(wp_ref, t, n), _split_parity(ap_ref, t, n)
        out = []
        for g in range(2):
            k_t = k2[g]
            a = jax.nn.sigmoid(a0_ref[g] + ap2[g])
            gcur = gam[g] * jnp.exp(-jnp.exp(-0.5) * jax.nn.sigmoid(w0_ref[g] + wp2[g]))
            inv = 1.0 / gcur
            kr = k_t * kk_ref[g]
            nrm = jnp.sqrt(jnp.sum(kr * kr, axis=0, keepdims=True))
            kkn = kr / jnp.maximum(nrm, 1e-12)
            z_s[t, g] = -kkn * gam[g]
            b_s[t, g] = kkn * a * inv
            km = k_t * (1.0 + (a - 1.0) * ka_ref[g])
            k_s[t, g] = km * inv
            r_s[t, g] = r2[g] * gcur
            v_s[t, g] = v2[g]
            rko_ref[t, g] = jnp.sum(r2[g] * km * rkp_ref[g], axis=0, keepdims=True)
            out.append(gcur)
        return tuple(out)

    ones = jnp.ones((n, LANES), F32)
    gam_end = lax.fori_loop(0, tblk, prep, (ones, ones), unroll=8)
    for g in range(2):
        gam_s[g] = gam_end[g]

    def row(ref, t, g, kk):
        return jnp.broadcast_to(ref[t, g, pl.ds(kk, 1), :], (n, LANES))

    for g in range(2):
        acc = jnp.zeros((n, LANES), F32)
        for kk in range(n):
            acc = acc + s_ref[g, kk] * row(z_s, t_first, g, kk)
        sz_ref[g] = acc

    def step(i, carry):
        t = t_first + i * t_step
        t_next = jnp.clip(t + t_step, 0, tblk - 1)
        for g in range(2):
            sz = sz_ref[g]
            v_t = v_s[t, g]
            y = jnp.zeros((n, LANES), F32)
            sz_next = jnp.zeros((n, LANES), F32)
            for kk in range(n):
                s_new = s_ref[g, kk] + sz * row(b_s, t, g, kk) + v_t * row(k_s, t, g, kk)
                s_ref[g, kk] = s_new
                y = y + s_new * row(r_s, t, g, kk)
                sz_next = sz_next + s_new * row(z_s, t_next, g, kk)
            y_ref[t, g] = y
            sz_ref[g] = sz_next
        return carry

    lax.fori_loop(0, tblk, step, 0)

    for g in range(2):
        for kk in range(n):
            s_ref[g, kk] = s_ref[g, kk] * jnp.broadcast_to(gam_s[g, pl.ds(kk, 1), :], (n, LANES))


def _rwkv_scan(r, k, v, wp, ap, kk_p, ka_p, rk_p, w0_p, a0_p, *, n_ctx, tblk):
    s_len, nbat, d = r.shape
    n = RWKV_HEAD
    nbg = LANES // (d // (2 * n))
    assert 2 * n == LANES and nbat % nbg == 0 and n_ctx % tblk == 0 and s_len % tblk == 0
    ngrp = nbat // nbg
    nbc, nb = n_ctx // tblk, s_len // tblk

    def tmap(zd, p):
        rev = jnp.where(p < nbc, nbc - 1 - p, nb - 1 - (p - nbc))
        return jnp.where(zd == 0, p, rev)

    shared = pl.BlockSpec((tblk, nbg, d), lambda zd, g, p: (tmap(zd, p), g, 0))
    perdir = pl.BlockSpec((tblk, nbg, d), lambda zd, g, p: (tmap(zd, p), g, zd))
    par = pl.BlockSpec((2, n, LANES), lambda zd, g, p: (0, 0, 0))
    par_dir = pl.BlockSpec((None, 2, n, LANES), lambda zd, g, p: (zd, 0, 0, 0))
    step_buf = pltpu.VMEM((tblk, 2, n, LANES), F32)
    return pl.pallas_call(
        functools.partial(_rwkv_scan_body, tblk=tblk, n=n),
        grid=(2, ngrp, nb),
        in_specs=[shared, shared, shared, perdir, perdir, par, par, par, par_dir, par_dir],
        out_specs=[pl.BlockSpec((None, tblk, 2, n, LANES), lambda zd, g, p: (zd, tmap(zd, p), g, 0, 0)),
                   pl.BlockSpec((None, tblk, 2, 1, LANES), lambda zd, g, p: (zd, tmap(zd, p), g, 0, 0))],
        out_shape=[jax.ShapeDtypeStruct((2, s_len, 2 * ngrp, n, LANES), F32),
                   jax.ShapeDtypeStruct((2, s_len, 2 * ngrp, 1, LANES), F32)],
        scratch_shapes=[pltpu.VMEM((2, n, n, LANES), F32), pltpu.VMEM((2, n, LANES), F32),
                        pltpu.VMEM((2, n, LANES), F32)] + [step_buf] * 5,
        compiler_params=_cparams(("parallel", "parallel", "arbitrary")),
        name="rwkv_scan",
    )(r, k, v, wp, ap, kk_p, ka_p, rk_p, w0_p, a0_p)


def _rwkv_norm_body(y_ref, v_ref, rk_ref, lw_ref, lb_ref, o_ref, *, eps, tblk, n):
    def step(t, carry):
        v2 = _split_parity(v_ref, t, n)
        outs = []
        for g in range(2):
            ys = y_ref[0, t, g] + y_ref[1, t, g]
            mu = jnp.mean(ys, axis=0, keepdims=True)
            var = jnp.mean(jnp.square(ys - mu), axis=0, keepdims=True)
            yn = (ys - mu) * lax.rsqrt(var + eps) * lw_ref[g] + lb_ref[g]
            outs.append(yn + (rk_ref[0, t, g] + rk_ref[1, t, g]) * v2[g])
        rows = jnp.concatenate(outs, axis=0).T
        nbg = o_ref.shape[0]
        o_ref[:, t, :] = jnp.concatenate([rows[hp * nbg:(hp + 1) * nbg] for hp in range(LANES // nbg)],
                                         axis=1)
        return carry

    lax.fori_loop(0, tblk, step, 0, unroll=4)


def _rwkv_norm(y, v, rk, ln_w_p, ln_b_p, *, tblk, eps):
    s_len, nbat, d = v.shape
    n = RWKV_HEAD
    nbg = LANES // (d // (2 * n))
    par = pl.BlockSpec((2, n, LANES), lambda p, g: (0, 0, 0))
    nat = pl.BlockSpec((tblk, nbg, d), lambda p, g: (p, g, 0))
    return pl.pallas_call(
        functools.partial(_rwkv_norm_body, eps=eps, tblk=tblk, n=n),
        grid=(s_len // tblk, nbat // nbg),
        in_specs=[pl.BlockSpec((2, tblk, 2, n, LANES), lambda p, g: (0, p, g, 0, 0)), nat,
                  pl.BlockSpec((2, tblk, 2, 1, LANES), lambda p, g: (0, p, g, 0, 0)), par, par],
        out_specs=pl.BlockSpec((nbg, tblk, d), lambda p, g: (g, p, 0)),
        out_shape=jax.ShapeDtypeStruct((nbat, s_len, d), F32),
        compiler_params=_cparams(("parallel", "parallel")),
        name="rwkv_norm",
    )(y, v, rk, ln_w_p, ln_b_p)


def _mlstm_body(q_ref, k_ref, v_ref, ic_ref, fc_ref, ir_ref, fr_ref, o_ref,
                c_ref, n_ref, m_ref, *, nh, dk, dv):
    zdir = pl.program_id(0)
    p = pl.program_id(2)

    @pl.when(p == 0)
    def _():
        c_ref[...] = jnp.zeros_like(c_ref)
        n_ref[...] = jnp.zeros_like(n_ref)
        m_ref[...] = jnp.zeros_like(m_ref)

    ti = lax.broadcasted_iota(jnp.int32, (CHUNK, CHUNK), 0)
    tj = lax.broadcasted_iota(jnp.int32, (CHUNK, CHUNK), 1)
    mask = (ti - tj) * (1 - 2 * zdir) >= 0
    tri = mask.astype(F32)
    fcol = fc_ref[0]
    icol = ic_ref[0]
    frow = fr_ref[0, 0]
    irow = ir_ref[0, 0]
    hi = lax.Precision.HIGHEST
    bcum_col = jnp.dot(tri, fcol, precision=hi, preferred_element_type=F32)
    bcum_row = lax.dot_general(frow, tri, (((1,), (1,)), ((), ())), precision=hi,
                               preferred_element_type=F32)

    hs = range(nh)
    q32 = [q_ref[:, h * dk:(h + 1) * dk].astype(F32) for h in hs]
    qb = [x.astype(BF16) for x in q32]
    k32 = [k_ref[:, h * dk:(h + 1) * dk].astype(F32) for h in hs]
    vb = [v_ref[:, h * dv:(h + 1) * dv].astype(BF16) for h in hs]
    c_st = [c_ref[h] for h in hs]
    n_st = [n_ref[h] for h in hs]
    m_st = [m_ref[h][:, :1] for h in hs]
    bc = [bcum_col[:, h:h + 1] for h in hs]
    log_d = [jnp.where(mask, bc[h] - bcum_row[h:h + 1, :] + irow[h:h + 1, :], -jnp.inf) for h in hs]
    log_inter = [bc[h] + m_st[h] for h in hs]
    m_t = [jnp.maximum(log_inter[h], jnp.max(log_d[h], axis=-1, keepdims=True)) for h in hs]
    b_end = [jnp.sum(frow[h:h + 1, :], axis=-1, keepdims=True) for h in hs]
    a_col = [b_end[h] - bc[h] + icol[:, h:h + 1] for h in hs]
    m_new = [jnp.maximum(b_end[h] + m_st[h], jnp.max(a_col[h], axis=0, keepdims=True)) for h in hs]
    qk = [lax.dot_general(qb[h], k32[h].astype(BF16), (((1,), (1,)), ((), ())),
                          preferred_element_type=F32) for h in hs]
    qc = [jnp.dot(qb[h], c_st[h].astype(BF16), preferred_element_type=F32) for h in hs]
    s = [qk[h] * jnp.exp(log_d[h] - m_t[h]) for h in hs]
    sv = [jnp.dot(s[h].astype(BF16), vb[h], preferred_element_type=F32) for h in hs]
    wk = [jnp.exp(a_col[h] - m_new[h]) * k32[h] for h in hs]
    kv = [lax.dot_general(wk[h].astype(BF16), vb[h], (((0,), (0,)), ((), ())),
                          preferred_element_type=F32) for h in hs]
    w_inter = [jnp.exp(log_inter[h] - m_t[h]) for h in hs]
    qn = [jnp.sum(q32[h] * n_st[h], axis=-1, keepdims=True) for h in hs]
    ssum = [jnp.sum(s[h], axis=-1, keepdims=True) for h in hs]
    wsum = [jnp.sum(wk[h], axis=0, keepdims=True) for h in hs]
    dec = [jnp.exp(b_end[h] + m_st[h] - m_new[h]) for h in hs]
    den = [w_inter[h] * qn[h] + ssum[h] for h in hs]
    inv = [1.0 / jnp.maximum(jnp.abs(den[h]), jnp.exp(-m_t[h])) for h in hs]
    for h in hs:
        o_ref[0, :, h * dv:(h + 1) * dv] = (w_inter[h] * qc[h] + sv[h]) * inv[h]
    for h in hs:
        c_ref[h] = dec[h] * c_st[h] + kv[h]
        n_ref[h] = dec[h] * n_st[h] + wsum[h]
        m_ref[h] = jnp.broadcast_to(m_new[h], (1, LANES))


def _mlstm_scan(qk, u, v_col0, ig, lf, *, nb, n_ctx, n_lat):
    m = qk.shape[0]
    nh = MLSTM_HEADS
    dkt = qk.shape[1] // 2
    dk = dkt // nh
    dvt = 2 * dkt
    dv = dvt // nh
    ncc, ncl = n_ctx // CHUNK, n_lat // CHUNK
    nchunks = ncc + ncl
    ctx_blocks = nb * ncc
    assert v_col0 % dvt == 0

    def rmap(zd, bb, p):
        rev = jnp.where(p < ncc, ncc - 1 - p, nchunks - 1 - (p - ncc))
        ch = jnp.where(zd == 0, p, rev)
        return jnp.where(ch < ncc, bb * ncc + ch, ctx_blocks + bb * ncl + (ch - ncc))

    ig_row = jnp.swapaxes(ig.reshape(2, m // CHUNK, CHUNK, nh), 2, 3)
    lf_row = jnp.swapaxes(lf.reshape(2, m // CHUNK, CHUNK, nh), 2, 3)
    col_spec = pl.BlockSpec((1, CHUNK, nh), lambda zd, bb, p: (zd, rmap(zd, bb, p), 0))
    row_spec = pl.BlockSpec((1, 1, nh, CHUNK), lambda zd, bb, p: (zd, rmap(zd, bb, p), 0, 0))
    return pl.pallas_call(
        functools.partial(_mlstm_body, nh=nh, dk=dk, dv=dv),
        grid=(2, nb, nchunks),
        in_specs=[pl.BlockSpec((CHUNK, dkt), lambda zd, bb, p: (rmap(zd, bb, p), 0)),
                  pl.BlockSpec((CHUNK, dkt), lambda zd, bb, p: (rmap(zd, bb, p), 1)),
                  pl.BlockSpec((CHUNK, dvt), lambda zd, bb, p: (rmap(zd, bb, p), v_col0 // dvt)),
                  col_spec, col_spec, row_spec, row_spec],
        out_specs=pl.BlockSpec((1, CHUNK, dvt), lambda zd, bb, p: (zd, rmap(zd, bb, p), 0)),
        out_shape=jax.ShapeDtypeStruct((2, m, dvt), F32),
        scratch_shapes=[pltpu.VMEM((nh, dk, dv), F32), pltpu.VMEM((nh, 1, dk), F32),
                        pltpu.VMEM((nh, 1, LANES), F32)],
        compiler_params=_cparams(("parallel", "parallel", "arbitrary")),
        name="mlstm_scan",
    )(qk, qk, u, ig, lf, ig_row, lf_row)


class _Dims:
    def __init__(self, nb, n_ctx, n_lat, d):
        self.nb, self.n_ctx, self.n_lat, self.d = nb, n_ctx, n_lat, d
        self.mc, self.mx = nb * n_ctx, nb * n_lat
        self.m = self.mc + self.mx
        self.s_len = n_ctx + n_lat
        self.tm = 1024 if (self.mc % 1024 == 0 and n_lat % 1024 == 0) else n_ctx
        self.tm_half = max(self.tm // 2, n_ctx) if self.tm > n_ctx else self.tm
        assert self.mc % self.tm == 0 and n_lat % self.tm == 0 and n_ctx % GRID_W == 0

    def mod_index(self, i, tm):
        nbc = self.mc // tm
        bpb = self.n_lat // tm
        return jnp.where(i < nbc, self.nb, (i - nbc) // bpb)

    def mod_spec(self, comp, tm, width, with_col=False):
        return pl.BlockSpec((None, None, 1, width),
                            lambda i, *r: (self.mod_index(i, tm), comp, 0, r[0] if with_col else 0))

    def time_major_block(self, i):
        nbc = self.mc // self.n_ctx
        bpb = self.n_lat // self.n_ctx
        return (jnp.where(i < nbc, 0, 1 + (i - nbc) % bpb), jnp.where(i < nbc, i, (i - nbc) // bpb))

    def time_view(self, a):
        return a.reshape(self.s_len, self.nb, -1)

    def chan_to_scan(self, p):
        lead = p.shape[:-1]
        hpn = p.shape[-1] // (2 * RWKV_HEAD)
        pt = jnp.moveaxis(p.reshape(lead + (hpn, 2, RWKV_HEAD)), -3, -1)
        pt = jnp.broadcast_to(pt[..., None], lead + (2, RWKV_HEAD, hpn, LANES // hpn))
        return pt.reshape(lead + (2, RWKV_HEAD, LANES))


def _pad_to(a, axis, mult):
    pad = -a.shape[axis] % mult
    if pad == 0:
        return a
    widths = [(0, 0)] * a.ndim
    widths[axis] = (0, pad)
    return jnp.pad(a, widths)


def _rwkv_layer(dm, xs, g1, sc1, sh1, p, v_first):
    m, d, tm = dm.m, dm.d, dm.tm
    xr, xw, xk, xv, xa, xg = _rwkv_mix(xs, g1, sc1, sh1, p['mu'], dm=dm)
    mm = functools.partial(_matmul, m=m, tm=tm)
    mt = functools.partial(_mm_tmajor, nb=dm.nb, tms=2 * GRID_W)
    rows = lambda a: a.reshape(m, -1)
    r = mt(xr, p['w_r'], tn=512, name="mm_r")
    k = mt(xk, p['w_k'], tn=512, name="mm_k")
    v = rows(mt(xv, p['w_v'], tn=512, name="mm_v"))
    row_spec = pl.BlockSpec((tm, 512), lambda i, j, k: (i, j))
    vec_spec = pl.BlockSpec((1, 512), lambda i, j, k: (0, j))
    if v_first is not None:
        lv = rows(mt(xv, p['v1'], tn=LANES, name="mm_v1"))
        v = mm(lv, p['v2'], kdim=LANES, tn=512, name="mm_vres",
               epi=(_epi_vres, [v, v_first, p['v0']], [row_spec, row_spec, vec_spec]))
    lw = rows(mt(xw, p['w1'], tn=2 * LANES, act=jnp.tanh, name="mm_w1"))
    la = rows(mt(xa, p['a1'], tn=2 * LANES, name="mm_a1"))
    lg = mm(xg.reshape(m, d), p['g1'], kdim=d, tn=p['g1'].shape[1], out_dtype=BF16, name="mm_g1",
            epi=(jax.nn.sigmoid, [], []))
    g = mm(lg, p['g2'], kdim=p['g2'].shape[0], tn=512, name="mm_g2")
    wp = mm(lw, p['w2'], kdim=2 * LANES, tn=512, name="mm_w2")
    ap = mm(la, p['a2'], kdim=2 * LANES, tn=512, name="mm_a2")
    cs, hv = dm.chan_to_scan, dm.time_view
    y, rk = _rwkv_scan(hv(r), hv(k), hv(v), hv(wp), hv(ap),
                       cs(p['k_k']), cs(p['k_a']), cs(p['r_k']), cs(p['w0']), cs(p['a0']),
                       n_ctx=dm.n_ctx, tblk=32)
    pre = _rwkv_norm(y, hv(v), rk, cs(p['ln_w']), cs(p['ln_b']), tblk=32, eps=RWKV_HEAD * 1e-5)
    return pre, g, v


def _mlstm_layer(dm, xs, g1, sc1, sh1, p):
    m, d, tm = dm.m, dm.d, dm.tm
    nh = MLSTM_HEADS
    u = _matmul(None, p['w_in'], m=m, kdim=d, tm=tm, tn=MLSTM_IN_TN, name="mm_mlstm_in",
                pro=(_pro_normmod, [xs, g1, sc1[0], sh1[0]],
                     [pl.BlockSpec((tm, d), lambda i, j, k: (i, 0)),
                      pl.BlockSpec((1, d), lambda i, j, k: (0, 0)),
                      dm.mod_spec(sc1[1], tm, d), dm.mod_spec(sh1[1], tm, d)]))
    qk = _mlstm_conv(u, p['conv_w'], p['conv_b'], dm=dm, width=d, tc=min(1024, d // 2),
                     q_scale=float(d // 2 // nh) ** -0.5)
    gates = u[:, 3 * d:3 * d + 4 * nh].reshape(m, 2, 2, nh) + p['b_gate']
    gates = GATE_CAP * jnp.tanh(gates / GATE_CAP)
    ig = jnp.moveaxis(gates[:, :, 0, :], 1, 0)
    lf = jnp.moveaxis(jax.nn.log_sigmoid(gates[:, :, 1, :]), 1, 0)
    hs = _mlstm_scan(qk, u, d, ig, lf, nb=dm.nb, n_ctx=dm.n_ctx, n_lat=dm.n_lat)
    return hs, u


def kernel(x, c, ctx, c_ctx, mod_w, mod_b, norm_g, final_g, rwkv_mu, rwkv_w_r, rwkv_w_k, rwkv_w_v, rwkv_w_o, rwkv_w0, rwkv_w1, rwkv_w2, rwkv_a0, rwkv_a1, rwkv_a2, rwkv_g1, rwkv_g2, rwkv_k_k, rwkv_k_a, rwkv_r_k, rwkv_ln_w, rwkv_ln_b, rwkv_v0, rwkv_v1, rwkv_v2, mlstm_w_in, mlstm_b_gate, mlstm_conv_w, mlstm_conv_b, mlstm_norm_w, mlstm_w_out, ffn_w_in, ffn_w_out):
    nb, n_lat, d = x.shape
    n_ctx = ctx.shape[1]
    depth = mod_w.shape[0]
    d_ff = ffn_w_out.shape[1]
    dm = _Dims(nb, n_ctx, n_lat, d)
    m, tm, tmh = dm.m, dm.tm, dm.tm_half
    bf = lambda a: a.astype(BF16)

    cond = jax.nn.silu(jnp.concatenate([c, c_ctx[None, :]], axis=0))
    rows = cond.shape[0]
    cond = _pad_to(cond, 0, 2 * SUBLANES)
    mod_w2 = mod_w.reshape(depth * d, 6 * d)
    mods = []
    for i in range(depth):
        mo = _matmul(cond, mod_w2, m=cond.shape[0], kdim=d, tm=cond.shape[0], tn=512,
                     w_row0=i * d, name="mm_mod") + mod_b[i]
        mods.append(mo[:rows].reshape(rows, 6, 1, d))

    def gate_res(res, gate, tm_, tn_=512):
        return (_epi_gate_res, [res, gate[0]],
                [pl.BlockSpec((tm_, tn_), lambda i, j, k: (i, j)),
                 dm.mod_spec(gate[1], tm_, tn_, with_col=True)])

    xs = jnp.concatenate([ctx.reshape(dm.mc, d), x.reshape(dm.mx, d)], axis=0)
    v_first = None
    for i in range(depth):
        j = i // 2
        sh1, sc1, gt1, sh2, sc2, gt2 = [(mods[i], n) for n in range(6)]
        g1 = norm_g[i, 0][None, :]
        g2 = norm_g[i, 1][None, :]
        full_k = lambda tm_: pl.BlockSpec((tm_, d), lambda i, j, k: (i, 0))
        skip = (lambda tm_: dm.mc // tm_) if i == depth - 1 else (lambda tm_: 0)
        if i % 2 == 0:
            lora = lambda a: bf(jnp.concatenate([_pad_to(a[0], 1, LANES), _pad_to(a[1], 1, LANES)], axis=1))

            def lora_up(a):
                ap_ = _pad_to(a, 1, LANES)
                zero = jnp.zeros_like(ap_[0])
                return bf(jnp.concatenate([jnp.concatenate([ap_[0], zero], axis=1),
                                           jnp.concatenate([zero, ap_[1]], axis=1)], axis=0))

            p = {'mu': rwkv_mu[j], 'w_r': bf(rwkv_w_r[j]), 'w_k': bf(rwkv_w_k[j]), 'w_v': bf(rwkv_w_v[j]),
                 'w0': rwkv_w0[j], 'w1': lora(rwkv_w1[j]), 'w2': lora_up(rwkv_w2[j]),
                 'a0': rwkv_a0[j], 'a1': lora(rwkv_a1[j]), 'a2': lora_up(rwkv_a2[j]),
                 'g1': bf(rwkv_g1[j]), 'g2': bf(rwkv_g2[j]),
                 'k_k': rwkv_k_k[j], 'k_a': rwkv_k_a[j], 'r_k': rwkv_r_k[j].reshape(-1),
                 'ln_w': rwkv_ln_w[j], 'ln_b': rwkv_ln_b[j]}
            if j > 0:
                p['v0'] = rwkv_v0[j - 1][None, :]
                p['v1'] = bf(_pad_to(rwkv_v1[j - 1], 1, LANES))
                p['v2'] = bf(_pad_to(rwkv_v2[j - 1], 0, LANES))
            pre, g, v_cur = _rwkv_layer(dm, xs, g1, sc1, sh1, p, v_first if j > 0 else None)
            if j == 0:
                v_first = v_cur
            tmo = n_ctx
            tm_spec = pl.BlockSpec((None, tmo, d), lambda i, j, k: dm.time_major_block(i)[::-1] + (0,))
            xs, h2 = _matmul(None, bf(rwkv_w_o[j]), m=m, kdim=d, tm=tmo, tn=d, name="mm_rwkv_out",
                             row_block0=skip(tmo),
                             pro=(_pro_mulg, [pre, g.reshape(nb, dm.s_len, d)],
                                  [tm_spec, tm_spec]),
                             epi=gate_res(xs, gt1, tmo, d),
                             post=(_post_normmod, [g2, sc2[0], sh2[0]],
                                   [pl.BlockSpec((1, d), lambda i, j, k: (0, 0)),
                                    dm.mod_spec(sc2[1], tmo, d), dm.mod_spec(sh2[1], tmo, d)], BF16))
        else:
            p = {'w_in': bf(_pad_to(mlstm_w_in[j], 1, MLSTM_IN_TN)),
                 'b_gate': mlstm_b_gate[j], 'conv_w': mlstm_conv_w[j], 'conv_b': mlstm_conv_b[j]}
            hs, u = _mlstm_layer(dm, xs, g1, sc1, sh1, p)
            h2 = None
            xs = _matmul(None, bf(mlstm_w_out[j]), m=m, kdim=d, tm=tmh, tn=512, name="mm_mlstm_out",
                         row_block0=skip(tmh),
                         pro=(_pro_mlstm_read, [hs, u, mlstm_norm_w[j][None, :]],
                              [pl.BlockSpec((2, tmh, d), lambda i, j, k: (0, i, 0)),
                               pl.BlockSpec((tmh, d), lambda i, j, k: (i, 2)),
                               pl.BlockSpec((1, d), lambda i, j, k: (0, 0))]),
                         epi=gate_res(xs, gt1, tmh))
        ffn_pro = None if h2 is not None else (
            _pro_normmod, [xs, g2, sc2[0], sh2[0]],
            [full_k(tm), pl.BlockSpec((1, d), lambda i, j, k: (0, 0)),
             dm.mod_spec(sc2[1], tm, d), dm.mod_spec(sh2[1], tm, d)])
        hid = _matmul(h2, bf(ffn_w_in[i]), m=m, kdim=d, tm=tm, tn=512, out_dtype=BF16,
                      n_out=d_ff, w2_col0=d_ff, name="mm_ffn_in", row_block0=skip(tm),
                      pro=ffn_pro, epi=(_epi_swiglu, [], []))
        xs = _matmul(hid, bf(ffn_w_out[i]), m=m, kdim=d_ff, tm=tm, tn=512, row_block0=skip(tm),
                     name="mm_ffn_out", epi=gate_res(xs, gt2, tm))
    out = _rmsnorm_rows(xs, final_g[None, :], tm=tm, row_block0=dm.mc // tm)
    return out.reshape(nb, n_lat, d)
```

```python
import functools

import jax
import jax.numpy as jnp
from jax import lax
from jax.experimental import pallas as pl
from jax.experimental.pallas import tpu as pltpu

F32 = jnp.float32
BF16 = jnp.bfloat16

GRID_W = 64
NORM_EPS = 1e-6
RWKV_HEAD = 64
MLSTM_HEADS = 8
CHUNK = 64
GATE_CAP = 15.0
LANES = 128
SUBLANES = 8
VMEM_LIMIT = 56 * 1024 * 1024
MLSTM_IN_TN = 1280


def _cparams(sem):
    return pltpu.CompilerParams(dimension_semantics=sem, vmem_limit_bytes=VMEM_LIMIT)


def _rms_mod(x, g, sc, sh):
    ms = jnp.mean(x * x, axis=-1, keepdims=True)
    return x * lax.rsqrt(ms + NORM_EPS) * g * (1.0 + sc) + sh


def _head_ln_lanes(y, nheads, eps):
    hd = y.shape[-1] // nheads
    out = []
    for h in range(nheads):
        seg = y[:, h * hd:(h + 1) * hd]
        mu = jnp.mean(seg, axis=-1, keepdims=True)
        var = jnp.mean(jnp.square(seg - mu), axis=-1, keepdims=True)
        out.append((seg - mu) * lax.rsqrt(var + eps))
    return jnp.concatenate(out, axis=-1)


def _pro_normmod(x_ref, g_ref, sc_ref, sh_ref):
    return _rms_mod(x_ref[...].astype(F32), g_ref[...], sc_ref[...], sh_ref[...]).astype(BF16)


def _pro_mulg(x_ref, g_ref):
    return (x_ref[...] * g_ref[...]).astype(BF16)


def _pro_mlstm_read(h_ref, o_ref, nw_ref):
    hn = _head_ln_lanes(h_ref[0] + h_ref[1], MLSTM_HEADS, NORM_EPS) * nw_ref[...]
    return (hn * jax.nn.sigmoid(o_ref[...])).astype(BF16)


def _post_normmod(y, g_ref, sc_ref, sh_ref):
    return _rms_mod(y, g_ref[...], sc_ref[...], sh_ref[...])


def _epi_swiglu(acc, acc2):
    return acc * jax.nn.sigmoid(acc) * acc2


def _epi_gate_res(acc, res_ref, gate_ref):
    return res_ref[...] + gate_ref[...] * acc


def _epi_vres(acc, v_ref, vf_ref, v0_ref):
    v = v_ref[...]
    return v + (vf_ref[...] - v) * jax.nn.sigmoid(v0_ref[...] + acc)


def _mm_body(*refs, nk, n_pro, pro_fn, n_epi, epi_fn, dual, n_post, post_fn):
    it = iter(refs)
    pro_refs = [next(it) for _ in range(n_pro)]
    w_ref = next(it)
    w2_ref = next(it) if dual else None
    epi_refs = [next(it) for _ in range(n_epi)]
    post_refs = [next(it) for _ in range(n_post)]
    o_ref = next(it)
    o2_ref = next(it) if post_fn is not None else None
    acc_ref = next(it)
    acc2_ref = next(it) if dual else None
    xb_ref = next(it) if pro_fn is not None else None

    j = pl.program_id(1)
    k = pl.program_id(2)

    if pro_fn is not None:
        @pl.when(j == 0)
        def _():
            xb_ref[...] = pro_fn(*pro_refs)
        xb = xb_ref[...]
    else:
        xb = pro_refs[0][...].astype(BF16)

    @pl.when(k == 0)
    def _():
        acc_ref[...] = jnp.zeros_like(acc_ref)
        if dual:
            acc2_ref[...] = jnp.zeros_like(acc2_ref)

    acc_ref[...] += jnp.dot(xb, w_ref[...].astype(BF16), preferred_element_type=F32)
    if dual:
        acc2_ref[...] += jnp.dot(xb, w2_ref[...].astype(BF16), preferred_element_type=F32)

    @pl.when(k == nk - 1)
    def _():
        acc = acc_ref[...]
        if dual:
            acc = epi_fn(acc, acc2_ref[...], *epi_refs)
        elif epi_fn is not None:
            acc = epi_fn(acc, *epi_refs)
        o_ref[...] = acc.astype(o_ref.dtype)
        if post_fn is not None:
            o2_ref[...] = post_fn(acc, *post_refs).astype(o2_ref.dtype)


def _matmul(x, w, *, m, kdim, tm, tn, tk=None, out_dtype=F32, n_out=None, w_col0=0, w2_col0=None,
            w_row0=0, x_col0=0, row_block0=0, pro=None, epi=None, post=None, name="mm"):
    n_out = w.shape[1] if n_out is None else n_out
    tk = kdim if tk is None else tk
    dual = w2_col0 is not None
    assert m % tm == 0 and n_out % tn == 0 and kdim % tk == 0
    assert w_col0 % tn == 0 and w_row0 % tk == 0 and x_col0 % tk == 0
    nk = kdim // tk
    c0, r0, xc0 = w_col0 // tn, w_row0 // tk, x_col0 // tk
    pro_fn, pro_args, pro_specs = pro if pro is not None else (None, [x], [
        pl.BlockSpec((tm, tk), lambda i, j, k: (i, k + xc0))])
    if pro_fn is not None:
        assert nk == 1
    epi_fn, epi_args, epi_specs = epi if epi is not None else (None, [], [])
    in_specs = list(pro_specs) + [pl.BlockSpec((tk, tn), lambda i, j, k: (k + r0, j + c0))]
    args = list(pro_args) + [w]
    if dual:
        assert w2_col0 % tn == 0
        c2 = w2_col0 // tn
        in_specs.append(pl.BlockSpec((tk, tn), lambda i, j, k: (k + r0, j + c2)))
        args.append(w)
    in_specs += list(epi_specs)
    args += list(epi_args)
    post_fn, post_args, post_specs, post_dtype = post if post is not None else (None, [], [], None)
    if post_fn is not None:
        assert tn == n_out and nk == 1
    in_specs += list(post_specs)
    args += list(post_args)
    scratch = [pltpu.VMEM((tm, tn), F32)]
    if dual:
        scratch.append(pltpu.VMEM((tm, tn), F32))
    if pro_fn is not None:
        scratch.append(pltpu.VMEM((tm, tk), BF16))
    out_spec = pl.BlockSpec((tm, tn), lambda i, j, k: (i, j))
    if row_block0:
        shift = lambda sp: pl.BlockSpec(sp.block_shape,
                                        lambda i, j, k: sp.index_map(i + row_block0, j, k))
        in_specs = [shift(sp) for sp in in_specs]
        out_spec = shift(out_spec)
    out_specs, out_shape = out_spec, jax.ShapeDtypeStruct((m, n_out), out_dtype)
    if post_fn is not None:
        out_specs, out_shape = [out_spec, out_spec], [out_shape, jax.ShapeDtypeStruct((m, n_out), post_dtype)]
    return pl.pallas_call(
        functools.partial(_mm_body, nk=nk, n_pro=len(pro_args), pro_fn=pro_fn,
                          n_epi=len(epi_args), epi_fn=epi_fn, dual=dual,
                          n_post=len(post_args), post_fn=post_fn),
        grid=(m // tm - row_block0, n_out // tn, nk),
        in_specs=in_specs,
        out_specs=out_specs,
        out_shape=out_shape,
        scratch_shapes=scratch,
        compiler_params=_cparams(("parallel", "arbitrary", "arbitrary")),
        name=name,
    )(*args)


def _rmsnorm_body(x_ref, g_ref, o_ref):
    x = x_ref[...]
    ms = jnp.mean(x * x, axis=-1, keepdims=True)
    o_ref[...] = x * lax.rsqrt(ms + NORM_EPS) * g_ref[...]


def _rmsnorm_rows(x, g, *, tm, row_block0):
    m, d = x.shape
    nblk = m // tm - row_block0
    return pl.pallas_call(
        _rmsnorm_body,
        grid=(nblk,),
        in_specs=[pl.BlockSpec((tm, d), lambda i: (i + row_block0, 0)),
                  pl.BlockSpec((1, d), lambda i: (0, 0))],
        out_specs=pl.BlockSpec((tm, d), lambda i: (i, 0)),
        out_shape=jax.ShapeDtypeStruct((nblk * tm, d), x.dtype),
        compiler_params=_cparams(("parallel",)),
        name="final_norm",
    )(x, g)


def _mm_tmajor_body(x_ref, w_ref, o_ref, *, nbg, kdim, tms, act):
    xb = jnp.concatenate([x_ref[:, b * kdim:(b + 1) * kdim] for b in range(nbg)], axis=0)
    acc = jnp.dot(xb, w_ref[...], preferred_element_type=F32)
    if act is not None:
        acc = act(acc)
    for b in range(nbg):
        o_ref[:, b, :] = acc[b * tms:(b + 1) * tms].astype(o_ref.dtype)


def _mm_tmajor(x, w, *, nb, tms, tn, act=None, name="mm_tmajor"):
    s_len = x.shape[0]
    kdim, n = w.shape
    nbg = SUBLANES
    assert x.shape[1] == nb * kdim and nb % nbg == 0 and s_len % tms == 0 and n % tn == 0
    return pl.pallas_call(
        functools.partial(_mm_tmajor_body, nbg=nbg, kdim=kdim, tms=tms, act=act),
        grid=(s_len // tms, nb // nbg, n // tn),
        in_specs=[pl.BlockSpec((tms, nbg * kdim), lambda i, g, j: (i, g)),
                  pl.BlockSpec((kdim, tn), lambda i, g, j: (0, j))],
        out_specs=pl.BlockSpec((tms, nbg, tn), lambda i, g, j: (i, g, j)),
        out_shape=jax.ShapeDtypeStruct((s_len, nb, n), F32),
        compiler_params=_cparams(("parallel", "parallel", "arbitrary")),
        name=name,
    )(x, w)


def _halo_specs(tm, width, col_map, m):
    per = tm // GRID_W
    last = m // GRID_W - 1
    return [pl.BlockSpec((tm, width), lambda i, *j: (i, col_map(*j))),
            pl.BlockSpec((GRID_W, width), lambda i, *j: (jnp.maximum(i * per - 1, 0), col_map(*j))),
            pl.BlockSpec((GRID_W, width), lambda i, *j: (jnp.minimum((i + 1) * per, last), col_map(*j)))]


def _mix_body(x_ref, up_ref, dn_ref, g_ref, sc_ref, sh_ref, mu_ref, *out_refs, tm, nbc, bpb, d):
    i = pl.program_id(0)
    g, sc, sh = g_ref[...], sc_ref[...], sh_ref[...]
    h = _rms_mod(x_ref[...].astype(F32), g, sc, sh)
    row = lax.broadcasted_iota(jnp.int32, (tm, 1), 0)

    def emit(lo, hi, shifted):
        hseg = h[:, lo:hi]
        xx = shifted - hseg
        for n, o_ref in enumerate(out_refs):
            o_ref[:, lo:hi] = (hseg + xx * mu_ref[n:n + 1, lo:hi]).astype(o_ref.dtype)

    @pl.when(i < nbc)
    def _():
        half = d // 2
        emit(0, half, jnp.where(row == 0, 0.0, pltpu.roll(h[:, :half], 1, axis=0)))
        emit(half, d, jnp.where(row == tm - 1, 0.0, pltpu.roll(h[:, half:], tm - 1, axis=0)))

    @pl.when(i >= nbc)
    def _():
        jb = (i - nbc) % bpb
        q = d // 4
        col = row % GRID_W
        emit(0, q, jnp.where(col == 0, 0.0, pltpu.roll(h[:, :q], 1, axis=0)))
        emit(q, 2 * q, jnp.where(col == GRID_W - 1, 0.0, pltpu.roll(h[:, q:2 * q], tm - 1, axis=0)))
        hu = _rms_mod(up_ref[...].astype(F32), g, sc, sh)[:, 2 * q:3 * q]
        hu = hu * jnp.where(jb > 0, 1.0, 0.0)
        emit(2 * q, 3 * q, jnp.concatenate([hu, h[:tm - GRID_W, 2 * q:3 * q]], axis=0))
        hd = _rms_mod(dn_ref[...].astype(F32), g, sc, sh)[:, 3 * q:]
        hd = hd * jnp.where(jb < bpb - 1, 1.0, 0.0)
        emit(3 * q, d, jnp.concatenate([h[GRID_W:, 3 * q:], hd], axis=0))


def _rwkv_mix(xs, g, sc, sh, mu, *, dm):
    m, d = xs.shape
    tm = dm.n_ctx
    nbc, bpb = dm.mc // tm, dm.n_lat // tm
    sc_spec, sh_spec = dm.mod_spec(sc[1], tm, d), dm.mod_spec(sh[1], tm, d)
    tm_spec = pl.BlockSpec((tm, d), lambda i: dm.time_major_block(i))
    bm_spec = pl.BlockSpec((None, tm, d), lambda i: dm.time_major_block(i)[::-1] + (0,))
    return pl.pallas_call(
        functools.partial(_mix_body, tm=tm, nbc=nbc, bpb=bpb, d=d),
        grid=(m // tm,),
        in_specs=_halo_specs(tm, d, lambda: 0, m) + [
            pl.BlockSpec((1, d), lambda i: (0, 0)), sc_spec, sh_spec,
            pl.BlockSpec(mu.shape, lambda i: (0, 0))],
        out_specs=[tm_spec] * 5 + [bm_spec],
        out_shape=[jax.ShapeDtypeStruct((dm.s_len, dm.nb * d), BF16)] * 5
        + [jax.ShapeDtypeStruct((dm.nb, dm.s_len, d), BF16)],
        compiler_params=_cparams(("parallel",)),
        name="rwkv_mix",
    )(xs, xs, xs, g, sc[0], sh[0], mu)


def _conv_body(x_ref, up_ref, dn_ref, w_ref, b_ref, o_ref, *, tm, nbc, bpb, n_qcols, q_scale):
    i = pl.program_id(0)
    j = pl.program_id(1)
    row = lax.broadcasted_iota(jnp.int32, (tm, 1), 0)
    x = x_ref[...]
    scale = jnp.where(j < n_qcols, q_scale, 1.0)

    def finish(y):
        y = y + b_ref[...]
        o_ref[...] = (y * jax.nn.sigmoid(y) * scale).astype(o_ref.dtype)

    @pl.when(i < nbc)
    def _():
        prev = jnp.where(row == 0, 0.0, pltpu.roll(x, 1, axis=0))
        nxt = jnp.where(row == tm - 1, 0.0, pltpu.roll(x, tm - 1, axis=0))
        finish(prev * w_ref[3:4, :] + x * w_ref[4:5, :] + nxt * w_ref[5:6, :])

    @pl.when(i >= nbc)
    def _():
        jb = (i - nbc) % bpb
        te = tm + 2 * GRID_W
        ext = jnp.concatenate([up_ref[...] * jnp.where(jb > 0, 1.0, 0.0), x,
                               dn_ref[...] * jnp.where(jb < bpb - 1, 1.0, 0.0)], axis=0)
        col = lax.broadcasted_iota(jnp.int32, (te, 1), 0) % GRID_W
        taps = (jnp.where(col == 0, 0.0, pltpu.roll(ext, 1, axis=0)), ext,
                jnp.where(col == GRID_W - 1, 0.0, pltpu.roll(ext, te - 1, axis=0)))
        y = jnp.zeros((tm, x.shape[1]), F32)
        for di in range(3):
            for dj in range(3):
                y = y + taps[dj][di * GRID_W:di * GRID_W + tm] * w_ref[3 * di + dj:3 * di + dj + 1, :]
        finish(y)


def _mlstm_conv(u, conv_w, conv_b, *, dm, width, tc, q_scale):
    m = u.shape[0]
    tm = dm.n_ctx
    nbc, bpb = dm.mc // tm, dm.n_lat // tm
    w9 = conv_w.reshape(9, width)
    return pl.pallas_call(
        functools.partial(_conv_body, tm=tm, nbc=nbc, bpb=bpb, n_qcols=width // 2 // tc,
                          q_scale=q_scale),
        grid=(m // tm, width // tc),
        in_specs=_halo_specs(tm, tc, lambda j: j, m) + [
            pl.BlockSpec((9, tc), lambda i, j: (0, j)), pl.BlockSpec((1, tc), lambda i, j: (0, j))],
        out_specs=pl.BlockSpec((tm, tc), lambda i, j: (i, j)),
        out_shape=jax.ShapeDtypeStruct((m, width), F32),
        compiler_params=_cparams(("parallel", "parallel")),
        name="mlstm_conv",
    )(u, u, u, w9, conv_b[None, :])


def _split_parity(ref, t, n):
    x = ref[t]
    hpn = x.shape[1] // (2 * n)
    rows = jnp.concatenate([x[:, hp * 2 * n:(hp + 1) * 2 * n] for hp in range(hpn)], axis=0)
    xt = rows.T
    return xt[:n], xt[n:]


def _rwkv_scan_body(r_ref, k_ref, v_ref, wp_ref, ap_ref, kk_ref, ka_ref, rkp_ref, w0_ref, a0_ref,
                    y_ref, rko_ref, s_ref, sz_ref, gam_s, r_s, v_s, k_s, z_s, b_s, *, tblk, n):
    zdir = pl.program_id(0)
    tb = pl.program_id(2)

    @pl.when(tb == 0)
    def _():
        s_ref[...] = jnp.zeros_like(s_ref)

    bwd = zdir == 1
    t_first = jnp.where(bwd, tblk - 1, 0)
    t_step = jnp.where(bwd, -1, 1)

    def prep(i, gam):
        t = t_first + i * t_step
        r2, k2, v2 = _split_parity(r_ref, t, n), _split_parity(k_ref, t, n), _split_parity(v_ref, t, n)
        wp2, ap2 = _split_parity(wp_ref, t, n), _split_parity(ap_ref, t, n)
        out = []
        for g in range(2):
            k_t = k2[g]
            a = jax.nn.sigmoid(a0_ref[g] + ap2[g])
            gcur = gam[g] * jnp.exp(-jnp.exp(-0.5) * jax.nn.sigmoid(w0_ref[g] + wp2[g]))
            inv = 1.0 / gcur
            kr = k_t * kk_ref[g]
            nrm = jnp.sqrt(jnp.sum(kr * kr, axis=0, keepdims=True))
            kkn = kr / jnp.maximum(nrm, 1e-12)
            z_s[t, g] = -kkn * gam[g]
            b_s[t, g] = kkn * a * inv
            km = k_t * (1.0 + (a - 1.0) * ka_ref[g])
            k_s[t, g] = km * inv
            r_s[t, g] = r2[g] * gcur
            v_s[t, g] = v2[g]
            rko_ref[t, g] = jnp.sum(r2[g] * km * rkp_ref[g], axis=0, keepdims=True)
            out.append(gcur)
        return tuple(out)

    ones = jnp.ones((n, LANES), F32)
    gam_end = lax.fori_loop(0, tblk, prep, (ones, ones), unroll=8)
    for g in range(2):
        gam_s[g] = gam_end[g]

    def row(ref, t, g, kk):
        return jnp.broadcast_to(ref[t, g, pl.ds(kk, 1), :], (n, LANES))

    for g in range(2):
        acc = jnp.zeros((n, LANES), F32)
        for kk in range(n):
            acc = acc + s_ref[g, kk] * row(z_s, t_first, g, kk)
        sz_ref[g] = acc

    def step(i, carry):
        t = t_first + i * t_step
        t_next = jnp.clip(t + t_step, 0, tblk - 1)
        for g in range(2):
            sz = sz_ref[g]
            v_t = v_s[t, g]
            y = jnp.zeros((n, LANES), F32)
            sz_next = jnp.zeros((n, LANES), F32)
            for kk in range(n):
                s_new = s_ref[g, kk] + sz * row(b_s, t, g, kk) + v_t * row(k_s, t, g, kk)
                s_ref[g, kk] = s_new
                y = y + s_new * row(r_s, t, g, kk)
                sz_next = sz_next + s_new * row(z_s, t_next, g, kk)
            y_ref[t, g] = y
            sz_ref[g] = sz_next
        return carry

    lax.fori_loop(0, tblk, step, 0)

    for g in range(2):
        for kk in range(n):
            s_ref[g, kk] = s_ref[g, kk] * jnp.broadcast_to(gam_s[g, pl.ds(kk, 1), :], (n, LANES))


def _rwkv_scan(r, k, v, wp, ap, kk_p, ka_p, rk_p, w0_p, a0_p, *, n_ctx, tblk):
    s_len, nbat, d = r.shape
    n = RWKV_HEAD
    nbg = LANES // (d // (2 * n))
    assert 2 * n == LANES and nbat % nbg == 0 and n_ctx % tblk == 0 and s_len % tblk == 0
    ngrp = nbat // nbg
    nbc, nb = n_ctx // tblk, s_len // tblk

    def tmap(zd, p):
        rev = jnp.where(p < nbc, nbc - 1 - p, nb - 1 - (p - nbc))
        return jnp.where(zd == 0, p, rev)

    shared = pl.BlockSpec((tblk, nbg, d), lambda zd, g, p: (tmap(zd, p), g, 0))
    perdir = pl.BlockSpec((tblk, nbg, d), lambda zd, g, p: (tmap(zd, p), g, zd))
    par = pl.BlockSpec((2, n, LANES), lambda zd, g, p: (0, 0, 0))
    par_dir = pl.BlockSpec((None, 2, n, LANES), lambda zd, g, p: (zd, 0, 0, 0))
    step_buf = pltpu.VMEM((tblk, 2, n, LANES), F32)
    return pl.pallas_call(
        functools.partial(_rwkv_scan_body, tblk=tblk, n=n),
        grid=(2, ngrp, nb),
        in_specs=[shared, shared, shared, perdir, perdir, par, par, par, par_dir, par_dir],
        out_specs=[pl.BlockSpec((None, tblk, 2, n, LANES), lambda zd, g, p: (zd, tmap(zd, p), g, 0, 0)),
                   pl.BlockSpec((None, tblk, 2, 1, LANES), lambda zd, g, p: (zd, tmap(zd, p), g, 0, 0))],
        out_shape=[jax.ShapeDtypeStruct((2, s_len, 2 * ngrp, n, LANES), F32),
                   jax.ShapeDtypeStruct((2, s_len, 2 * ngrp, 1, LANES), F32)],
        scratch_shapes=[pltpu.VMEM((2, n, n, LANES), F32), pltpu.VMEM((2, n, LANES), F32),
                        pltpu.VMEM((2, n, LANES), F32)] + [step_buf] * 5,
        compiler_params=_cparams(("parallel", "parallel", "arbitrary")),
        name="rwkv_scan",
    )(r, k, v, wp, ap, kk_p, ka_p, rk_p, w0_p, a0_p)


def _rwkv_norm_body(y_ref, v_ref, rk_ref, lw_ref, lb_ref, o_ref, *, eps, tblk, n):
    def step(t, carry):
        v2 = _split_parity(v_ref, t, n)
        outs = []
        for g in range(2):
            ys = y_ref[0, t, g] + y_ref[1, t, g]
            mu = jnp.mean(ys, axis=0, keepdims=True)
            var = jnp.mean(jnp.square(ys - mu), axis=0, keepdims=True)
            yn = (ys - mu) * lax.rsqrt(var + eps) * lw_ref[g] + lb_ref[g]
            outs.append(yn + (rk_ref[0, t, g] + rk_ref[1, t, g]) * v2[g])
        rows = jnp.concatenate(outs, axis=0).T
        nbg = o_ref.shape[0]
        o_ref[:, t, :] = jnp.concatenate([rows[hp * nbg:(hp + 1) * nbg] for hp in range(LANES // nbg)],
                                         axis=1)
        return carry

    lax.fori_loop(0, tblk, step, 0, unroll=4)


def _rwkv_norm(y, v, rk, ln_w_p, ln_b_p, *, tblk, eps):
    s_len, nbat, d = v.shape
    n = RWKV_HEAD
    nbg = LANES // (d // (2 * n))
    par = pl.BlockSpec((2, n, LANES), lambda p, g: (0, 0, 0))
    nat = pl.BlockSpec((tblk, nbg, d), lambda p, g: (p, g, 0))
    return pl.pallas_call(
        functools.partial(_rwkv_norm_body, eps=eps, tblk=tblk, n=n),
        grid=(s_len // tblk, nbat // nbg),
        in_specs=[pl.BlockSpec((2, tblk, 2, n, LANES), lambda p, g: (0, p, g, 0, 0)), nat,
                  pl.BlockSpec((2, tblk, 2, 1, LANES), lambda p, g: (0, p, g, 0, 0)), par, par],
        out_specs=pl.BlockSpec((nbg, tblk, d), lambda p, g: (g, p, 0)),
        out_shape=jax.ShapeDtypeStruct((nbat, s_len, d), F32),
        compiler_params=_cparams(("parallel", "parallel")),
        name="rwkv_norm",
    )(y, v, rk, ln_w_p, ln_b_p)


def _mlstm_body(q_ref, k_ref, v_ref, ic_ref, fc_ref, ir_ref, fr_ref, o_ref,
                c_ref, n_ref, m_ref, *, nh, dk, dv):
    zdir = pl.program_id(0)
    p = pl.program_id(2)

    @pl.when(p == 0)
    def _():
        c_ref[...] = jnp.zeros_like(c_ref)
        n_ref[...] = jnp.zeros_like(n_ref)
        m_ref[...] = jnp.zeros_like(m_ref)

    ti = lax.broadcasted_iota(jnp.int32, (CHUNK, CHUNK), 0)
    tj = lax.broadcasted_iota(jnp.int32, (CHUNK, CHUNK), 1)
    mask = (ti - tj) * (1 - 2 * zdir) >= 0
    tri = mask.astype(F32)
    fcol = fc_ref[0]
    icol = ic_ref[0]
    frow = fr_ref[0, 0]
    irow = ir_ref[0, 0]
    hi = lax.Precision.HIGHEST
    bcum_col = jnp.dot(tri, fcol, precision=hi, preferred_element_type=F32)
    bcum_row = lax.dot_general(frow, tri, (((1,), (1,)), ((), ())), precision=hi,
                               preferred_element_type=F32)

    hs = range(nh)
    q32 = [q_ref[:, h * dk:(h + 1) * dk].astype(F32) for h in hs]
    qb = [x.astype(BF16) for x in q32]
    k32 = [k_ref[:, h * dk:(h + 1) * dk].astype(F32) for h in hs]
    vb = [v_ref[:, h * dv:(h + 1) * dv].astype(BF16) for h in hs]
    c_st = [c_ref[h] for h in hs]
    n_st = [n_ref[h] for h in hs]
    m_st = [m_ref[h][:, :1] for h in hs]
    bc = [bcum_col[:, h:h + 1] for h in hs]
    log_d = [jnp.where(mask, bc[h] - bcum_row[h:h + 1, :] + irow[h:h + 1, :], -jnp.inf) for h in hs]
    log_inter = [bc[h] + m_st[h] for h in hs]
    m_t = [jnp.maximum(log_inter[h], jnp.max(log_d[h], axis=-1, keepdims=True)) for h in hs]
    b_end = [jnp.sum(frow[h:h + 1, :], axis=-1, keepdims=True) for h in hs]
    a_col = [b_end[h] - bc[h] + icol[:, h:h + 1] for h in hs]
    m_new = [jnp.maximum(b_end[h] + m_st[h], jnp.max(a_col[h], axis=0, keepdims=True)) for h in hs]
    qk = [lax.dot_general(qb[h], k32[h].astype(BF16), (((1,), (1,)), ((), ())),
                          preferred_element_type=F32) for h in hs]
    qc = [jnp.dot(qb[h], c_st[h].astype(BF16), preferred_element_type=F32) for h in hs]
    s = [qk[h] * jnp.exp(log_d[h] - m_t[h]) for h in hs]
    sv = [jnp.dot(s[h].astype(BF16), vb[h], preferred_element_type=F32) for h in hs]
    wk = [jnp.exp(a_col[h] - m_new[h]) * k32[h] for h in hs]
    kv = [lax.dot_general(wk[h].astype(BF16), vb[h], (((0,), (0,)), ((), ())),
                          preferred_element_type=F32) for h in hs]
    w_inter = [jnp.exp(log_inter[h] - m_t[h]) for h in hs]
    qn = [jnp.sum(q32[h] * n_st[h], axis=-1, keepdims=True) for h in hs]
    ssum = [jnp.sum(s[h], axis=-1, keepdims=True) for h in hs]
    wsum = [jnp.sum(wk[h], axis=0, keepdims=True) for h in hs]
    dec = [jnp.exp(b_end[h] + m_st[h] - m_new[h]) for h in hs]
    den = [w_inter[h] * qn[h] + ssum[h] for h in hs]
    inv = [1.0 / jnp.maximum(jnp.abs(den[h]), jnp.exp(-m_t[h])) for h in hs]
    for h in hs:
        o_ref[0, :, h * dv:(h + 1) * dv] = (w_inter[h] * qc[h] + sv[h]) * inv[h]
    for h in hs:
        c_ref[h] = dec[h] * c_st[h] + kv[h]
        n_ref[h] = dec[h] * n_st[h] + wsum[h]
        m_ref[h] = jnp.broadcast_to(m_new[h], (1, LANES))


def _mlstm_scan(qk, u, v_col0, ig, lf, *, nb, n_ctx, n_lat):
    m = qk.shape[0]
    nh = MLSTM_HEADS
    dkt = qk.shape[1] // 2
    dk = dkt // nh
    dvt = 2 * dkt
    dv = dvt // nh
    ncc, ncl = n_ctx // CHUNK, n_lat // CHUNK
    nchunks = ncc + ncl
    ctx_blocks = nb * ncc
    assert v_col0 % dvt == 0

    def rmap(zd, bb, p):
        rev = jnp.where(p < ncc, ncc - 1 - p, nchunks - 1 - (p - ncc))
        ch = jnp.where(zd == 0, p, rev)
        return jnp.where(ch < ncc, bb * ncc + ch, ctx_blocks + bb * ncl + (ch - ncc))

    ig_row = jnp.swapaxes(ig.reshape(2, m // CHUNK, CHUNK, nh), 2, 3)
    lf_row = jnp.swapaxes(lf.reshape(2, m // CHUNK, CHUNK, nh), 2, 3)
    col_spec = pl.BlockSpec((1, CHUNK, nh), lambda zd, bb, p: (zd, rmap(zd, bb, p), 0))
    row_spec = pl.BlockSpec((1, 1, nh, CHUNK), lambda zd, bb, p: (zd, rmap(zd, bb, p), 0, 0))
    return pl.pallas_call(
        functools.partial(_mlstm_body, nh=nh, dk=dk, dv=dv),
        grid=(2, nb, nchunks),
        in_specs=[pl.BlockSpec((CHUNK, dkt), lambda zd, bb, p: (rmap(zd, bb, p), 0)),
                  pl.BlockSpec((CHUNK, dkt), lambda zd, bb, p: (rmap(zd, bb, p), 1)),
                  pl.BlockSpec((CHUNK, dvt), lambda zd, bb, p: (rmap(zd, bb, p), v_col0 // dvt)),
                  col_spec, col_spec, row_spec, row_spec],
        out_specs=pl.BlockSpec((1, CHUNK, dvt), lambda zd, bb, p: (zd, rmap(zd, bb, p), 0)),
        out_shape=jax.ShapeDtypeStruct((2, m, dvt), F32),
        scratch_shapes=[pltpu.VMEM((nh, dk, dv), F32), pltpu.VMEM((nh, 1, dk), F32),
                        pltpu.VMEM((nh, 1, LANES), F32)],
        compiler_params=_cparams(("parallel", "parallel", "arbitrary")),
        name="mlstm_scan",
    )(qk, qk, u, ig, lf, ig_row, lf_row)


class _Dims:
    def __init__(self, nb, n_ctx, n_lat, d):
        self.nb, self.n_ctx, self.n_lat, self.d = nb, n_ctx, n_lat, d
        self.mc, self.mx = nb * n_ctx, nb * n_lat
        self.m = self.mc + self.mx
        self.s_len = n_ctx + n_lat
        self.tm = 1024 if (self.mc % 1024 == 0 and n_lat % 1024 == 0) else n_ctx
        self.tm_half = max(self.tm // 2, n_ctx) if self.tm > n_ctx else self.tm
        assert self.mc % self.tm == 0 and n_lat % self.tm == 0 and n_ctx % GRID_W == 0

    def mod_index(self, i, tm):
        nbc = self.mc // tm
        bpb = self.n_lat // tm
        return jnp.where(i < nbc, self.nb, (i - nbc) // bpb)

    def mod_spec(self, comp, tm, width, with_col=False):
        return pl.BlockSpec((None, None, 1, width),
                            lambda i, *r: (self.mod_index(i, tm), comp, 0, r[0] if with_col else 0))

    def time_major_block(self, i):
        nbc = self.mc // self.n_ctx
        bpb = self.n_lat // self.n_ctx
        return (jnp.where(i < nbc, 0, 1 + (i - nbc) % bpb), jnp.where(i < nbc, i, (i - nbc) // bpb))

    def time_view(self, a):
        return a.reshape(self.s_len, self.nb, -1)

    def chan_to_scan(self, p):
        lead = p.shape[:-1]
        hpn = p.shape[-1] // (2 * RWKV_HEAD)
        pt = jnp.moveaxis(p.reshape(lead + (hpn, 2, RWKV_HEAD)), -3, -1)
        pt = jnp.broadcast_to(pt[..., None], lead + (2, RWKV_HEAD, hpn, LANES // hpn))
        return pt.reshape(lead + (2, RWKV_HEAD, LANES))


def _pad_to(a, axis, mult):
    pad = -a.shape[axis] % mult
    if pad == 0:
        return a
    widths = [(0, 0)] * a.ndim
    widths[axis] = (0, pad)
    return jnp.pad(a, widths)


def _rwkv_layer(dm, xs, g1, sc1, sh1, p, v_first):
    m, d, tm = dm.m, dm.d, dm.tm
    xr, xw, xk, xv, xa, xg = _rwkv_mix(xs, g1, sc1, sh1, p['mu'], dm=dm)
    mm = functools.partial(_matmul, m=m, tm=tm)
    mt = functools.partial(_mm_tmajor, nb=dm.nb, tms=2 * GRID_W)
    rows = lambda a: a.reshape(m, -1)
    r = mt(xr, p['w_r'], tn=512, name="mm_r")
    k = mt(xk, p['w_k'], tn=512, name="mm_k")
    v = rows(mt(xv, p['w_v'], tn=512, name="mm_v"))
    row_spec = pl.BlockSpec((tm, 512), lambda i, j, k: (i, j))
    vec_spec = pl.BlockSpec((1, 512), lambda i, j, k: (0, j))
    if v_first is not None:
        lv = rows(mt(xv, p['v1'], tn=LANES, name="mm_v1"))
        v = mm(lv, p['v2'], kdim=LANES, tn=512, name="mm_vres",
               epi=(_epi_vres, [v, v_first, p['v0']], [row_spec, row_spec, vec_spec]))
    lw = rows(mt(xw, p['w1'], tn=2 * LANES, act=jnp.tanh, name="mm_w1"))
    la = rows(mt(xa, p['a1'], tn=2 * LANES, name="mm_a1"))
    lg = mm(xg.reshape(m, d), p['g1'], kdim=d, tn=p['g1'].shape[1], out_dtype=BF16, name="mm_g1",
            epi=(jax.nn.sigmoid, [], []))
    g = mm(lg, p['g2'], kdim=p['g2'].shape[0], tn=512, name="mm_g2")
    wp = mm(lw, p['w2'], kdim=2 * LANES, tn=512, name="mm_w2")
    ap = mm(la, p['a2'], kdim=2 * LANES, tn=512, name="mm_a2")
    cs, hv = dm.chan_to_scan, dm.time_view
    y, rk = _rwkv_scan(hv(r), hv(k), hv(v), hv(wp), hv(ap),
                       cs(p['k_k']), cs(p['k_a']), cs(p['r_k']), cs(p['w0']), cs(p['a0']),
                       n_ctx=dm.n_ctx, tblk=32)
    pre = _rwkv_norm(y, hv(v), rk, cs(p['ln_w']), cs(p['ln_b']), tblk=32, eps=RWKV_HEAD * 1e-5)
    return pre, g, v


def _mlstm_layer(dm, xs, g1, sc1, sh1, p):
    m, d, tm = dm.m, dm.d, dm.tm
    nh = MLSTM_HEADS
    u = _matmul(None, p['w_in'], m=m, kdim=d, tm=tm, tn=MLSTM_IN_TN, name="mm_mlstm_in",
                pro=(_pro_normmod, [xs, g1, sc1[0], sh1[0]],
                     [pl.BlockSpec((tm, d), lambda i, j, k: (i, 0)),
                      pl.BlockSpec((1, d), lambda i, j, k: (0, 0)),
                      dm.mod_spec(sc1[1], tm, d), dm.mod_spec(sh1[1], tm, d)]))
    qk = _mlstm_conv(u, p['conv_w'], p['conv_b'], dm=dm, width=d, tc=min(1024, d // 2),
                     q_scale=float(d // 2 // nh) ** -0.5)
    gates = u[:, 3 * d:3 * d + 4 * nh].reshape(m, 2, 2, nh) + p['b_gate']
    gates = GATE_CAP * jnp.tanh(gates / GATE_CAP)
    ig = jnp.moveaxis(gates[:, :, 0, :], 1, 0)
    lf = jnp.moveaxis(jax.nn.log_sigmoid(gates[:, :, 1, :]), 1, 0)
    hs = _mlstm_scan(qk, u, d, ig, lf, nb=dm.nb, n_ctx=dm.n_ctx, n_lat=dm.n_lat)
    return hs, u


def kernel(x, c, ctx, c_ctx, mod_w, mod_b, norm_g, final_g, rwkv_mu, rwkv_w_r, rwkv_w_k, rwkv_w_v, rwkv_w_o, rwkv_w0, rwkv_w1, rwkv_w2, rwkv_a0, rwkv_a1, rwkv_a2, rwkv_g1, rwkv_g2, rwkv_k_k, rwkv_k_a, rwkv_r_k, rwkv_ln_w, rwkv_ln_b, rwkv_v0, rwkv_v1, rwkv_v2, mlstm_w_in, mlstm_b_gate, mlstm_conv_w, mlstm_conv_b, mlstm_norm_w, mlstm_w_out, ffn_w_in, ffn_w_out):
    nb, n_lat, d = x.shape
    n_ctx = ctx.shape[1]
    depth = mod_w.shape[0]
    d_ff = ffn_w_out.shape[1]
    dm = _Dims(nb, n_ctx, n_lat, d)
    m, tm, tmh = dm.m, dm.tm, dm.tm_half
    bf = lambda a: a.astype(BF16)

    cond = jax.nn.silu(jnp.concatenate([c, c_ctx[None, :]], axis=0))
    rows = cond.shape[0]
    cond = _pad_to(cond, 0, 2 * SUBLANES)
    mod_w2 = mod_w.reshape(depth * d, 6 * d)
    mods = []
    for i in range(depth):
        mo = _matmul(cond, mod_w2, m=cond.shape[0], kdim=d, tm=cond.shape[0], tn=512,
                     w_row0=i * d, name="mm_mod") + mod_b[i]
        mods.append(mo[:rows].reshape(rows, 6, 1, d))

    def gate_res(res, gate, tm_, tn_=512):
        return (_epi_gate_res, [res, gate[0]],
                [pl.BlockSpec((tm_, tn_), lambda i, j, k: (i, j)),
                 dm.mod_spec(gate[1], tm_, tn_, with_col=True)])

    xs = jnp.concatenate([ctx.reshape(dm.mc, d), x.reshape(dm.mx, d)], axis=0)
    v_first = None
    for i in range(depth):
        j = i // 2
        sh1, sc1, gt1, sh2, sc2, gt2 = [(mods[i], n) for n in range(6)]
        g1 = norm_g[i, 0][None, :]
        g2 = norm_g[i, 1][None, :]
        full_k = lambda tm_: pl.BlockSpec((tm_, d), lambda i, j, k: (i, 0))
        skip = (lambda tm_: dm.mc // tm_) if i == depth - 1 else (lambda tm_: 0)
        if i % 2 == 0:
            lora = lambda a: bf(jnp.concatenate([_pad_to(a[0], 1, LANES), _pad_to(a[1], 1, LANES)], axis=1))

            def lora_up(a):
                ap_ = _pad_to(a, 1, LANES)
                zero = jnp.zeros_like(ap_[0])
                return bf(jnp.concatenate([jnp.concatenate([ap_[0], zero], axis=1),
                                           jnp.concatenate([zero, ap_[1]], axis=1)], axis=0))

            p = {'mu': rwkv_mu[j], 'w_r': bf(rwkv_w_r[j]), 'w_k': bf(rwkv_w_k[j]), 'w_v': bf(rwkv_w_v[j]),
                 'w0': rwkv_w0[j], 'w1': lora(rwkv_w1[j]), 'w2': lora_up(rwkv_w2[j]),
                 'a0': rwkv_a0[j], 'a1': lora(rwkv_a1[j]), 'a2': lora_up(rwkv_a2[j]),
                 'g1': bf(rwkv_g1[j]), 'g2': bf(rwkv_g2[j]),
                 'k_k': rwkv_k_k[j], 'k_a': rwkv_k_a[j], 'r_k': rwkv_r_k[j].reshape(-1),
                 'ln_w': rwkv_ln_w[j], 'ln_b': rwkv_ln_b[j]}
            if j > 0:
                p['v0'] = rwkv_v0[j - 1][None, :]
                p['v1'] = bf(_pad_to(rwkv_v1[j - 1], 1, LANES))
                p['v2'] = bf(_pad_to(rwkv_v2[j - 1], 0, LANES))
            pre, g, v_cur = _rwkv_layer(dm, xs, g1, sc1, sh1, p, v_first if j > 0 else None)
            if j == 0:
                v_first = v_cur
            tmo = n_ctx
            tm_spec = pl.BlockSpec((None, tmo, d), lambda i, j, k: dm.time_major_block(i)[::-1] + (0,))
            xs, h2 = _matmul(None, bf(rwkv_w_o[j]), m=m, kdim=d, tm=tmo, tn=d, name="mm_rwkv_out",
                             row_block0=skip(tmo),
                             pro=(_pro_mulg, [pre, g.reshape(nb, dm.s_len, d)],
                                  [tm_spec, tm_spec]),
                             epi=gate_res(xs, gt1, tmo, d),
                             post=(_post_normmod, [g2, sc2[0], sh2[0]],
                                   [pl.BlockSpec((1, d), lambda i, j, k: (0, 0)),
                                    dm.mod_spec(sc2[1], tmo, d), dm.mod_spec(sh2[1], tmo, d)], BF16))
        else:
            p = {'w_in': bf(_pad_to(mlstm_w_in[j], 1, MLSTM_IN_TN)),
                 'b_gate': mlstm_b_gate[j], 'conv_w': mlstm_conv_w[j], 'conv_b': mlstm_conv_b[j]}
            hs, u = _mlstm_layer(dm, xs, g1, sc1, sh1, p)
            tmo = n_ctx
            xs, h2 = _matmul(None, bf(mlstm_w_out[j]), m=m, kdim=d, tm=tmo, tn=d, name="mm_mlstm_out",
                             row_block0=skip(tmo),
                             pro=(_pro_mlstm_read, [hs, u, mlstm_norm_w[j][None, :]],
                                  [pl.BlockSpec((2, tmo, d), lambda i, j, k: (0, i, 0)),
                                   pl.BlockSpec((tmo, d), lambda i, j, k: (i, 2)),
                                   pl.BlockSpec((1, d), lambda i, j, k: (0, 0))]),
                             epi=gate_res(xs, gt1, tmo, d),
                             post=(_post_normmod, [g2, sc2[0], sh2[0]],
                                   [pl.BlockSpec((1, d), lambda i, j, k: (0, 0)),
                                    dm.mod_spec(sc2[1], tmo, d), dm.mod_spec(sh2[1], tmo, d)], BF16))
        ffn_pro = None if h2 is not None else (
            _pro_normmod, [xs, g2, sc2[0], sh2[0]],
            [full_k(tm), pl.BlockSpec((1, d), lambda i, j, k: (0, 0)),
             dm.mod_spec(sc2[1], tm, d), dm.mod_spec(sh2[1], tm, d)])
        hid = _matmul(h2, bf(ffn_w_in[i]), m=m, kdim=d, tm=tm, tn=512, out_dtype=BF16,
                      n_out=d_ff, w2_col0=d_ff, name="mm_ffn_in", row_block0=skip(tm),
                      pro=ffn_pro, epi=(_epi_swiglu, [], []))
        xs = _matmul(hid, bf(ffn_w_out[i]), m=m, kdim=d_ff, tm=tm, tn=512, row_block0=skip(tm),
                     name="mm_ffn_out", epi=gate_res(xs, gt2, tm))
    out = _rmsnorm_rows(xs, final_g[None, :], tm=tm, row_block0=dm.mc // tm)
    return out.reshape(nb, n_lat, d)
```
